```python
import math
import jax, jax.numpy as jnp
from jax import lax
import numpy as np

D_MODEL = 1024
BATCH = 16
SEQ = 4096
DEPTH = 4

MEM_LEN = 256
EPS = 1e-6

D_MIX = D_MODEL

MLA_HEADS = 8
MLA_NOPE = 64
MLA_ROPE = 32
MLA_V = 64
MLA_QK = MLA_NOPE + MLA_ROPE
MLA_Q_RANK = 384
MLA_KV_RANK = 256
MLA_WIDTH = MLA_HEADS * MLA_V
ROPE_BASE = 10000.0
Q_BLOCK = 128

HG_HEADS = 4
HG_DK = 64
HG_DV = 64
HG_WIDTH = HG_HEADS * HG_DV
HG_CHUNK = 64

CONV_GROUPS = 4
CONV_WIDTH = D_MIX - MLA_WIDTH - HG_WIDTH
CONV_K = 3

X_HEADS = 4
X_HEAD_DIM = 128

D_FF = 4 * D_MODEL

SPLIT_SIZES = (MLA_Q_RANK, MLA_KV_RANK, MLA_ROPE,
               HG_HEADS * HG_DK, HG_HEADS * HG_DK, HG_HEADS * HG_DV, HG_HEADS * HG_DV,
               CONV_WIDTH, CONV_WIDTH, CONV_WIDTH)
N_IN = sum(SPLIT_SIZES)
SPLIT_POINTS = tuple(int(v) for v in np.cumsum(SPLIT_SIZES)[:-1])

kernel_name = "hybrid_mla_hgrn2_shortconv_block"


def rmsnorm(x, g):
    xf = x.astype(jnp.float32)
    y = xf * lax.rsqrt(jnp.mean(xf * xf, axis=-1, keepdims=True) + EPS) * g.astype(jnp.float32)
    return y.astype(x.dtype)


def rotate(x, cos, sin):
    half = x.shape[-1] // 2
    x1, x2 = x[..., :half], x[..., half:]
    return jnp.concatenate([x1 * cos - x2 * sin, x2 * cos + x1 * sin], axis=-1)


def mla_mixer(cq, ckv, kr, positions, q_norm_g, kv_norm_g, w_uq, w_ukv, qn_g, kn_g):
    B, S = cq.shape[0], cq.shape[1]
    q = (rmsnorm(cq, q_norm_g) @ w_uq).reshape(B, S, MLA_HEADS, MLA_QK)
    kv = (rmsnorm(ckv, kv_norm_g) @ w_ukv).reshape(B, S, MLA_HEADS, MLA_NOPE + MLA_V)
    k_nope, v = kv[..., :MLA_NOPE], kv[..., MLA_NOPE:]
    k = jnp.concatenate([k_nope, jnp.broadcast_to(kr[:, :, None, :], (B, S, MLA_HEADS, MLA_ROPE))], axis=-1)
    q = rmsnorm(q, qn_g)
    k = rmsnorm(k, kn_g)
    inv_freq = ROPE_BASE ** (-jnp.arange(0, MLA_ROPE, 2, dtype=jnp.float32) / MLA_ROPE)
    ang = positions.astype(jnp.float32)[:, None] * inv_freq[None, :]
    cos = jnp.cos(ang)[None, :, None, :].astype(q.dtype)
    sin = jnp.sin(ang)[None, :, None, :].astype(q.dtype)
    q = jnp.concatenate([q[..., :MLA_NOPE], rotate(q[..., MLA_NOPE:], cos, sin)], axis=-1)
    k = jnp.concatenate([k[..., :MLA_NOPE], rotate(k[..., MLA_NOPE:], cos, sin)], axis=-1)
    qh = q.transpose(0, 2, 1, 3)
    kh = k.transpose(0, 2, 1, 3)
    vh = v.transpose(0, 2, 1, 3)
    nb = S // Q_BLOCK
    qb = qh.reshape(B, MLA_HEADS, nb, Q_BLOCK, MLA_QK).transpose(2, 0, 1, 3, 4)
    pb = positions.reshape(nb, Q_BLOCK)
    scale = 1.0 / math.sqrt(MLA_QK)

    def block(args):
        qblk, pblk = args
        s = jnp.einsum('bhqd,bhkd->bhqk', qblk, kh).astype(jnp.float32) * scale
        mask = positions[None, :] <= pblk[:, None]
        s = jnp.where(mask, s, -jnp.inf)
        p = jax.nn.softmax(s, axis=-1).astype(vh.dtype)
        return jnp.einsum('bhqk,bhkd->bhqd', p, vh)

    ob = lax.map(block, (qb, pb))
    return ob.transpose(1, 0, 3, 2, 4).reshape(B, S, MLA_WIDTH)


def hgrn2_mixer(q_in, f_in, i_in, g_in, lb, o_norm_g):
    dt = q_in.dtype
    B, S = q_in.shape[0], q_in.shape[1]
    q = jax.nn.silu(q_in.astype(jnp.float32))
    lbf = jnp.maximum(lb.astype(jnp.float32), 0.0)
    log_f = jnp.logaddexp(jnp.log(lbf), jnp.log1p(-lbf) + jax.nn.log_sigmoid(f_in.astype(jnp.float32)))
    k = -jnp.expm1(log_f)
    v = i_in.astype(jnp.float32)
    nc = S // HG_CHUNK

    def to_chunks(t, d):
        return t.reshape(B, nc, HG_CHUNK, HG_HEADS, d).transpose(1, 0, 3, 2, 4)

    qc, kc, fc, vc = to_chunks(q, HG_DK), to_chunks(k, HG_DK), to_chunks(log_f, HG_DK), to_chunks(v, HG_DV)
    causal = jnp.tril(jnp.ones((HG_CHUNK, HG_CHUNK), dtype=bool))

    def step(state, xs):
        qx, kx, vx, fx = xs
        b = jnp.cumsum(fx, axis=2)
        o_inter = jnp.einsum('bhtd,bhde->bhte', qx * jnp.exp(b), state)
        diff = b[:, :, :, None, :] - b[:, :, None, :, :]
        decay = jnp.exp(jnp.where(causal[:, :, None], diff, -jnp.inf))
        attn = jnp.einsum('bhtd,bhtsd,bhsd->bhts', qx, decay, kx)
        o_intra = jnp.einsum('bhts,bhse->bhte', attn, vx)
        b_last = b[:, :, -1:, :]
        state = (jnp.exp(b_last[:, :, 0, :])[..., None] * state
                 + jnp.einsum('bhsd,bhse->bhde', kx * jnp.exp(b_last - b), vx))
        return state, o_inter + o_intra

    s0 = jnp.zeros((B, HG_HEADS, HG_DK, HG_DV), jnp.float32)
    _, oc = lax.scan(step, s0, (qc, kc, vc, fc))
    o = oc.transpose(1, 0, 3, 2, 4).reshape(B, S, HG_HEADS, HG_DV)
    o = rmsnorm(o, o_norm_g.reshape(HG_HEADS, HG_DV)).reshape(B, S, HG_WIDTH)
    o = o * jax.nn.silu(g_in.astype(jnp.float32))
    return o.astype(dt)


def short_conv_mixer(b_in, c_in, x_in, w_conv):
    u = c_in * x_in
    kern = w_conv[:, None, :].astype(u.dtype)
    y = lax.conv_general_dilated(u, kern, window_strides=(1,), padding=[(CONV_K - 1, 0)],
                                 dimension_numbers=('NWC', 'WIO', 'NWC'),
                                 feature_group_count=CONV_WIDTH)
    return b_in * y


def memory_cross_attention(h, mem_n, w_q, w_kv, qn_g, kn_g, w_o):
    B, S = h.shape[0], h.shape[1]
    M = mem_n.shape[1]
    q = (h @ w_q).reshape(B, S, X_HEADS, X_HEAD_DIM)
    kv = (mem_n @ w_kv).reshape(B, M, 2, X_HEADS, X_HEAD_DIM)
    k, v = kv[:, :, 0], kv[:, :, 1]
    q = rmsnorm(q, qn_g)
    k = rmsnorm(k, kn_g)
    s = jnp.einsum('bshd,bmhd->bhsm', q, k).astype(jnp.float32) * (1.0 / math.sqrt(X_HEAD_DIM))
    p = jax.nn.softmax(s, axis=-1).astype(v.dtype)
    o = jnp.einsum('bhsm,bmhd->bshd', p, v).reshape(B, S, X_HEADS * X_HEAD_DIM)
    return o @ w_o


def setup_inputs(seed: int = 0) -> dict:
    key = jax.random.key(seed)
    ks = iter(jax.random.split(key, 32))

    def nrm(shape, scale):
        return jax.random.normal(next(ks), shape, jnp.float32) * scale

    def gain(shape):
        return 1.0 + 0.02 * jax.random.normal(next(ks), shape, jnp.float32)

    L = DEPTH
    return {
        "x": nrm((BATCH, SEQ, D_MODEL), 1.0),
        "mem": nrm((BATCH, MEM_LEN, D_MODEL), 1.0),
        "positions": jnp.arange(SEQ, dtype=jnp.int32),
        "mix_norm_g": gain((L, D_MODEL)),
        "w_in": nrm((L, D_MODEL, N_IN), D_MODEL ** -0.5),
        "mla_q_norm_g": gain((L, MLA_Q_RANK)),
        "mla_kv_norm_g": gain((L, MLA_KV_RANK)),
        "w_uq": nrm((L, MLA_Q_RANK, MLA_HEADS * MLA_QK), MLA_Q_RANK ** -0.5),
        "w_ukv": nrm((L, MLA_KV_RANK, MLA_HEADS * (MLA_NOPE + MLA_V)), MLA_KV_RANK ** -0.5),
        "mla_qn_g": gain((L, MLA_QK)),
        "mla_kn_g": gain((L, MLA_QK)),
        "hgrn_lb_logits": nrm((L, HG_HEADS * HG_DK), 0.5),
        "hgrn_o_norm_g": gain((L, HG_WIDTH)),
        "conv_w": nrm((L, CONV_K, CONV_WIDTH), CONV_K ** -0.5),
        "w_out": nrm((L, D_MIX, D_MODEL), D_MIX ** -0.5),
        "xattn_norm_g": gain((L, D_MODEL)),
        "mem_norm_g": gain((L, D_MODEL)),
        "w_xq": nrm((L, D_MODEL, X_HEADS * X_HEAD_DIM), D_MODEL ** -0.5),
        "w_xkv": nrm((L, D_MODEL, 2 * X_HEADS * X_HEAD_DIM), D_MODEL ** -0.5),
        "xq_norm_g": gain((L, X_HEAD_DIM)),
        "xk_norm_g": gain((L, X_HEAD_DIM)),
        "w_xo": nrm((L, X_HEADS * X_HEAD_DIM, D_MODEL), (X_HEADS * X_HEAD_DIM) ** -0.5),
        "mlp_norm_g": gain((L, D_MODEL)),
        "w_up": nrm((L, D_MODEL, D_FF), D_MODEL ** -0.5),
        "w_down": nrm((L, D_FF, D_MODEL), D_FF ** -0.5),
    }


def reference(x, mem, positions, mix_norm_g, w_in, mla_q_norm_g, mla_kv_norm_g, w_uq, w_ukv,
              mla_qn_g, mla_kn_g, hgrn_lb_logits, hgrn_o_norm_g, conv_w, w_out,
              xattn_norm_g, mem_norm_g, w_xq, w_xkv, xq_norm_g, xk_norm_g, w_xo,
              mlp_norm_g, w_up, w_down):
    lb_soft = jax.nn.softmax(hgrn_lb_logits.astype(jnp.float32), axis=0)
    lower_bounds = jnp.cumsum(lb_soft, axis=0) - lb_soft[0:1]
    for l in range(DEPTH):
        h = rmsnorm(x, mix_norm_g[l])
        proj = h @ w_in[l]
        cq, ckv, kr, hq, hf, hi, hg, cb, cc, cx = jnp.split(proj, SPLIT_POINTS, axis=-1)
        y_mla = mla_mixer(cq, ckv, kr, positions, mla_q_norm_g[l], mla_kv_norm_g[l],
                          w_uq[l], w_ukv[l], mla_qn_g[l], mla_kn_g[l])
        y_hg = hgrn2_mixer(hq, hf, hi, hg, lower_bounds[l], hgrn_o_norm_g[l])
        y_cv = short_conv_mixer(cb, cc, cx, conv_w[l])
        x = x + jnp.concatenate([y_mla, y_hg, y_cv], axis=-1) @ w_out[l]
        x = x + memory_cross_attention(rmsnorm(x, xattn_norm_g[l]), rmsnorm(mem, mem_norm_g[l]),
                                       w_xq[l], w_xkv[l], xq_norm_g[l], xk_norm_g[l], w_xo[l])
        hm = rmsnorm(x, mlp_norm_g[l])
        x = x + jnp.square(jax.nn.relu(hm @ w_up[l])) @ w_down[l]
    return x
```

```python
import functools
import math

import numpy as np
import jax
import jax.numpy as jnp
from jax import lax
from jax.experimental import pallas as pl
from jax.experimental.pallas import tpu as pltpu

F32 = jnp.float32
BF16 = jnp.bfloat16

D_MODEL = 1024
DEPTH = 4
MEM_LEN = 256
EPS = 1e-6

MLA_HEADS = 8
MLA_NOPE = 64
MLA_ROPE = 32
MLA_V = 64
MLA_QK = MLA_NOPE + MLA_ROPE
MLA_Q_RANK = 384
MLA_KV_RANK = 256
ROPE_BASE = 10000.0

HG_HEADS = 4
HG_DK = 64
HG_CHUNK = 64
HG_SUB = 16
HG_WIDTH = 256
CONV_WIDTH = 256
CONV_K = 3

X_HEADS = 4
X_HEAD_DIM = 128
D_FF = 4 * D_MODEL

LANES = 128
HEAD_SLOT = LANES
ROPE_HALF = MLA_ROPE // 2
N_IN_PAD = MLA_Q_RANK + MLA_KV_RANK + HEAD_SLOT + 4 * HG_WIDTH + 3 * CONV_WIDTH
HC_OFF = MLA_Q_RANK + MLA_KV_RANK + HEAD_SLOT
HC_WIDTH = N_IN_PAD - HC_OFF

ROW_TILE = 512
ATT_TQ = 256
ATT_TK = 256
HG_ROWS = 512
MASK_VALUE = -1e30
EXP_CLAMP = 60.0
VMEM_LIMIT = 56 * 1024 * 1024


def _head_lane_map():
    m = -np.ones((HEAD_SLOT,), np.int64)
    m[0:ROPE_HALF] = MLA_NOPE + np.arange(ROPE_HALF)
    m[ROPE_HALF:64] = np.arange(64 - ROPE_HALF)
    m[64:64 + ROPE_HALF] = MLA_NOPE + ROPE_HALF + np.arange(ROPE_HALF)
    m[64 + ROPE_HALF:96] = (64 - ROPE_HALF) + np.arange(ROPE_HALF)
    return m


_LANE_MAP = _head_lane_map()


def _rms(x, n=None):
    n = x.shape[-1] if n is None else n
    return lax.rsqrt(jnp.sum(x * x, axis=-1, keepdims=True) * (1.0 / n) + EPS)


def _sigmoid(x):
    return 1.0 / (1.0 + jnp.exp(-x))


def _dot(a, b):
    return jnp.dot(a, b, preferred_element_type=F32)


def _dot_nt(a, b):
    return lax.dot_general(a, b, (((1,), (1,)), ((), ())), preferred_element_type=F32)


def _dot_tn(a, b):
    return lax.dot_general(a, b, (((0,), (0,)), ((), ())), preferred_element_type=F32)


def _in_proj_kernel(x_ref, g_ref, w_in_ref, gq_ref, w_uq_ref, gkv_ref, w_ukv_ref,
                    cq_tab, sq_tab, ck_tab, sk_tab,
                    q_out, k_out, v_out, hc_out):
    x = x_ref[0]
    h = x * _rms(x) * g_ref[...]
    proj = _dot(h.astype(BF16), w_in_ref[...])
    hc_out[0] = proj[:, HC_OFF:]
    cq = proj[:, :MLA_Q_RANK]
    ckv = proj[:, MLA_Q_RANK:MLA_Q_RANK + MLA_KV_RANK]
    kr = proj[:, MLA_Q_RANK + MLA_KV_RANK:HC_OFF]
    cqn = cq * _rms(cq) * gq_ref[...]
    qf = _dot(cqn.astype(BF16), w_uq_ref[...])
    ckvn = ckv * _rms(ckv) * gkv_ref[...]
    kvf = _dot(ckvn.astype(BF16), w_ukv_ref[...])
    v_out[0] = kvf[:, MLA_HEADS * HEAD_SLOT:].astype(BF16)
    cq_t, sq_t, ck_t, sk_t = cq_tab[...], sq_tab[...], ck_tab[...], sk_tab[...]
    for hd in range(MLA_HEADS):
        sl = slice(hd * HEAD_SLOT, (hd + 1) * HEAD_SLOT)
        qh = qf[:, sl]
        qh = (qh * cq_t + pltpu.roll(qh, 64, 1) * sq_t) * _rms(qh, MLA_QK)
        q_out[0, hd] = qh.astype(BF16)
        kh = kvf[:, sl] + kr
        kh = (kh * ck_t + pltpu.roll(kh, 64, 1) * sk_t) * _rms(kh, MLA_QK)
        k_out[0, hd] = kh.astype(BF16)


def _in_proj(x, g, w_in, gq, w_uq, gkv, w_ukv, tabs):
    B, S, _ = x.shape
    tm = ROW_TILE
    const = lambda b, s: (0, 0)
    tab_spec = pl.BlockSpec((tm, HEAD_SLOT), lambda b, s: (s, 0))
    return pl.pallas_call(
        _in_proj_kernel,
        grid=(B, S // tm),
        in_specs=[
            pl.BlockSpec((1, tm, D_MODEL), lambda b, s: (b, s, 0)),
            pl.BlockSpec((1, D_MODEL), const),
            pl.BlockSpec((D_MODEL, N_IN_PAD), const),
            pl.BlockSpec((1, MLA_Q_RANK), const),
            pl.BlockSpec((MLA_Q_RANK, MLA_HEADS * HEAD_SLOT), const),
            pl.BlockSpec((1, MLA_KV_RANK), const),
            pl.BlockSpec((MLA_KV_RANK, MLA_HEADS * HEAD_SLOT + MLA_HEADS * MLA_V), const),
            tab_spec, tab_spec, tab_spec, tab_spec,
        ],
        out_specs=[
            pl.BlockSpec((1, MLA_HEADS, tm, HEAD_SLOT), lambda b, s: (b, 0, s, 0)),
            pl.BlockSpec((1, MLA_HEADS, tm, HEAD_SLOT), lambda b, s: (b, 0, s, 0)),
            pl.BlockSpec((1, tm, MLA_HEADS * MLA_V), lambda b, s: (b, s, 0)),
            pl.BlockSpec((1, tm, HC_WIDTH), lambda b, s: (b, s, 0)),
        ],
        out_shape=[
            jax.ShapeDtypeStruct((B, MLA_HEADS, S, HEAD_SLOT), BF16),
            jax.ShapeDtypeStruct((B, MLA_HEADS, S, HEAD_SLOT), BF16),
            jax.ShapeDtypeStruct((B, S, MLA_HEADS * MLA_V), BF16),
            jax.ShapeDtypeStruct((B, S, HC_WIDTH), F32),
        ],
        compiler_params=pltpu.CompilerParams(
            dimension_semantics=("parallel", "parallel"), vmem_limit_bytes=VMEM_LIMIT),
        name="in_proj",
    )(x, g, w_in, gq, w_uq, gkv, w_ukv, *tabs)


def _attn_kernel(nkv_ref, q_ref, k_ref, v_ref, posq_ref, posk_ref, o_ref, m_sc, l_sc, acc_sc):
    qi = pl.program_id(2)
    n_kv = nkv_ref[qi]
    m_sc[...] = jnp.full(m_sc.shape, MASK_VALUE, F32)
    l_sc[...] = jnp.zeros(l_sc.shape, F32)
    acc_sc[...] = jnp.zeros(acc_sc.shape, F32)
    posq = posq_ref[...]

    def body(ki, carry):
        off = pl.multiple_of(ki * ATT_TK, ATT_TK)
        mask = posk_ref[ki] <= posq
        vt = v_ref[0, pl.ds(off, ATT_TK), :]
        for hd in range(2):
            kt = k_ref[0, hd, pl.ds(off, ATT_TK), :]
            s = _dot_nt(q_ref[0, hd], kt)
            s = jnp.where(mask, s, MASK_VALUE)
            m_old = m_sc[hd]
            m_new = jnp.maximum(m_old, jnp.max(s, axis=1, keepdims=True))
            alpha = jnp.exp(m_old - m_new)
            p = jnp.exp(s - m_new)
            l_sc[hd] = alpha * l_sc[hd] + jnp.sum(p, axis=1, keepdims=True)
            acc_sc[hd] = alpha * acc_sc[hd] + _dot(p.astype(BF16), vt)
            m_sc[hd] = m_new
        return carry

    lax.fori_loop(0, n_kv, body, 0)
    o0 = acc_sc[0] / l_sc[0]
    o1 = acc_sc[1] / l_sc[1]
    lane = lax.broadcasted_iota(jnp.int32, o0.shape, 1)
    o_ref[0] = jnp.where(lane < MLA_V, o0, o1).astype(BF16)


def _attention(q, k, v, n_kv, posq, posk):
    B, H, S, _ = q.shape
    tq = ATT_TQ
    grid_spec = pltpu.PrefetchScalarGridSpec(
        num_scalar_prefetch=1,
        grid=(B, H // 2, S // tq),
        in_specs=[
            pl.BlockSpec((1, 2, tq, HEAD_SLOT), lambda b, hp, qi, n: (b, hp, qi, 0)),
            pl.BlockSpec((1, 2, S, HEAD_SLOT), lambda b, hp, qi, n: (b, hp, 0, 0)),
            pl.BlockSpec((1, S, 2 * MLA_V), lambda b, hp, qi, n: (b, 0, hp)),
            pl.BlockSpec((tq, 1), lambda b, hp, qi, n: (qi, 0)),
            pl.BlockSpec((S // ATT_TK, 1, ATT_TK), lambda b, hp, qi, n: (0, 0, 0)),
        ],
        out_specs=pl.BlockSpec((1, tq, 2 * MLA_V), lambda b, hp, qi, n: (b, qi, hp)),
        scratch_shapes=[
            pltpu.VMEM((2, tq, 1), F32),
            pltpu.VMEM((2, tq, 1), F32),
            pltpu.VMEM((2, tq, 2 * MLA_V), F32),
        ],
    )
    return pl.pallas_call(
        _attn_kernel,
        grid_spec=grid_spec,
        out_shape=jax.ShapeDtypeStruct((B, S, H * MLA_V), BF16),
        compiler_params=pltpu.CompilerParams(
            dimension_semantics=("parallel", "parallel", "arbitrary"),
            vmem_limit_bytes=VMEM_LIMIT),
        name="mla_attention",
    )(n_kv, q, k, v, posq, posk)


def _hgrn_conv_kernel(hc_ref, lbl_ref, onorm_ref, convw_ref, y_ref,
                      st_sc, ubuf_sc, o_sc, *, layer):
    sb = pl.program_id(1)
    R = HG_ROWS
    C = HG_CHUNK
    W = HG_WIDTH

    @pl.when(sb == 0)
    def _():
        st_sc[...] = jnp.zeros(st_sc.shape, F32)
        ubuf_sc[0:8, :] = jnp.zeros((8, CONV_WIDTH), F32)

    lg = lbl_ref[...]
    e = jnp.exp(lg - jnp.max(lg, axis=0, keepdims=True))
    soft = e / jnp.sum(e, axis=0, keepdims=True)
    lb = jnp.zeros((1, W), F32)
    for i in range(1, layer + 1):
        lb = lb + soft[i:i + 1, :]
    lb = jnp.maximum(lb, 0.0)
    log_lb = jnp.log(lb)
    log_1m = jnp.log1p(-lb)

    ri = lax.broadcasted_iota(jnp.int32, (C, C), 0)
    ci = lax.broadcasted_iota(jnp.int32, (C, C), 1)
    sub_shift = HG_SUB.bit_length() - 1
    dk_shift = HG_DK.bit_length() - 1
    tri = jnp.where(ci <= ri, 1.0, 0.0)
    tri_blk = jnp.where((ci >> sub_shift) == (ri >> sub_shift), tri, 0.0)
    cum_mat = jnp.concatenate([tri, tri_blk], axis=0).astype(BF16)
    tt = ri & (HG_SUB - 1)
    lane_w = lax.broadcasted_iota(jnp.int32, (1, W), 1)
    head_masks = [jnp.where((lane_w >> dk_shift) == hh, 1.0, 0.0) for hh in range(HG_HEADS)]
    rw = lax.broadcasted_iota(jnp.int32, (W, W), 0)
    cw = lax.broadcasted_iota(jnp.int32, (W, W), 1)
    bd_mask = (rw >> dk_shift) == (cw >> dk_shift)
    n_sub = C // HG_SUB

    def chunk(c, carry):
        r0 = pl.multiple_of(c * C, C)
        zq = hc_ref[0, pl.ds(r0, C), 0:W]
        zf = hc_ref[0, pl.ds(r0, C), W:2 * W]
        vi = hc_ref[0, pl.ds(r0, C), 2 * W:3 * W]
        q = zq * _sigmoid(zq)
        log_sig = jnp.minimum(zf, 0.0) - jnp.log1p(jnp.exp(-jnp.abs(zf)))
        c2 = log_1m + log_sig
        log_f = jnp.maximum(log_lb, c2) + jnp.log1p(jnp.exp(-jnp.abs(log_lb - c2)))
        kk = (1.0 - lb) * _sigmoid(-zf)
        f_hi = log_f.astype(BF16)
        r1 = log_f - f_hi.astype(F32)
        f_mid = r1.astype(BF16)
        f_lo = (r1 - f_mid.astype(F32)).astype(BF16)
        cum = _dot(cum_mat, f_hi) + _dot(cum_mat, f_mid) + _dot(cum_mat, f_lo)
        b = cum[:C]
        g = cum[C:]
        b_last = b[C - 1:C, :]
        rref = b - g
        qd = q * jnp.exp(g)
        vb = vi.astype(BF16)
        st = st_sc[...]
        o_inter = _dot_nt((q * jnp.exp(b)).astype(BF16), st.astype(BF16))
        o_parts = []
        for i in range(n_sub):
            rr = rref[i * HG_SUB:i * HG_SUB + 1, :]
            kd = kk * jnp.exp(jnp.minimum(rr - b, EXP_CLAMP))
            qi = qd[i * HG_SUB:(i + 1) * HG_SUB]
            lhs = jnp.concatenate([qi * hm for hm in head_masks], axis=0)
            a = _dot_nt(lhs.astype(BF16), kd.astype(BF16))
            a = jnp.where(ci <= i * HG_SUB + tt, a, 0.0)
            pv = _dot(a.astype(BF16), vb)
            oi = pv[0:HG_SUB] * head_masks[0]
            for hh in range(1, HG_HEADS):
                oi = oi + pv[hh * HG_SUB:(hh + 1) * HG_SUB] * head_masks[hh]
            o_parts.append(oi)
        o_sc[pl.ds(r0, C), :] = o_inter + jnp.concatenate(o_parts, axis=0)
        kdl = kk * jnp.exp(b_last - b)
        upd = _dot_tn(vb, kdl.astype(BF16))
        st_sc[...] = st * jnp.exp(b_last) + jnp.where(bd_mask, upd, 0.0)
        return carry

    lax.fori_loop(0, R // C, chunk, 0)

    o = o_sc[...]
    o2 = o * o
    o2_hi = o2.astype(BF16)
    o2_lo = (o2 - o2_hi.astype(F32)).astype(BF16)
    ones_bd = jnp.where(bd_mask, 1.0, 0.0).astype(BF16)
    ms = (_dot(o2_hi, ones_bd) + _dot(o2_lo, ones_bd)) * (1.0 / HG_DK)
    zg = hc_ref[0, :, 3 * W:4 * W]
    y_hg = o * lax.rsqrt(ms + EPS) * onorm_ref[...] * (zg * _sigmoid(zg))
    y_ref[0, :, 0:W] = y_hg.astype(BF16)

    cb = hc_ref[0, :, 4 * W:4 * W + CONV_WIDTH]
    cc = hc_ref[0, :, 4 * W + CONV_WIDTH:4 * W + 2 * CONV_WIDTH]
    cx = hc_ref[0, :, 4 * W + 2 * CONV_WIDTH:4 * W + 3 * CONV_WIDTH]
    u = cc * cx
    ubuf_sc[8:8 + R, :] = u
    u1 = ubuf_sc[7:7 + R, :]
    u2 = ubuf_sc[6:6 + R, :]
    wc = convw_ref[...]
    y_cv = cb * (u2 * wc[0:1, :] + u1 * wc[1:2, :] + u * wc[2:3, :])
    y_ref[0, :, W:W + CONV_WIDTH] = y_cv.astype(BF16)
    ubuf_sc[0:8, :] = ubuf_sc[R:R + 8, :]


def _hgrn_conv(hc, lb_logits, onorm_g, conv_w, layer):
    B, S, _ = hc.shape
    R = HG_ROWS
    const = lambda b, s: (0, 0)
    return pl.pallas_call(
        functools.partial(_hgrn_conv_kernel, layer=layer),
        grid=(B, S // R),
        in_specs=[
            pl.BlockSpec((1, R, HC_WIDTH), lambda b, s: (b, s, 0)),
            pl.BlockSpec((DEPTH, HG_WIDTH), const),
            pl.BlockSpec((1, HG_WIDTH), const),
            pl.BlockSpec((CONV_K, CONV_WIDTH), const),
        ],
        out_specs=pl.BlockSpec((1, R, HG_WIDTH + CONV_WIDTH), lambda b, s: (b, s, 0)),
        out_shape=jax.ShapeDtypeStruct((B, S, HG_WIDTH + CONV_WIDTH), BF16),
        scratch_shapes=[
            pltpu.VMEM((HG_WIDTH, HG_WIDTH), F32),
            pltpu.VMEM((R + 8, CONV_WIDTH), F32),
            pltpu.VMEM((R, HG_WIDTH), F32),
        ],
        compiler_params=pltpu.CompilerParams(
            dimension_semantics=("parallel", "arbitrary"), vmem_limit_bytes=VMEM_LIMIT),
        name="hgrn_conv",
    )(hc, lb_logits, onorm_g, conv_w)


def _mem_kv_kernel(mem_ref, g_ref, w_ref, gk_ref, k_out, v_out):
    m = mem_ref[0]
    mn = m * _rms(m) * g_ref[0]
    kv = _dot(mn.astype(BF16), w_ref[0])
    hw = X_HEADS * X_HEAD_DIM
    gk = gk_ref[0]
    for hd in range(X_HEADS):
        sl = slice(hd * X_HEAD_DIM, (hd + 1) * X_HEAD_DIM)
        kh = kv[:, sl]
        k_out[0, 0, :, sl] = (kh * _rms(kh) * gk).astype(BF16)
    v_out[0, 0] = kv[:, hw:].astype(BF16)


def _mem_kv(mem, mem_norm_g, w_xkv, xk_norm_g):
    B, M, _ = mem.shape
    L = w_xkv.shape[0]
    hw = X_HEADS * X_HEAD_DIM
    out_spec = pl.BlockSpec((1, 1, M, hw), lambda l, b: (l, b, 0, 0))
    return pl.pallas_call(
        _mem_kv_kernel,
        grid=(L, B),
        in_specs=[
            pl.BlockSpec((1, M, D_MODEL), lambda l, b: (b, 0, 0)),
            pl.BlockSpec((1, 1, D_MODEL), lambda l, b: (l, 0, 0)),
            pl.BlockSpec((1, D_MODEL, 2 * hw), lambda l, b: (l, 0, 0)),
            pl.BlockSpec((1, 1, X_HEAD_DIM), lambda l, b: (l, 0, 0)),
        ],
        out_specs=[out_spec, out_spec],
        out_shape=[jax.ShapeDtypeStruct((L, B, M, hw), BF16)] * 2,
        compiler_params=pltpu.CompilerParams(
            dimension_semantics=("parallel", "parallel"), vmem_limit_bytes=VMEM_LIMIT),
        name="mem_kv",
    )(mem, mem_norm_g, w_xkv, xk_norm_g)


def _out_xattn_kernel(x_ref, ya_ref, yb_ref, wo_ref, g_ref, wq_ref, gq_ref,
                      k_ref, v_ref, wxo_ref, o_ref):
    half = wo_ref.shape[0] // 2
    x1 = x_ref[0] + _dot(ya_ref[0], wo_ref[:half, :]) + _dot(yb_ref[0], wo_ref[half:, :])
    h = x1 * _rms(x1) * g_ref[...]
    q = _dot(h.astype(BF16), wq_ref[...])
    gq = gq_ref[...] * (1.0 / math.sqrt(X_HEAD_DIM))
    outs = []
    for hd in range(X_HEADS):
        sl = slice(hd * X_HEAD_DIM, (hd + 1) * X_HEAD_DIM)
        qh = q[:, sl]
        qh = (qh * _rms(qh) * gq).astype(BF16)
        s = _dot_nt(qh, k_ref[0, 0, :, sl])
        p = jnp.exp(s - jnp.max(s, axis=1, keepdims=True))
        l = jnp.sum(p, axis=1, keepdims=True)
        outs.append(_dot(p.astype(BF16), v_ref[0, 0, :, sl]) / l)
    o = jnp.concatenate(outs, axis=1).astype(BF16)
    o_ref[0] = x1 + _dot(o, wxo_ref[...])


def _out_xattn(x, y_mla, y_hc, w_out, g, w_xq, gq, k_mem, v_mem, w_xo, layer):
    B, S, _ = x.shape
    tm = ROW_TILE
    hw = X_HEADS * X_HEAD_DIM
    const = lambda b, s: (0, 0)
    row = lambda w: pl.BlockSpec((1, tm, w), lambda b, s: (b, s, 0))
    mem_spec = pl.BlockSpec((1, 1, MEM_LEN, hw), lambda b, s: (layer, b, 0, 0))
    return pl.pallas_call(
        _out_xattn_kernel,
        grid=(B, S // tm),
        in_specs=[
            row(D_MODEL), row(y_mla.shape[-1]), row(y_hc.shape[-1]),
            pl.BlockSpec((D_MODEL, D_MODEL), const),
            pl.BlockSpec((1, D_MODEL), const),
            pl.BlockSpec((D_MODEL, hw), const),
            pl.BlockSpec((1, X_HEAD_DIM), const),
            mem_spec, mem_spec,
            pl.BlockSpec((hw, D_MODEL), const),
        ],
        out_specs=row(D_MODEL),
        out_shape=jax.ShapeDtypeStruct((B, S, D_MODEL), F32),
        compiler_params=pltpu.CompilerParams(
            dimension_semantics=("parallel", "parallel"), vmem_limit_bytes=VMEM_LIMIT),
        name="out_xattn",
    )(x, y_mla, y_hc, w_out, g, w_xq, gq, k_mem, v_mem, w_xo)


def _mlp_kernel(x_ref, g_ref, wu_ref, wd_ref, o_ref):
    x = x_ref[0]
    h = (x * _rms(x) * g_ref[...]).astype(BF16)
    acc = x
    step = D_MODEL
    for c in range(D_FF // step):
        u = _dot(h, wu_ref[:, c * step:(c + 1) * step])
        a = jnp.square(jnp.maximum(u, 0.0)).astype(BF16)
        acc = acc + _dot(a, wd_ref[c * step:(c + 1) * step, :])
    o_ref[0] = acc


def _mlp(x, g, w_up, w_down):
    B, S, _ = x.shape
    tm = ROW_TILE
    const = lambda b, s: (0, 0)
    row = pl.BlockSpec((1, tm, D_MODEL), lambda b, s: (b, s, 0))
    return pl.pallas_call(
        _mlp_kernel,
        grid=(B, S // tm),
        in_specs=[row, pl.BlockSpec((1, D_MODEL), const),
                  pl.BlockSpec((D_MODEL, D_FF), const),
                  pl.BlockSpec((D_FF, D_MODEL), const)],
        out_specs=row,
        out_shape=jax.ShapeDtypeStruct((B, S, D_MODEL), F32),
        compiler_params=pltpu.CompilerParams(
            dimension_semantics=("parallel", "parallel"), vmem_limit_bytes=VMEM_LIMIT),
        name="mlp",
    )(x, g, w_up, w_down)


def _scatter_lanes(w_cols):
    idx = jnp.asarray(np.where(_LANE_MAP >= 0, _LANE_MAP, 0), jnp.int32)
    valid = jnp.asarray(_LANE_MAP >= 0)
    return jnp.where(valid, jnp.take(w_cols, idx, axis=-1), 0.0)


def _prep_layer(l, positions, w_in, w_uq, w_ukv, mla_qn_g, mla_kn_g):
    o_kr = MLA_Q_RANK + MLA_KV_RANK
    w = w_in[l]
    kr_src = jnp.concatenate(
        [jnp.zeros((D_MODEL, MLA_NOPE), F32), w[:, o_kr:o_kr + MLA_ROPE]], axis=1)
    w_in_p = jnp.concatenate(
        [w[:, :o_kr], _scatter_lanes(kr_src), w[:, o_kr + MLA_ROPE:]], axis=1).astype(BF16)

    wq = w_uq[l].reshape(MLA_Q_RANK, MLA_HEADS, MLA_QK)
    w_uq_p = _scatter_lanes(wq).reshape(MLA_Q_RANK, MLA_HEADS * HEAD_SLOT).astype(BF16)

    wkv = w_ukv[l].reshape(MLA_KV_RANK, MLA_HEADS, MLA_NOPE + MLA_V)
    k_src = jnp.concatenate(
        [wkv[..., :MLA_NOPE], jnp.zeros((MLA_KV_RANK, MLA_HEADS, MLA_ROPE), F32)], axis=-1)
    w_k_p = _scatter_lanes(k_src).reshape(MLA_KV_RANK, MLA_HEADS * HEAD_SLOT)
    w_v = wkv[..., MLA_NOPE:].reshape(MLA_KV_RANK, MLA_HEADS * MLA_V)
    w_ukv_p = jnp.concatenate([w_k_p, w_v], axis=1).astype(BF16)

    inv_freq = ROPE_BASE ** (-jnp.arange(0, MLA_ROPE, 2, dtype=F32) / MLA_ROPE)
    ang = positions.astype(F32)[:, None] * inv_freq[None, :]
    cos, sin = jnp.cos(ang), jnp.sin(ang)
    S = positions.shape[0]
    pad = HEAD_SLOT // 2 - ROPE_HALF
    c_tab = jnp.concatenate([cos, jnp.ones((S, pad), F32), cos, jnp.ones((S, pad), F32)], axis=1)
    s_tab = jnp.concatenate([-sin, jnp.zeros((S, pad), F32), sin, jnp.zeros((S, pad), F32)], axis=1)

    def tables(gain, scale):
        g = _scatter_lanes(gain)[None, :] * scale
        return c_tab * g, s_tab * jnp.roll(g, HEAD_SLOT // 2, axis=1)

    cq_t, sq_t = tables(mla_qn_g[l], 1.0 / math.sqrt(MLA_QK))
    ck_t, sk_t = tables(mla_kn_g[l], 1.0)
    return w_in_p, w_uq_p, w_ukv_p, (cq_t, sq_t, ck_t, sk_t)


def kernel(x, mem, positions, mix_norm_g, w_in, mla_q_norm_g, mla_kv_norm_g, w_uq, w_ukv,
           mla_qn_g, mla_kn_g, hgrn_lb_logits, hgrn_o_norm_g, conv_w, w_out,
           xattn_norm_g, mem_norm_g, w_xq, w_xkv, xq_norm_g, xk_norm_g, w_xo,
           mlp_norm_g, w_up, w_down):
    B, S, _ = x.shape
    L = w_in.shape[0]
    nq, nk = S // ATT_TQ, S // ATT_TK
    pos_q_max = jnp.max(positions.reshape(nq, ATT_TQ), axis=1)
    pos_k_min = jnp.min(positions.reshape(nk, ATT_TK), axis=1)
    vis = pos_k_min[None, :] <= pos_q_max[:, None]
    n_kv = jnp.max(jnp.where(vis, jnp.arange(1, nk + 1, dtype=jnp.int32)[None, :], 0),
                   axis=1).astype(jnp.int32)
    posq = positions.reshape(S, 1)
    posk = positions.reshape(nk, 1, ATT_TK)

    k_mem, v_mem = _mem_kv(mem, mem_norm_g.reshape(L, 1, D_MODEL), w_xkv.astype(BF16),
                           xk_norm_g.reshape(L, 1, X_HEAD_DIM))
    for l in range(L):
        w_in_p, w_uq_p, w_ukv_p, tabs = _prep_layer(l, positions, w_in, w_uq, w_ukv,
                                                    mla_qn_g, mla_kn_g)
        q, k, v, hc = _in_proj(x, mix_norm_g[l][None], w_in_p, mla_q_norm_g[l][None], w_uq_p,
                               mla_kv_norm_g[l][None], w_ukv_p, tabs)
        y_mla = _attention(q, k, v, n_kv, posq, posk)
        y_hc = _hgrn_conv(hc, hgrn_lb_logits, hgrn_o_norm_g[l][None], conv_w[l], l)
        x = _out_xattn(x, y_mla, y_hc, w_out[l].astype(BF16), xattn_norm_g[l][None],
                       w_xq[l].astype(BF16), xq_norm_g[l][None], k_mem, v_mem,
                       w_xo[l].astype(BF16), l)
        x = _mlp(x, mlp_norm_g[l][None], w_up[l].astype(BF16), w_down[l].astype(BF16))
    return x
```

```python
import functools
import math

import numpy as np
import jax
import jax.numpy as jnp
from jax import lax
from jax.experimental import pallas as pl
from jax.experimental.pallas import tpu as pltpu

F32 = jnp.float32
BF16 = jnp.bfloat16

D_MODEL = 1024
DEPTH = 4
MEM_LEN = 256
EPS = 1e-6

MLA_HEADS = 8
MLA_NOPE = 64
MLA_ROPE = 32
MLA_V = 64
MLA_QK = MLA_NOPE + MLA_ROPE
MLA_Q_RANK = 384
MLA_KV_RANK = 256
ROPE_BASE = 10000.0

HG_HEADS = 4
HG_DK = 64
HG_CHUNK = 64
HG_SUB = 16
HG_WIDTH = 256
CONV_WIDTH = 256
CONV_K = 3

X_HEADS = 4
X_HEAD_DIM = 128
D_FF = 4 * D_MODEL

LANES = 128
HEAD_SLOT = LANES
ROPE_HALF = MLA_ROPE // 2
N_IN_PAD = MLA_Q_RANK + MLA_KV_RANK + HEAD_SLOT + 4 * HG_WIDTH + 3 * CONV_WIDTH
HC_OFF = MLA_Q_RANK + MLA_KV_RANK + HEAD_SLOT
HC_WIDTH = N_IN_PAD - HC_OFF

ROW_TILE = 512
ATT_TQ = 256
ATT_TK = 256
ATT_HEADS = 4
HG_ROWS = 512
MASK_VALUE = -1e30
EXP_CLAMP = 60.0
VMEM_LIMIT = 56 * 1024 * 1024


def _head_lane_map():
    m = -np.ones((HEAD_SLOT,), np.int64)
    m[0:ROPE_HALF] = MLA_NOPE + np.arange(ROPE_HALF)
    m[ROPE_HALF:64] = np.arange(64 - ROPE_HALF)
    m[64:64 + ROPE_HALF] = MLA_NOPE + ROPE_HALF + np.arange(ROPE_HALF)
    m[64 + ROPE_HALF:96] = (64 - ROPE_HALF) + np.arange(ROPE_HALF)
    return m


_LANE_MAP = _head_lane_map()


def _rms(x, n=None):
    n = x.shape[-1] if n is None else n
    return lax.rsqrt(jnp.sum(x * x, axis=-1, keepdims=True) * (1.0 / n) + EPS)


def _sigmoid(x):
    return 1.0 / (1.0 + jnp.exp(-x))


def _dot(a, b):
    return jnp.dot(a, b, preferred_element_type=F32)


def _dot_nt(a, b):
    return lax.dot_general(a, b, (((1,), (1,)), ((), ())), preferred_element_type=F32)


def _dot_tn(a, b):
    return lax.dot_general(a, b, (((0,), (0,)), ((), ())), preferred_element_type=F32)


def _in_proj_kernel(x_ref, g_ref, w_in_ref, gq_ref, w_uq_ref, gkv_ref, w_uk_ref, w_vt_ref,
                    cq_tab, sq_tab, ck_tab, sk_tab,
                    q_out, k_out, vt_out, hc_out):
    x = x_ref[0]
    h = x * _rms(x) * g_ref[...]
    proj = _dot(h.astype(BF16), w_in_ref[...])
    hc_out[0] = proj[:, HC_OFF:]
    cq = proj[:, :MLA_Q_RANK]
    ckv = proj[:, MLA_Q_RANK:MLA_Q_RANK + MLA_KV_RANK]
    kr = proj[:, MLA_Q_RANK + MLA_KV_RANK:HC_OFF]
    cqn = cq * _rms(cq) * gq_ref[...]
    qf = _dot(cqn.astype(BF16), w_uq_ref[...])
    ckvn = (ckv * _rms(ckv) * gkv_ref[...]).astype(BF16)
    kvf = _dot(ckvn, w_uk_ref[...])
    vt = _dot_nt(w_vt_ref[...], ckvn).astype(BF16)
    for j in range(vt_out.shape[1]):
        vt_out[0, j] = vt[:, j * ATT_TK:(j + 1) * ATT_TK]
    cq_t, sq_t, ck_t, sk_t = cq_tab[...], sq_tab[...], ck_tab[...], sk_tab[...]
    for hd in range(MLA_HEADS):
        sl = slice(hd * HEAD_SLOT, (hd + 1) * HEAD_SLOT)
        qh = qf[:, sl]
        qh = (qh * cq_t + pltpu.roll(qh, 64, 1) * sq_t) * _rms(qh, MLA_QK)
        q_out[0, hd] = qh.astype(BF16)
        kh = kvf[:, sl] + kr
        kh = (kh * ck_t + pltpu.roll(kh, 64, 1) * sk_t) * _rms(kh, MLA_QK)
        k_out[0, hd] = kh.astype(BF16)


def _in_proj(x, g, w_in, gq, w_uq, gkv, w_uk, w_vt, tabs):
    B, S, _ = x.shape
    tm = ROW_TILE
    vw = MLA_HEADS * MLA_V
    const = lambda b, s: (0, 0)
    tab_spec = pl.BlockSpec((tm, HEAD_SLOT), lambda b, s: (s, 0))
    return pl.pallas_call(
        _in_proj_kernel,
        grid=(B, S // tm),
        in_specs=[
            pl.BlockSpec((1, tm, D_MODEL), lambda b, s: (b, s, 0)),
            pl.BlockSpec((1, D_MODEL), const),
            pl.BlockSpec((D_MODEL, N_IN_PAD), const),
            pl.BlockSpec((1, MLA_Q_RANK), const),
            pl.BlockSpec((MLA_Q_RANK, MLA_HEADS * HEAD_SLOT), const),
            pl.BlockSpec((1, MLA_KV_RANK), const),
            pl.BlockSpec((MLA_KV_RANK, MLA_HEADS * HEAD_SLOT), const),
            pl.BlockSpec((vw, MLA_KV_RANK), const),
            tab_spec, tab_spec, tab_spec, tab_spec,
        ],
        out_specs=[
            pl.BlockSpec((1, MLA_HEADS, tm, HEAD_SLOT), lambda b, s: (b, 0, s, 0)),
            pl.BlockSpec((1, MLA_HEADS, tm, HEAD_SLOT), lambda b, s: (b, 0, s, 0)),
            pl.BlockSpec((1, tm // ATT_TK, vw, ATT_TK), lambda b, s: (b, s, 0, 0)),
            pl.BlockSpec((1, tm, HC_WIDTH), lambda b, s: (b, s, 0)),
        ],
        out_shape=[
            jax.ShapeDtypeStruct((B, MLA_HEADS, S, HEAD_SLOT), BF16),
            jax.ShapeDtypeStruct((B, MLA_HEADS, S, HEAD_SLOT), BF16),
            jax.ShapeDtypeStruct((B, S // ATT_TK, vw, ATT_TK), BF16),
            jax.ShapeDtypeStruct((B, S, HC_WIDTH), F32),
        ],
        compiler_params=pltpu.CompilerParams(
            dimension_semantics=("parallel", "parallel"), vmem_limit_bytes=VMEM_LIMIT),
        name="in_proj",
    )(x, g, w_in, gq, w_uq, gkv, w_uk, w_vt, *tabs)


def _attn_kernel(nfull_ref, nkv_ref, q_ref, k_ref, vt_ref, posq_ref, posk_ref, o_ref,
                 s_sc, p_sc, acc_sc):
    qi = pl.program_id(2)
    n_full = nfull_ref[qi]
    n_kv = nkv_ref[qi]
    tq = q_ref.shape[2]
    nk = vt_ref.shape[1]
    qts = [q_ref[0, hd].astype(F32).T.astype(BF16) for hd in range(ATT_HEADS)]
    posq = posq_ref[0]

    def scores(ki):
        off = pl.multiple_of(ki * ATT_TK, ATT_TK)
        return [_dot(k_ref[0, hd, pl.ds(off, ATT_TK), :], qts[hd]) for hd in range(ATT_HEADS)]

    def pv(ki, hd, p):
        return _dot(vt_ref[0, ki, hd * MLA_V:(hd + 1) * MLA_V, :], p)

    def step(ki, carry, masked):
        s_next = scores(jnp.minimum(ki + 1, nk - 1))
        pvs = [pv(jnp.maximum(ki - 1, 0), hd, p_sc[hd]) for hd in range(ATT_HEADS)]
        if masked:
            off = pl.multiple_of(ki * ATT_TK, ATT_TK)
            mask = posk_ref[pl.ds(off, ATT_TK), :] <= posq
        new = []
        for hd in range(ATT_HEADS):
            m_old, l_old = carry[hd]
            s = s_sc[hd]
            if masked:
                s = jnp.where(mask, s, MASK_VALUE)
            m_new = jnp.maximum(m_old, jnp.max(s, axis=0, keepdims=True))
            alpha = jnp.exp2(m_old - m_new)
            p = jnp.exp2(s - m_new)
            l_new = alpha * l_old + jnp.sum(p, axis=0, keepdims=True)
            acc_sc[hd] = alpha * (acc_sc[hd] + pvs[hd])
            p_sc[hd] = p.astype(BF16)
            new.append((m_new, l_new))
        for hd in range(ATT_HEADS):
            s_sc[hd] = s_next[hd]
        return tuple(new)

    s0 = scores(0)
    for hd in range(ATT_HEADS):
        s_sc[hd] = s0[hd]
    p_sc[...] = jnp.zeros(p_sc.shape, BF16)
    acc_sc[...] = jnp.zeros(acc_sc.shape, F32)
    init = tuple((jnp.full((1, tq), MASK_VALUE, F32), jnp.zeros((1, tq), F32))
                 for _ in range(ATT_HEADS))
    carry = lax.fori_loop(0, n_full, functools.partial(step, masked=False), init)
    carry = lax.fori_loop(n_full, n_kv, functools.partial(step, masked=True), carry)
    last = jnp.maximum(n_kv - 1, 0)
    o_t = jnp.concatenate([(acc_sc[hd] + pv(last, hd, p_sc[hd])) / carry[hd][1]
                           for hd in range(ATT_HEADS)], axis=0)
    o_ref[0] = o_t.T.astype(BF16)


def _attention(q, k, vt, n_full, n_kv, posq, posk):
    B, H, S, _ = q.shape
    tq = ATT_TQ
    nh = ATT_HEADS
    nk = S // ATT_TK
    grid_spec = pltpu.PrefetchScalarGridSpec(
        num_scalar_prefetch=2,
        grid=(B, H // nh, S // tq),
        in_specs=[
            pl.BlockSpec((1, nh, tq, HEAD_SLOT), lambda b, hp, qi, *_: (b, hp, qi, 0)),
            pl.BlockSpec((1, nh, S, HEAD_SLOT), lambda b, hp, qi, *_: (b, hp, 0, 0)),
            pl.BlockSpec((1, nk, nh * MLA_V, ATT_TK), lambda b, hp, qi, *_: (b, 0, hp, 0)),
            pl.BlockSpec((1, 1, tq), lambda b, hp, qi, *_: (qi, 0, 0)),
            pl.BlockSpec((S, 1), lambda b, hp, qi, *_: (0, 0)),
        ],
        out_specs=pl.BlockSpec((1, tq, nh * MLA_V), lambda b, hp, qi, *_: (b, qi, hp)),
        scratch_shapes=[
            pltpu.VMEM((nh, ATT_TK, tq), F32),
            pltpu.VMEM((nh, ATT_TK, tq), BF16),
            pltpu.VMEM((nh, MLA_V, tq), F32),
        ],
    )
    return pl.pallas_call(
        _attn_kernel,
        grid_spec=grid_spec,
        out_shape=jax.ShapeDtypeStruct((B, S, H * MLA_V), BF16),
        compiler_params=pltpu.CompilerParams(
            dimension_semantics=("parallel", "parallel", "arbitrary"),
            vmem_limit_bytes=VMEM_LIMIT),
        name="mla_attention",
    )(n_full, n_kv, q, k, vt, posq, posk)


def _hgrn_conv_kernel(hc_ref, lbl_ref, onorm_ref, convw_ref, y_ref,
                      st_sc, ubuf_sc, o_sc, *, layer):
    sb = pl.program_id(1)
    R = HG_ROWS
    C = HG_CHUNK
    W = HG_WIDTH

    @pl.when(sb == 0)
    def _():
        st_sc[...] = jnp.zeros(st_sc.shape, F32)
        ubuf_sc[0:8, :] = jnp.zeros((8, CONV_WIDTH), F32)

    lg = lbl_ref[...]
    e = jnp.exp(lg - jnp.max(lg, axis=0, keepdims=True))
    soft = e / jnp.sum(e, axis=0, keepdims=True)
    lb = jnp.zeros((1, W), F32)
    for i in range(1, layer + 1):
        lb = lb + soft[i:i + 1, :]
    lb = jnp.maximum(lb, 0.0)
    log_lb = jnp.log(lb)
    log_1m = jnp.log1p(-lb)

    ri = lax.broadcasted_iota(jnp.int32, (C, C), 0)
    ci = lax.broadcasted_iota(jnp.int32, (C, C), 1)
    sub_shift = HG_SUB.bit_length() - 1
    dk_shift = HG_DK.bit_length() - 1
    tri = jnp.where(ci <= ri, 1.0, 0.0)
    tri_blk = jnp.where((ci >> sub_shift) == (ri >> sub_shift), tri, 0.0)
    cum_mat = jnp.concatenate([tri, tri_blk], axis=0).astype(BF16)
    tt = ri & (HG_SUB - 1)
    lane_w = lax.broadcasted_iota(jnp.int32, (1, W), 1)
    head_masks = [jnp.where((lane_w >> dk_shift) == hh, 1.0, 0.0) for hh in range(HG_HEADS)]
    rw = lax.broadcasted_iota(jnp.int32, (W, W), 0)
    cw = lax.broadcasted_iota(jnp.int32, (W, W), 1)
    bd_mask = (rw >> dk_shift) == (cw >> dk_shift)
    n_sub = C // HG_SUB

    def chunk(c, carry):
        r0 = pl.multiple_of(c * C, C)
        zq = hc_ref[0, pl.ds(r0, C), 0:W]
        zf = hc_ref[0, pl.ds(r0, C), W:2 * W]
        vi = hc_ref[0, pl.ds(r0, C), 2 * W:3 * W]
        q = zq * _sigmoid(zq)
        log_sig = jnp.minimum(zf, 0.0) - jnp.log1p(jnp.exp(-jnp.abs(zf)))
        c2 = log_1m + log_sig
        log_f = jnp.maximum(log_lb, c2) + jnp.log1p(jnp.exp(-jnp.abs(log_lb - c2)))
        kk = (1.0 - lb) * _sigmoid(-zf)
        f_hi = log_f.astype(BF16)
        r1 = log_f - f_hi.astype(F32)
        f_mid = r1.astype(BF16)
        f_lo = (r1 - f_mid.astype(F32)).astype(BF16)
        cum = _dot(cum_mat, f_hi) + _dot(cum_mat, f_mid) + _dot(cum_mat, f_lo)
        b = cum[:C]
        g = cum[C:]
        b_last = b[C - 1:C, :]
        rref = b - g
        qd = q * jnp.exp(g)
        vb = vi.astype(BF16)
        st = st_sc[...]
        o_inter = _dot_nt((q * jnp.exp(b)).astype(BF16), st.astype(BF16))
        o_parts = []
        for i in range(n_sub):
            rr = rref[i * HG_SUB:i * HG_SUB + 1, :]
            kd = kk * jnp.exp(jnp.minimum(rr - b, EXP_CLAMP))
            qi = qd[i * HG_SUB:(i + 1) * HG_SUB]
            lhs = jnp.concatenate([qi * hm for hm in head_masks], axis=0)
            a = _dot_nt(lhs.astype(BF16), kd.astype(BF16))
            a = jnp.where(ci <= i * HG_SUB + tt, a, 0.0)
            pv = _dot(a.astype(BF16), vb)
            oi = pv[0:HG_SUB] * head_masks[0]
            for hh in range(1, HG_HEADS):
                oi = oi + pv[hh * HG_SUB:(hh + 1) * HG_SUB] * head_masks[hh]
            o_parts.append(oi)
        o_sc[pl.ds(r0, C), :] = o_inter + jnp.concatenate(o_parts, axis=0)
        kdl = kk * jnp.exp(b_last - b)
        upd = _dot_tn(vb, kdl.astype(BF16))
        st_sc[...] = st * jnp.exp(b_last) + jnp.where(bd_mask, upd, 0.0)
        return carry

    lax.fori_loop(0, R // C, chunk, 0)

    o = o_sc[...]
    o2 = o * o
    o2_hi = o2.astype(BF16)
    o2_lo = (o2 - o2_hi.astype(F32)).astype(BF16)
    ones_bd = jnp.where(bd_mask, 1.0, 0.0).astype(BF16)
    ms = (_dot(o2_hi, ones_bd) + _dot(o2_lo, ones_bd)) * (1.0 / HG_DK)
    zg = hc_ref[0, :, 3 * W:4 * W]
    y_hg = o * lax.rsqrt(ms + EPS) * onorm_ref[...] * (zg * _sigmoid(zg))
    y_ref[0, :, 0:W] = y_hg.astype(BF16)

    cb = hc_ref[0, :, 4 * W:4 * W + CONV_WIDTH]
    cc = hc_ref[0, :, 4 * W + CONV_WIDTH:4 * W + 2 * CONV_WIDTH]
    cx = hc_ref[0, :, 4 * W + 2 * CONV_WIDTH:4 * W + 3 * CONV_WIDTH]
    u = cc * cx
    ubuf_sc[8:8 + R, :] = u
    u1 = ubuf_sc[7:7 + R, :]
    u2 = ubuf_sc[6:6 + R, :]
    wc = convw_ref[...]
    y_cv = cb * (u2 * wc[0:1, :] + u1 * wc[1:2, :] + u * wc[2:3, :])
    y_ref[0, :, W:W + CONV_WIDTH] = y_cv.astype(BF16)
    ubuf_sc[0:8, :] = ubuf_sc[R:R + 8, :]


def _hgrn_conv(hc, lb_logits, onorm_g, conv_w, layer):
    B, S, _ = hc.shape
    R = HG_ROWS
    const = lambda b, s: (0, 0)
    return pl.pallas_call(
        functools.partial(_hgrn_conv_kernel, layer=layer),
        grid=(B, S // R),
        in_specs=[
            pl.BlockSpec((1, R, HC_WIDTH), lambda b, s: (b, s, 0)),
            pl.BlockSpec((DEPTH, HG_WIDTH), const),
            pl.BlockSpec((1, HG_WIDTH), const),
            pl.BlockSpec((CONV_K, CONV_WIDTH), const),
        ],
        out_specs=pl.BlockSpec((1, R, HG_WIDTH + CONV_WIDTH), lambda b, s: (b, s, 0)),
        out_shape=jax.ShapeDtypeStruct((B, S, HG_WIDTH + CONV_WIDTH), BF16),
        scratch_shapes=[
            pltpu.VMEM((HG_WIDTH, HG_WIDTH), F32),
            pltpu.VMEM((R + 8, CONV_WIDTH), F32),
            pltpu.VMEM((R, HG_WIDTH), F32),
        ],
        compiler_params=pltpu.CompilerParams(
            dimension_semantics=("parallel", "arbitrary"), vmem_limit_bytes=VMEM_LIMIT),
        name="hgrn_conv",
    )(hc, lb_logits, onorm_g, conv_w)


def _mem_kv_kernel(mem_ref, g_ref, w_ref, gk_ref, k_out, v_out):
    m = mem_ref[0]
    mn = m * _rms(m) * g_ref[0]
    kv = _dot(mn.astype(BF16), w_ref[0])
    hw = X_HEADS * X_HEAD_DIM
    gk = gk_ref[0]
    for hd in range(X_HEADS):
        sl = slice(hd * X_HEAD_DIM, (hd + 1) * X_HEAD_DIM)
        kh = kv[:, sl]
        k_out[0, 0, :, sl] = (kh * _rms(kh) * gk).astype(BF16)
    v_out[0, 0] = kv[:, hw:].astype(BF16)


def _mem_kv(mem, mem_norm_g, w_xkv, xk_norm_g):
    B, M, _ = mem.shape
    L = w_xkv.shape[0]
    hw = X_HEADS * X_HEAD_DIM
    out_spec = pl.BlockSpec((1, 1, M, hw), lambda l, b: (l, b, 0, 0))
    return pl.pallas_call(
        _mem_kv_kernel,
        grid=(L, B),
        in_specs=[
            pl.BlockSpec((1, M, D_MODEL), lambda l, b: (b, 0, 0)),
            pl.BlockSpec((1, 1, D_MODEL), lambda l, b: (l, 0, 0)),
            pl.BlockSpec((1, D_MODEL, 2 * hw), lambda l, b: (l, 0, 0)),
            pl.BlockSpec((1, 1, X_HEAD_DIM), lambda l, b: (l, 0, 0)),
        ],
        out_specs=[out_spec, out_spec],
        out_shape=[jax.ShapeDtypeStruct((L, B, M, hw), BF16)] * 2,
        compiler_params=pltpu.CompilerParams(
            dimension_semantics=("parallel", "parallel"), vmem_limit_bytes=VMEM_LIMIT),
        name="mem_kv",
    )(mem, mem_norm_g, w_xkv, xk_norm_g)


def _out_xattn_kernel(x_ref, ya_ref, yb_ref, wo_ref, g_ref, wq_ref, gq_ref,
                      k_ref, v_ref, wxo_ref, o_ref):
    half = wo_ref.shape[0] // 2
    x1 = x_ref[0] + _dot(ya_ref[0], wo_ref[:half, :]) + _dot(yb_ref[0], wo_ref[half:, :])
    h = x1 * _rms(x1) * g_ref[...]
    q = _dot(h.astype(BF16), wq_ref[...])
    gq = gq_ref[...] * (1.0 / math.sqrt(X_HEAD_DIM))
    outs = []
    for hd in range(X_HEADS):
        sl = slice(hd * X_HEAD_DIM, (hd + 1) * X_HEAD_DIM)
        qh = q[:, sl]
        qh = (qh * _rms(qh) * gq).astype(BF16)
        s = _dot_nt(qh, k_ref[0, 0, :, sl])
        p = jnp.exp(s - jnp.max(s, axis=1, keepdims=True))
        l = jnp.sum(p, axis=1, keepdims=True)
        outs.append(_dot(p.astype(BF16), v_ref[0, 0, :, sl]) / l)
    o = jnp.concatenate(outs, axis=1).astype(BF16)
    o_ref[0] = x1 + _dot(o, wxo_ref[...])


def _out_xattn(x, y_mla, y_hc, w_out, g, w_xq, gq, k_mem, v_mem, w_xo, layer):
    B, S, _ = x.shape
    tm = ROW_TILE
    hw = X_HEADS * X_HEAD_DIM
    const = lambda b, s: (0, 0)
    row = lambda w: pl.BlockSpec((1, tm, w), lambda b, s: (b, s, 0))
    mem_spec = pl.BlockSpec((1, 1, MEM_LEN, hw), lambda b, s: (layer, b, 0, 0))
    return pl.pallas_call(
        _out_xattn_kernel,
        grid=(B, S // tm),
        in_specs=[
            row(D_MODEL), row(y_mla.shape[-1]), row(y_hc.shape[-1]),
            pl.BlockSpec((D_MODEL, D_MODEL), const),
            pl.BlockSpec((1, D_MODEL), const),
            pl.BlockSpec((D_MODEL, hw), const),
            pl.BlockSpec((1, X_HEAD_DIM), const),
            mem_spec, mem_spec,
            pl.BlockSpec((hw, D_MODEL), const),
        ],
        out_specs=row(D_MODEL),
        out_shape=jax.ShapeDtypeStruct((B, S, D_MODEL), F32),
        compiler_params=pltpu.CompilerParams(
            dimension_semantics=("parallel", "parallel"), vmem_limit_bytes=VMEM_LIMIT),
        name="out_xattn",
    )(x, y_mla, y_hc, w_out, g, w_xq, gq, k_mem, v_mem, w_xo)


def _mlp_kernel(x_ref, g_ref, wu_ref, wd_ref, o_ref):
    x = x_ref[0]
    h = (x * _rms(x) * g_ref[...]).astype(BF16)
    acc = x
    step = D_MODEL
    for c in range(D_FF // step):
        u = _dot(h, wu_ref[:, c * step:(c + 1) * step])
        a = jnp.square(jnp.maximum(u, 0.0)).astype(BF16)
        acc = acc + _dot(a, wd_ref[c * step:(c + 1) * step, :])
    o_ref[0] = acc


def _mlp(x, g, w_up, w_down):
    B, S, _ = x.shape
    tm = ROW_TILE
    const = lambda b, s: (0, 0)
    row = pl.BlockSpec((1, tm, D_MODEL), lambda b, s: (b, s, 0))
    return pl.pallas_call(
        _mlp_kernel,
        grid=(B, S // tm),
        in_specs=[row, pl.BlockSpec((1, D_MODEL), const),
                  pl.BlockSpec((D_MODEL, D_FF), const),
                  pl.BlockSpec((D_FF, D_MODEL), const)],
        out_specs=row,
        out_shape=jax.ShapeDtypeStruct((B, S, D_MODEL), F32),
        compiler_params=pltpu.CompilerParams(
            dimension_semantics=("parallel", "parallel"), vmem_limit_bytes=VMEM_LIMIT),
        name="mlp",
    )(x, g, w_up, w_down)


def _scatter_lanes(w_cols):
    idx = jnp.asarray(np.where(_LANE_MAP >= 0, _LANE_MAP, 0), jnp.int32)
    valid = jnp.asarray(_LANE_MAP >= 0)
    return jnp.where(valid, jnp.take(w_cols, idx, axis=-1), 0.0)


def _prep_layer(l, positions, w_in, w_uq, w_ukv, mla_qn_g, mla_kn_g):
    o_kr = MLA_Q_RANK + MLA_KV_RANK
    w = w_in[l]
    kr_src = jnp.concatenate(
        [jnp.zeros((D_MODEL, MLA_NOPE), F32), w[:, o_kr:o_kr + MLA_ROPE]], axis=1)
    w_in_p = jnp.concatenate(
        [w[:, :o_kr], _scatter_lanes(kr_src), w[:, o_kr + MLA_ROPE:]], axis=1).astype(BF16)

    wq = w_uq[l].reshape(MLA_Q_RANK, MLA_HEADS, MLA_QK)
    w_uq_p = _scatter_lanes(wq).reshape(MLA_Q_RANK, MLA_HEADS * HEAD_SLOT).astype(BF16)

    wkv = w_ukv[l].reshape(MLA_KV_RANK, MLA_HEADS, MLA_NOPE + MLA_V)
    k_src = jnp.concatenate(
        [wkv[..., :MLA_NOPE], jnp.zeros((MLA_KV_RANK, MLA_HEADS, MLA_ROPE), F32)], axis=-1)
    w_uk_p = _scatter_lanes(k_src).reshape(MLA_KV_RANK, MLA_HEADS * HEAD_SLOT).astype(BF16)
    w_vt = wkv[..., MLA_NOPE:].reshape(MLA_KV_RANK, MLA_HEADS * MLA_V).T.astype(BF16)

    inv_freq = ROPE_BASE ** (-jnp.arange(0, MLA_ROPE, 2, dtype=F32) / MLA_ROPE)
    ang = positions.astype(F32)[:, None] * inv_freq[None, :]
    cos, sin = jnp.cos(ang), jnp.sin(ang)
    S = positions.shape[0]
    pad = HEAD_SLOT // 2 - ROPE_HALF
    c_tab = jnp.concatenate([cos, jnp.ones((S, pad), F32), cos, jnp.ones((S, pad), F32)], axis=1)
    s_tab = jnp.concatenate([-sin, jnp.zeros((S, pad), F32), sin, jnp.zeros((S, pad), F32)], axis=1)

    def tables(gain, scale):
        g = _scatter_lanes(gain)[None, :] * scale
        return c_tab * g, s_tab * jnp.roll(g, HEAD_SLOT // 2, axis=1)

    cq_t, sq_t = tables(mla_qn_g[l], math.log2(math.e) / math.sqrt(MLA_QK))
    ck_t, sk_t = tables(mla_kn_g[l], 1.0)
    return w_in_p, w_uq_p, w_uk_p, w_vt, (cq_t, sq_t, ck_t, sk_t)


def kernel(x, mem, positions, mix_norm_g, w_in, mla_q_norm_g, mla_kv_norm_g, w_uq, w_ukv,
           mla_qn_g, mla_kn_g, hgrn_lb_logits, hgrn_o_norm_g, conv_w, w_out,
           xattn_norm_g, mem_norm_g, w_xq, w_xkv, xq_norm_g, xk_norm_g, w_xo,
           mlp_norm_g, w_up, w_down):
    B, S, _ = x.shape
    L = w_in.shape[0]
    nq, nk = S // ATT_TQ, S // ATT_TK
    pq = positions.reshape(nq, ATT_TQ)
    pk = positions.reshape(nk, ATT_TK)
    vis = jnp.min(pk, axis=1)[None, :] <= jnp.max(pq, axis=1)[:, None]
    n_kv = jnp.max(jnp.where(vis, jnp.arange(1, nk + 1, dtype=jnp.int32)[None, :], 0),
                   axis=1).astype(jnp.int32)
    full = jnp.max(pk, axis=1)[None, :] <= jnp.min(pq, axis=1)[:, None]
    n_full = jnp.sum(jnp.cumprod(full.astype(jnp.int32), axis=1), axis=1).astype(jnp.int32)
    posq = positions.reshape(nq, 1, ATT_TQ)
    posk = positions.reshape(S, 1)

    k_mem, v_mem = _mem_kv(mem, mem_norm_g.reshape(L, 1, D_MODEL), w_xkv.astype(BF16),
                           xk_norm_g.reshape(L, 1, X_HEAD_DIM))
    for l in range(L):
        w_in_p, w_uq_p, w_uk_p, w_vt, tabs = _prep_layer(l, positions, w_in, w_uq, w_ukv,
                                                         mla_qn_g, mla_kn_g)
        q, k, vt, hc = _in_proj(x, mix_norm_g[l][None], w_in_p, mla_q_norm_g[l][None], w_uq_p,
                                mla_kv_norm_g[l][None], w_uk_p, w_vt, tabs)
        y_mla = _attention(q, k, vt, n_full, n_kv, posq, posk)
        y_hc = _hgrn_conv(hc, hgrn_lb_logits, hgrn_o_norm_g[l][None], conv_w[l], l)
        x = _out_xattn(x, y_mla, y_hc, w_out[l].astype(BF16), xattn_norm_g[l][None],
                       w_xq[l].astype(BF16), xq_norm_g[l][None], k_mem, v_mem,
                       w_xo[l].astype(BF16), l)
        x = _mlp(x, mlp_norm_g[l][None], w_up[l].astype(BF16), w_down[l].astype(BF16))
    return x
```

```python
import functools
import math

import numpy as np
import jax
import jax.numpy as jnp
from jax import lax
from jax.experimental import pallas as pl
from jax.experimental.pallas import tpu as pltpu

F32 = jnp.float32
BF16 = jnp.bfloat16

D_MODEL = 1024
DEPTH = 4
MEM_LEN = 256
EPS = 1e-6

MLA_HEADS = 8
MLA_NOPE = 64
MLA_ROPE = 32
MLA_V = 64
MLA_QK = MLA_NOPE + MLA_ROPE
MLA_Q_RANK = 384
MLA_KV_RANK = 256
ROPE_BASE = 10000.0

HG_HEADS = 4
HG_DK = 64
HG_CHUNK = 64
HG_SUB = 16
HG_WIDTH = 256
CONV_WIDTH = 256
CONV_K = 3

X_HEADS = 4
X_HEAD_DIM = 128
D_FF = 4 * D_MODEL

LANES = 128
HEAD_SLOT = LANES
ROPE_HALF = MLA_ROPE // 2
N_IN_PAD = MLA_Q_RANK + MLA_KV_RANK + HEAD_SLOT + 4 * HG_WIDTH + 3 * CONV_WIDTH
HC_OFF = MLA_Q_RANK + MLA_KV_RANK + HEAD_SLOT
HC_WIDTH = N_IN_PAD - HC_OFF

ROW_TILE = 512
ATT_TQ = 256
ATT_TK = 256
ATT_HEADS = 4
HG_ROWS = 512
MASK_VALUE = -1e30
ATT_PLAIN_MAX_SCORE = 64.0
EXP_CLAMP = 60.0
VMEM_LIMIT = 56 * 1024 * 1024


def _head_lane_map():
    m = -np.ones((HEAD_SLOT,), np.int64)
    m[0:ROPE_HALF] = MLA_NOPE + np.arange(ROPE_HALF)
    m[ROPE_HALF:64] = np.arange(64 - ROPE_HALF)
    m[64:64 + ROPE_HALF] = MLA_NOPE + ROPE_HALF + np.arange(ROPE_HALF)
    m[64 + ROPE_HALF:96] = (64 - ROPE_HALF) + np.arange(ROPE_HALF)
    return m


_LANE_MAP = _head_lane_map()


def _rms(x, n=None):
    n = x.shape[-1] if n is None else n
    return lax.rsqrt(jnp.sum(x * x, axis=-1, keepdims=True) * (1.0 / n) + EPS)


def _sigmoid(x):
    return 1.0 / (1.0 + jnp.exp(-x))


def _dot(a, b):
    return jnp.dot(a, b, preferred_element_type=F32)


def _dot_nt(a, b):
    return lax.dot_general(a, b, (((1,), (1,)), ((), ())), preferred_element_type=F32)


def _dot_tn(a, b):
    return lax.dot_general(a, b, (((0,), (0,)), ((), ())), preferred_element_type=F32)


def _in_proj_kernel(x_ref, g_ref, w_in_ref, lat_out, hc_out):
    x = x_ref[0]
    h = (x * _rms(x) * g_ref[...]).astype(BF16)
    lat_out[0] = _dot(h, w_in_ref[:, :HC_OFF])
    hc_out[0] = _dot(h, w_in_ref[:, HC_OFF:])


def _in_proj(x, g, w_in):
    B, S, _ = x.shape
    tm = ROW_TILE
    const = lambda b, s: (0, 0)
    return pl.pallas_call(
        _in_proj_kernel,
        grid=(B, S // tm),
        in_specs=[
            pl.BlockSpec((1, tm, D_MODEL), lambda b, s: (b, s, 0)),
            pl.BlockSpec((1, D_MODEL), const),
            pl.BlockSpec((D_MODEL, N_IN_PAD), const),
        ],
        out_specs=[
            pl.BlockSpec((1, tm, HC_OFF), lambda b, s: (b, s, 0)),
            pl.BlockSpec((1, tm, HC_WIDTH), lambda b, s: (b, s, 0)),
        ],
        out_shape=[
            jax.ShapeDtypeStruct((B, S, HC_OFF), F32),
            jax.ShapeDtypeStruct((B, S, HC_WIDTH), F32),
        ],
        compiler_params=pltpu.CompilerParams(
            dimension_semantics=("parallel", "parallel"), vmem_limit_bytes=VMEM_LIMIT),
        name="in_proj",
    )(x, g, w_in)


def _mla_qkv_kernel(lat_ref, gq_ref, w_uq_ref, gkv_ref, w_uk_ref, w_vt_ref,
                    cq_tab, sq_tab, ck_tab, sk_tab, q_out, k_out, vt_out):
    cq = lat_ref[0, :, :MLA_Q_RANK]
    ckv = lat_ref[0, :, MLA_Q_RANK:MLA_Q_RANK + MLA_KV_RANK]
    kr = lat_ref[0, :, MLA_Q_RANK + MLA_KV_RANK:]
    cqn = (cq * _rms(cq) * gq_ref[...]).astype(BF16)
    ckvn = (ckv * _rms(ckv) * gkv_ref[...]).astype(BF16)
    qf = _dot(cqn, w_uq_ref[...])
    kf = _dot(ckvn, w_uk_ref[...])
    vt = _dot_nt(w_vt_ref[...], ckvn).astype(BF16)
    for j in range(vt_out.shape[1]):
        vt_out[0, j] = vt[:, j * ATT_TK:(j + 1) * ATT_TK]
    cq_t, sq_t, ck_t, sk_t = cq_tab[...], sq_tab[...], ck_tab[...], sk_tab[...]
    for hd in range(MLA_HEADS):
        sl = slice(hd * HEAD_SLOT, (hd + 1) * HEAD_SLOT)
        qh = qf[:, sl]
        qh = (qh * cq_t + pltpu.roll(qh, 64, 1) * sq_t) * _rms(qh, MLA_QK)
        q_out[0, hd] = qh.astype(BF16)
        kh = kf[:, sl] + kr
        kh = (kh * ck_t + pltpu.roll(kh, 64, 1) * sk_t) * _rms(kh, MLA_QK)
        k_out[0, hd] = kh.astype(BF16)


def _mla_qkv(lat, gq, w_uq, gkv, w_uk, w_vt, tabs):
    B, S, _ = lat.shape
    tm = ROW_TILE
    vw = MLA_HEADS * MLA_V
    const = lambda b, s: (0, 0)
    tab_spec = pl.BlockSpec((tm, HEAD_SLOT), lambda b, s: (s, 0))
    return pl.pallas_call(
        _mla_qkv_kernel,
        grid=(B, S // tm),
        in_specs=[
            pl.BlockSpec((1, tm, HC_OFF), lambda b, s: (b, s, 0)),
            pl.BlockSpec((1, MLA_Q_RANK), const),
            pl.BlockSpec((MLA_Q_RANK, MLA_HEADS * HEAD_SLOT), const),
            pl.BlockSpec((1, MLA_KV_RANK), const),
            pl.BlockSpec((MLA_KV_RANK, MLA_HEADS * HEAD_SLOT), const),
            pl.BlockSpec((vw, MLA_KV_RANK), const),
            tab_spec, tab_spec, tab_spec, tab_spec,
        ],
        out_specs=[
            pl.BlockSpec((1, MLA_HEADS, tm, HEAD_SLOT), lambda b, s: (b, 0, s, 0)),
            pl.BlockSpec((1, MLA_HEADS, tm, HEAD_SLOT), lambda b, s: (b, 0, s, 0)),
            pl.BlockSpec((1, tm // ATT_TK, vw, ATT_TK), lambda b, s: (b, s, 0, 0)),
        ],
        out_shape=[
            jax.ShapeDtypeStruct((B, MLA_HEADS, S, HEAD_SLOT), BF16),
            jax.ShapeDtypeStruct((B, MLA_HEADS, S, HEAD_SLOT), BF16),
            jax.ShapeDtypeStruct((B, S // ATT_TK, vw, ATT_TK), BF16),
        ],
        compiler_params=pltpu.CompilerParams(
            dimension_semantics=("parallel", "parallel"), vmem_limit_bytes=VMEM_LIMIT),
        name="mla_qkv",
    )(lat, gq, w_uq, gkv, w_uk, w_vt, *tabs)


def _attn_kernel(nfull_ref, nkv_ref, q_ref, k_ref, vt_ref, posq_ref, posk_ref, o_ref,
                 p_sc, acc_sc, *maybe_s_sc, online):
    qi = pl.program_id(2)
    n_full = nfull_ref[qi]
    n_kv = nkv_ref[qi]
    tq = q_ref.shape[2]
    nk = vt_ref.shape[1]
    heads = range(ATT_HEADS)
    qts = [q_ref[0, hd].astype(F32).T.astype(BF16) for hd in heads]
    posq = posq_ref[0]

    def scores(ki):
        off = pl.multiple_of(ki * ATT_TK, ATT_TK)
        return [_dot(k_ref[0, hd, pl.ds(off, ATT_TK), :], qts[hd]) for hd in heads]

    def pv(ki, hd):
        return _dot(vt_ref[0, ki, hd * MLA_V:(hd + 1) * MLA_V, :], p_sc[hd])

    def mask_of(ki):
        off = pl.multiple_of(ki * ATT_TK, ATT_TK)
        return posk_ref[pl.ds(off, ATT_TK), :] <= posq

    def step_plain(ki, carry, masked):
        ss = scores(ki)
        pvs = [pv(jnp.maximum(ki - 1, 0), hd) for hd in heads]
        if masked:
            mask = mask_of(ki)
        new = []
        for hd in heads:
            s = jnp.where(mask, ss[hd], MASK_VALUE) if masked else ss[hd]
            p = jnp.exp2(s)
            new.append(carry[hd] + jnp.sum(p, axis=0, keepdims=True))
            acc_sc[hd] = acc_sc[hd] + pvs[hd]
            p_sc[hd] = p.astype(BF16)
        return tuple(new)

    def step_online(ki, carry, masked):
        (s_sc,) = maybe_s_sc
        s_next = scores(jnp.minimum(ki + 1, nk - 1))
        pvs = [pv(jnp.maximum(ki - 1, 0), hd) for hd in heads]
        if masked:
            mask = mask_of(ki)
        new = []
        for hd in heads:
            m_old, l_old = carry[hd]
            s = s_sc[hd]
            if masked:
                s = jnp.where(mask, s, MASK_VALUE)
            m_new = jnp.maximum(m_old, jnp.max(s, axis=0, keepdims=True))
            alpha = jnp.exp2(m_old - m_new)
            p = jnp.exp2(s - m_new)
            l_new = alpha * l_old + jnp.sum(p, axis=0, keepdims=True)
            acc_sc[hd] = alpha * (acc_sc[hd] + pvs[hd])
            p_sc[hd] = p.astype(BF16)
            new.append((m_new, l_new))
        for hd in heads:
            s_sc[hd] = s_next[hd]
        return tuple(new)

    p_sc[...] = jnp.zeros(p_sc.shape, BF16)
    acc_sc[...] = jnp.zeros(acc_sc.shape, F32)
    if online:
        s0 = scores(0)
        for hd in heads:
            maybe_s_sc[0][hd] = s0[hd]
        init = tuple((jnp.full((1, tq), MASK_VALUE, F32), jnp.zeros((1, tq), F32))
                     for _ in heads)
        step = step_online
    else:
        init = tuple(jnp.zeros((1, tq), F32) for _ in heads)
        step = step_plain
    carry = lax.fori_loop(0, n_full, functools.partial(step, masked=False), init)
    carry = lax.fori_loop(n_full, n_kv, functools.partial(step, masked=True), carry)
    last = jnp.maximum(n_kv - 1, 0)
    ls = [c[1] for c in carry] if online else carry
    o_t = jnp.concatenate([(acc_sc[hd] + pv(last, hd)) / ls[hd] for hd in heads], axis=0)
    o_ref[0] = o_t.T.astype(BF16)


def _attention(q, k, vt, n_full, n_kv, posq, posk, online):
    B, H, S, _ = q.shape
    tq = ATT_TQ
    nh = ATT_HEADS
    nk = S // ATT_TK
    scratch = [pltpu.VMEM((nh, ATT_TK, tq), BF16), pltpu.VMEM((nh, MLA_V, tq), F32)]
    if online:
        scratch.append(pltpu.VMEM((nh, ATT_TK, tq), F32))
    grid_spec = pltpu.PrefetchScalarGridSpec(
        num_scalar_prefetch=2,
        grid=(B, H // nh, S // tq),
        in_specs=[
            pl.BlockSpec((1, nh, tq, HEAD_SLOT), lambda b, hp, qi, *_: (b, hp, qi, 0)),
            pl.BlockSpec((1, nh, S, HEAD_SLOT), lambda b, hp, qi, *_: (b, hp, 0, 0)),
            pl.BlockSpec((1, nk, nh * MLA_V, ATT_TK), lambda b, hp, qi, *_: (b, 0, hp, 0)),
            pl.BlockSpec((1, 1, tq), lambda b, hp, qi, *_: (qi, 0, 0)),
            pl.BlockSpec((S, 1), lambda b, hp, qi, *_: (0, 0)),
        ],
        out_specs=pl.BlockSpec((1, tq, nh * MLA_V), lambda b, hp, qi, *_: (b, qi, hp)),
        scratch_shapes=scratch,
    )
    return pl.pallas_call(
        functools.partial(_attn_kernel, online=online),
        grid_spec=grid_spec,
        out_shape=jax.ShapeDtypeStruct((B, S, H * MLA_V), BF16),
        compiler_params=pltpu.CompilerParams(
            dimension_semantics=("parallel", "parallel", "arbitrary"),
            vmem_limit_bytes=VMEM_LIMIT),
        name="mla_attention_online" if online else "mla_attention",
    )(n_full, n_kv, q, k, vt, posq, posk)


def _hgrn_conv_kernel(hc_ref, lbl_ref, onorm_ref, convw_ref, y_ref,
                      st_sc, ubuf_sc, o_sc, *, layer):
    sb = pl.program_id(1)
    R = HG_ROWS
    C = HG_CHUNK
    W = HG_WIDTH

    @pl.when(sb == 0)
    def _():
        st_sc[...] = jnp.zeros(st_sc.shape, F32)
        ubuf_sc[0:8, :] = jnp.zeros((8, CONV_WIDTH), F32)

    lg = lbl_ref[...]
    e = jnp.exp(lg - jnp.max(lg, axis=0, keepdims=True))
    soft = e / jnp.sum(e, axis=0, keepdims=True)
    lb = jnp.zeros((1, W), F32)
    for i in range(1, layer + 1):
        lb = lb + soft[i:i + 1, :]
    lb = jnp.maximum(lb, 0.0)
    log_lb = jnp.log(lb)
    log_1m = jnp.log1p(-lb)

    ri = lax.broadcasted_iota(jnp.int32, (C, C), 0)
    ci = lax.broadcasted_iota(jnp.int32, (C, C), 1)
    sub_shift = HG_SUB.bit_length() - 1
    dk_shift = HG_DK.bit_length() - 1
    tri = jnp.where(ci <= ri, 1.0, 0.0)
    tri_blk = jnp.where((ci >> sub_shift) == (ri >> sub_shift), tri, 0.0)
    cum_mat = jnp.concatenate([tri, tri_blk], axis=0).astype(BF16)
    tt = ri & (HG_SUB - 1)
    lane_w = lax.broadcasted_iota(jnp.int32, (1, W), 1)
    head_masks = [jnp.where((lane_w >> dk_shift) == hh, 1.0, 0.0) for hh in range(HG_HEADS)]
    rw = lax.broadcasted_iota(jnp.int32, (W, W), 0)
    cw = lax.broadcasted_iota(jnp.int32, (W, W), 1)
    bd_mask = (rw >> dk_shift) == (cw >> dk_shift)
    n_sub = C // HG_SUB

    def chunk(c, carry):
        r0 = pl.multiple_of(c * C, C)
        zq = hc_ref[0, pl.ds(r0, C), 0:W]
        zf = hc_ref[0, pl.ds(r0, C), W:2 * W]
        vi = hc_ref[0, pl.ds(r0, C), 2 * W:3 * W]
        q = zq * _sigmoid(zq)
        log_sig = jnp.minimum(zf, 0.0) - jnp.log1p(jnp.exp(-jnp.abs(zf)))
        c2 = log_1m + log_sig
        log_f = jnp.maximum(log_lb, c2) + jnp.log1p(jnp.exp(-jnp.abs(log_lb - c2)))
        kk = (1.0 - lb) * _sigmoid(-zf)
        f_hi = log_f.astype(BF16)
        r1 = log_f - f_hi.astype(F32)
        f_mid = r1.astype(BF16)
        f_lo = (r1 - f_mid.astype(F32)).astype(BF16)
        cum = _dot(cum_mat, f_hi) + _dot(cum_mat, f_mid) + _dot(cum_mat, f_lo)
        b = cum[:C]
        g = cum[C:]
        b_last = b[C - 1:C, :]
        rref = b - g
        qd = q * jnp.exp(g)
        vb = vi.astype(BF16)
        st = st_sc[...]
        o_inter = _dot_nt((q * jnp.exp(b)).astype(BF16), st.astype(BF16))
        o_parts = []
        for i in range(n_sub):
            rr = rref[i * HG_SUB:i * HG_SUB + 1, :]
            kd = kk * jnp.exp(jnp.minimum(rr - b, EXP_CLAMP))
            qi = qd[i * HG_SUB:(i + 1) * HG_SUB]
            lhs = jnp.concatenate([qi * hm for hm in head_masks], axis=0)
            a = _dot_nt(lhs.astype(BF16), kd.astype(BF16))
            a = jnp.where(ci <= i * HG_SUB + tt, a, 0.0)
            pv = _dot(a.astype(BF16), vb)
            oi = pv[0:HG_SUB] * head_masks[0]
            for hh in range(1, HG_HEADS):
                oi = oi + pv[hh * HG_SUB:(hh + 1) * HG_SUB] * head_masks[hh]
            o_parts.append(oi)
        o_sc[pl.ds(r0, C), :] = o_inter + jnp.concatenate(o_parts, axis=0)
        kdl = kk * jnp.exp(b_last - b)
        upd = _dot_tn(vb, kdl.astype(BF16))
        st_sc[...] = st * jnp.exp(b_last) + jnp.where(bd_mask, upd, 0.0)
        return carry

    lax.fori_loop(0, R // C, chunk, 0)

    o = o_sc[...]
    o2 = o * o
    o2_hi = o2.astype(BF16)
    o2_lo = (o2 - o2_hi.astype(F32)).astype(BF16)
    ones_bd = jnp.where(bd_mask, 1.0, 0.0).astype(BF16)
    ms = (_dot(o2_hi, ones_bd) + _dot(o2_lo, ones_bd)) * (1.0 / HG_DK)
    zg = hc_ref[0, :, 3 * W:4 * W]
    y_hg = o * lax.rsqrt(ms + EPS) * onorm_ref[...] * (zg * _sigmoid(zg))
    y_ref[0, :, 0:W] = y_hg.astype(BF16)

    cb = hc_ref[0, :, 4 * W:4 * W + CONV_WIDTH]
    cc = hc_ref[0, :, 4 * W + CONV_WIDTH:4 * W + 2 * CONV_WIDTH]
    cx = hc_ref[0, :, 4 * W + 2 * CONV_WIDTH:4 * W + 3 * CONV_WIDTH]
    u = cc * cx
    ubuf_sc[8:8 + R, :] = u
    u1 = ubuf_sc[7:7 + R, :]
    u2 = ubuf_sc[6:6 + R, :]
    wc = convw_ref[...]
    y_cv = cb * (u2 * wc[0:1, :] + u1 * wc[1:2, :] + u * wc[2:3, :])
    y_ref[0, :, W:W + CONV_WIDTH] = y_cv.astype(BF16)
    ubuf_sc[0:8, :] = ubuf_sc[R:R + 8, :]


def _hgrn_conv(hc, lb_logits, onorm_g, conv_w, layer):
    B, S, _ = hc.shape
    R = HG_ROWS
    const = lambda b, s: (0, 0)
    return pl.pallas_call(
        functools.partial(_hgrn_conv_kernel, layer=layer),
        grid=(B, S // R),
        in_specs=[
            pl.BlockSpec((1, R, HC_WIDTH), lambda b, s: (b, s, 0)),
            pl.BlockSpec((DEPTH, HG_WIDTH), const),
            pl.BlockSpec((1, HG_WIDTH), const),
            pl.BlockSpec((CONV_K, CONV_WIDTH), const),
        ],
        out_specs=pl.BlockSpec((1, R, HG_WIDTH + CONV_WIDTH), lambda b, s: (b, s, 0)),
        out_shape=jax.ShapeDtypeStruct((B, S, HG_WIDTH + CONV_WIDTH), BF16),
        scratch_shapes=[
            pltpu.VMEM((HG_WIDTH, HG_WIDTH), F32),
            pltpu.VMEM((R + 8, CONV_WIDTH), F32),
            pltpu.VMEM((R, HG_WIDTH), F32),
        ],
        compiler_params=pltpu.CompilerParams(
            dimension_semantics=("parallel", "arbitrary"), vmem_limit_bytes=VMEM_LIMIT),
        name="hgrn_conv",
    )(hc, lb_logits, onorm_g, conv_w)


def _mem_kv_kernel(mem_ref, g_ref, w_ref, gk_ref, k_out, v_out):
    m = mem_ref[0]
    mn = m * _rms(m) * g_ref[0]
    kv = _dot(mn.astype(BF16), w_ref[0])
    hw = X_HEADS * X_HEAD_DIM
    gk = gk_ref[0]
    for hd in range(X_HEADS):
        sl = slice(hd * X_HEAD_DIM, (hd + 1) * X_HEAD_DIM)
        kh = kv[:, sl]
        k_out[0, 0, :, sl] = (kh * _rms(kh) * gk).astype(BF16)
    v_out[0, 0] = kv[:, hw:].astype(BF16)


def _mem_kv(mem, mem_norm_g, w_xkv, xk_norm_g):
    B, M, _ = mem.shape
    L = w_xkv.shape[0]
    hw = X_HEADS * X_HEAD_DIM
    out_spec = pl.BlockSpec((1, 1, M, hw), lambda l, b: (l, b, 0, 0))
    return pl.pallas_call(
        _mem_kv_kernel,
        grid=(L, B),
        in_specs=[
            pl.BlockSpec((1, M, D_MODEL), lambda l, b: (b, 0, 0)),
            pl.BlockSpec((1, 1, D_MODEL), lambda l, b: (l, 0, 0)),
            pl.BlockSpec((1, D_MODEL, 2 * hw), lambda l, b: (l, 0, 0)),
            pl.BlockSpec((1, 1, X_HEAD_DIM), lambda l, b: (l, 0, 0)),
        ],
        out_specs=[out_spec, out_spec],
        out_shape=[jax.ShapeDtypeStruct((L, B, M, hw), BF16)] * 2,
        compiler_params=pltpu.CompilerParams(
            dimension_semantics=("parallel", "parallel"), vmem_limit_bytes=VMEM_LIMIT),
        name="mem_kv",
    )(mem, mem_norm_g, w_xkv, xk_norm_g)


def _out_xattn_kernel(x_ref, ya_ref, yb_ref, wo_ref, g_ref, wq_ref, gq_ref,
                      k_ref, v_ref, wxo_ref, o_ref):
    half = wo_ref.shape[0] // 2
    x1 = x_ref[0] + _dot(ya_ref[0], wo_ref[:half, :]) + _dot(yb_ref[0], wo_ref[half:, :])
    h = x1 * _rms(x1) * g_ref[...]
    q = _dot(h.astype(BF16), wq_ref[...])
    gq = gq_ref[...] * (1.0 / math.sqrt(X_HEAD_DIM))
    outs = []
    for hd in range(X_HEADS):
        sl = slice(hd * X_HEAD_DIM, (hd + 1) * X_HEAD_DIM)
        qh = q[:, sl]
        qh = (qh * _rms(qh) * gq).astype(BF16)
        s = _dot_nt(qh, k_ref[0, 0, :, sl])
        p = jnp.exp(s - jnp.max(s, axis=1, keepdims=True))
        l = jnp.sum(p, axis=1, keepdims=True)
        outs.append(_dot(p.astype(BF16), v_ref[0, 0, :, sl]) / l)
    o = jnp.concatenate(outs, axis=1).astype(BF16)
    o_ref[0] = x1 + _dot(o, wxo_ref[...])


def _out_xattn(x, y_mla, y_hc, w_out, g, w_xq, gq, k_mem, v_mem, w_xo, layer):
    B, S, _ = x.shape
    tm = ROW_TILE
    hw = X_HEADS * X_HEAD_DIM
    const = lambda b, s: (0, 0)
    row = lambda w: pl.BlockSpec((1, tm, w), lambda b, s: (b, s, 0))
    mem_spec = pl.BlockSpec((1, 1, MEM_LEN, hw), lambda b, s: (layer, b, 0, 0))
    return pl.pallas_call(
        _out_xattn_kernel,
        grid=(B, S // tm),
        in_specs=[
            row(D_MODEL), row(y_mla.shape[-1]), row(y_hc.shape[-1]),
            pl.BlockSpec((D_MODEL, D_MODEL), const),
            pl.BlockSpec((1, D_MODEL), const),
            pl.BlockSpec((D_MODEL, hw), const),
            pl.BlockSpec((1, X_HEAD_DIM), const),
            mem_spec, mem_spec,
            pl.BlockSpec((hw, D_MODEL), const),
        ],
        out_specs=row(D_MODEL),
        out_shape=jax.ShapeDtypeStruct((B, S, D_MODEL), F32),
        compiler_params=pltpu.CompilerParams(
            dimension_semantics=("parallel", "parallel"), vmem_limit_bytes=VMEM_LIMIT),
        name="out_xattn",
    )(x, y_mla, y_hc, w_out, g, w_xq, gq, k_mem, v_mem, w_xo)


def _mlp_kernel(x_ref, g_ref, wu_ref, wd_ref, o_ref):
    x = x_ref[0]
    h = (x * _rms(x) * g_ref[...]).astype(BF16)
    acc = x
    step = D_MODEL
    for c in range(D_FF // step):
        u = _dot(h, wu_ref[:, c * step:(c + 1) * step])
        a = jnp.square(jnp.maximum(u, 0.0)).astype(BF16)
        acc = acc + _dot(a, wd_ref[c * step:(c + 1) * step, :])
    o_ref[0] = acc


def _mlp(x, g, w_up, w_down):
    B, S, _ = x.shape
    tm = ROW_TILE
    const = lambda b, s: (0, 0)
    row = pl.BlockSpec((1, tm, D_MODEL), lambda b, s: (b, s, 0))
    return pl.pallas_call(
        _mlp_kernel,
        grid=(B, S // tm),
        in_specs=[row, pl.BlockSpec((1, D_MODEL), const),
                  pl.BlockSpec((D_MODEL, D_FF), const),
                  pl.BlockSpec((D_FF, D_MODEL), const)],
        out_specs=row,
        out_shape=jax.ShapeDtypeStruct((B, S, D_MODEL), F32),
        compiler_params=pltpu.CompilerParams(
            dimension_semantics=("parallel", "parallel"), vmem_limit_bytes=VMEM_LIMIT),
        name="mlp",
    )(x, g, w_up, w_down)


def _scatter_lanes(w_cols):
    idx = jnp.asarray(np.where(_LANE_MAP >= 0, _LANE_MAP, 0), jnp.int32)
    valid = jnp.asarray(_LANE_MAP >= 0)
    return jnp.where(valid, jnp.take(w_cols, idx, axis=-1), 0.0)


def _prep_layer(l, positions, w_in, w_uq, w_ukv, mla_qn_g, mla_kn_g):
    o_kr = MLA_Q_RANK + MLA_KV_RANK
    w = w_in[l]
    kr_src = jnp.concatenate(
        [jnp.zeros((D_MODEL, MLA_NOPE), F32), w[:, o_kr:o_kr + MLA_ROPE]], axis=1)
    w_in_p = jnp.concatenate(
        [w[:, :o_kr], _scatter_lanes(kr_src), w[:, o_kr + MLA_ROPE:]], axis=1).astype(BF16)

    wq = w_uq[l].reshape(MLA_Q_RANK, MLA_HEADS, MLA_QK)
    w_uq_p = _scatter_lanes(wq).reshape(MLA_Q_RANK, MLA_HEADS * HEAD_SLOT).astype(BF16)

    wkv = w_ukv[l].reshape(MLA_KV_RANK, MLA_HEADS, MLA_NOPE + MLA_V)
    k_src = jnp.concatenate(
        [wkv[..., :MLA_NOPE], jnp.zeros((MLA_KV_RANK, MLA_HEADS, MLA_ROPE), F32)], axis=-1)
    w_uk_p = _scatter_lanes(k_src).reshape(MLA_KV_RANK, MLA_HEADS * HEAD_SLOT).astype(BF16)
    w_vt = wkv[..., MLA_NOPE:].reshape(MLA_KV_RANK, MLA_HEADS * MLA_V).T.astype(BF16)

    inv_freq = ROPE_BASE ** (-jnp.arange(0, MLA_ROPE, 2, dtype=F32) / MLA_ROPE)
    ang = positions.astype(F32)[:, None] * inv_freq[None, :]
    cos, sin = jnp.cos(ang), jnp.sin(ang)
    S = positions.shape[0]
    pad = HEAD_SLOT // 2 - ROPE_HALF
    c_tab = jnp.concatenate([cos, jnp.ones((S, pad), F32), cos, jnp.ones((S, pad), F32)], axis=1)
    s_tab = jnp.concatenate([-sin, jnp.zeros((S, pad), F32), sin, jnp.zeros((S, pad), F32)], axis=1)

    def tables(gain, scale):
        g = _scatter_lanes(gain)[None, :] * scale
        return c_tab * g, s_tab * jnp.roll(g, HEAD_SLOT // 2, axis=1)

    cq_t, sq_t = tables(mla_qn_g[l], math.log2(math.e) / math.sqrt(MLA_QK))
    ck_t, sk_t = tables(mla_kn_g[l], 1.0)
    return w_in_p, w_uq_p, w_uk_p, w_vt, (cq_t, sq_t, ck_t, sk_t)


def kernel(x, mem, positions, mix_norm_g, w_in, mla_q_norm_g, mla_kv_norm_g, w_uq, w_ukv,
           mla_qn_g, mla_kn_g, hgrn_lb_logits, hgrn_o_norm_g, conv_w, w_out,
           xattn_norm_g, mem_norm_g, w_xq, w_xkv, xq_norm_g, xk_norm_g, w_xo,
           mlp_norm_g, w_up, w_down):
    B, S, _ = x.shape
    L = w_in.shape[0]
    nq, nk = S // ATT_TQ, S // ATT_TK
    pq = positions.reshape(nq, ATT_TQ)
    pk = positions.reshape(nk, ATT_TK)
    vis = jnp.min(pk, axis=1)[None, :] <= jnp.max(pq, axis=1)[:, None]
    n_kv = jnp.max(jnp.where(vis, jnp.arange(1, nk + 1, dtype=jnp.int32)[None, :], 0),
                   axis=1).astype(jnp.int32)
    full = jnp.max(pk, axis=1)[None, :] <= jnp.min(pq, axis=1)[:, None]
    n_full = jnp.sum(jnp.cumprod(full.astype(jnp.int32), axis=1), axis=1).astype(jnp.int32)
    posq = positions.reshape(nq, 1, ATT_TQ)
    posk = positions.reshape(S, 1)

    k_mem, v_mem = _mem_kv(mem, mem_norm_g.reshape(L, 1, D_MODEL), w_xkv.astype(BF16),
                           xk_norm_g.reshape(L, 1, X_HEAD_DIM))
    for l in range(L):
        w_in_p, w_uq_p, w_uk_p, w_vt, tabs = _prep_layer(l, positions, w_in, w_uq, w_ukv,
                                                         mla_qn_g, mla_kn_g)
        lat, hc = _in_proj(x, mix_norm_g[l][None], w_in_p)
        q, k, vt = _mla_qkv(lat, mla_q_norm_g[l][None], w_uq_p, mla_kv_norm_g[l][None],
                            w_uk_p, w_vt, tabs)
        score_bound = (math.log2(math.e) * math.sqrt(MLA_QK)
                       * jnp.max(jnp.abs(mla_qn_g[l])) * jnp.max(jnp.abs(mla_kn_g[l])))
        y_mla = lax.cond(score_bound <= ATT_PLAIN_MAX_SCORE,
                         functools.partial(_attention, online=False),
                         functools.partial(_attention, online=True),
                         q, k, vt, n_full, n_kv, posq, posk)
        y_hc = _hgrn_conv(hc, hgrn_lb_logits, hgrn_o_norm_g[l][None], conv_w[l], l)
        x = _out_xattn(x, y_mla, y_hc, w_out[l].astype(BF16), xattn_norm_g[l][None],
                       w_xq[l].astype(BF16), xq_norm_g[l][None], k_mem, v_mem,
                       w_xo[l].astype(BF16), l)
        x = _mlp(x, mlp_norm_g[l][None], w_up[l].astype(BF16), w_down[l].astype(BF16))
    return x
```

```python
import functools
import math

import numpy as np
import jax
import jax.numpy as jnp
from jax import lax
from jax.experimental import pallas as pl
from jax.experimental.pallas import tpu as pltpu

F32 = jnp.float32
BF16 = jnp.bfloat16

D_MODEL = 1024
DEPTH = 4
MEM_LEN = 256
EPS = 1e-6

MLA_HEADS = 8
MLA_NOPE = 64
MLA_ROPE = 32
MLA_V = 64
MLA_QK = MLA_NOPE + MLA_ROPE
MLA_Q_RANK = 384
MLA_KV_RANK = 256
ROPE_BASE = 10000.0

HG_HEADS = 4
HG_DK = 64
HG_CHUNK = 64
HG_SUB = 16
HG_GROUP = 4
HG_WIDTH = 256
CONV_WIDTH = 256
CONV_K = 3

X_HEADS = 4
X_HEAD_DIM = 128
D_FF = 4 * D_MODEL

LANES = 128
HEAD_SLOT = LANES
ROPE_HALF = MLA_ROPE // 2
N_IN_PAD = MLA_Q_RANK + MLA_KV_RANK + HEAD_SLOT + 4 * HG_WIDTH + 3 * CONV_WIDTH
HC_OFF = MLA_Q_RANK + MLA_KV_RANK + HEAD_SLOT
HC_WIDTH = N_IN_PAD - HC_OFF

ROW_TILE = 512
ATT_TQ = 256
ATT_TK = 256
ATT_HEADS = 4
HG_ROWS = 512
MASK_VALUE = -1e30
ATT_PLAIN_MAX_SCORE = 64.0
EXP_CLAMP = 60.0
VMEM_LIMIT = 56 * 1024 * 1024


def _head_lane_map():
    m = -np.ones((HEAD_SLOT,), np.int64)
    m[0:ROPE_HALF] = MLA_NOPE + np.arange(ROPE_HALF)
    m[ROPE_HALF:64] = np.arange(64 - ROPE_HALF)
    m[64:64 + ROPE_HALF] = MLA_NOPE + ROPE_HALF + np.arange(ROPE_HALF)
    m[64 + ROPE_HALF:96] = (64 - ROPE_HALF) + np.arange(ROPE_HALF)
    return m


_LANE_MAP = _head_lane_map()


def _rms(x, n=None):
    n = x.shape[-1] if n is None else n
    return lax.rsqrt(jnp.sum(x * x, axis=-1, keepdims=True) * (1.0 / n) + EPS)


def _sigmoid(x):
    return 1.0 / (1.0 + jnp.exp(-x))


def _dot(a, b):
    return jnp.dot(a, b, preferred_element_type=F32)


def _dot_nt(a, b):
    return lax.dot_general(a, b, (((1,), (1,)), ((), ())), preferred_element_type=F32)


def _dot_tn(a, b):
    return lax.dot_general(a, b, (((0,), (0,)), ((), ())), preferred_element_type=F32)


def _in_proj_kernel(x_ref, g_ref, w_in_ref, lat_out, hc_out):
    x = x_ref[0]
    h = (x * _rms(x) * g_ref[...]).astype(BF16)
    lat_out[0] = _dot(h, w_in_ref[:, :HC_OFF])
    hc_out[0] = _dot(h, w_in_ref[:, HC_OFF:])


def _in_proj(x, g, w_in):
    B, S, _ = x.shape
    tm = ROW_TILE
    const = lambda b, s: (0, 0)
    return pl.pallas_call(
        _in_proj_kernel,
        grid=(B, S // tm),
        in_specs=[
            pl.BlockSpec((1, tm, D_MODEL), lambda b, s: (b, s, 0)),
            pl.BlockSpec((1, D_MODEL), const),
            pl.BlockSpec((D_MODEL, N_IN_PAD), const),
        ],
        out_specs=[
            pl.BlockSpec((1, tm, HC_OFF), lambda b, s: (b, s, 0)),
            pl.BlockSpec((1, tm, HC_WIDTH), lambda b, s: (b, s, 0)),
        ],
        out_shape=[
            jax.ShapeDtypeStruct((B, S, HC_OFF), F32),
            jax.ShapeDtypeStruct((B, S, HC_WIDTH), F32),
        ],
        compiler_params=pltpu.CompilerParams(
            dimension_semantics=("parallel", "parallel"), vmem_limit_bytes=VMEM_LIMIT),
        name="in_proj",
    )(x, g, w_in)


def _mla_qkv_kernel(lat_ref, gq_ref, w_uq_ref, gkv_ref, w_uk_ref, w_vt_ref,
                    cq_tab, sq_tab, ck_tab, sk_tab, q_out, k_out, vt_out):
    cq = lat_ref[0, :, :MLA_Q_RANK]
    ckv = lat_ref[0, :, MLA_Q_RANK:MLA_Q_RANK + MLA_KV_RANK]
    kr = lat_ref[0, :, MLA_Q_RANK + MLA_KV_RANK:]
    cqn = (cq * _rms(cq) * gq_ref[...]).astype(BF16)
    ckvn = (ckv * _rms(ckv) * gkv_ref[...]).astype(BF16)
    qf = _dot(cqn, w_uq_ref[...])
    kf = _dot(ckvn, w_uk_ref[...])
    vt = _dot_nt(w_vt_ref[...], ckvn).astype(BF16)
    for j in range(vt_out.shape[1]):
        vt_out[0, j] = vt[:, j * ATT_TK:(j + 1) * ATT_TK]
    cq_t, sq_t, ck_t, sk_t = cq_tab[...], sq_tab[...], ck_tab[...], sk_tab[...]
    for hd in range(MLA_HEADS):
        sl = slice(hd * HEAD_SLOT, (hd + 1) * HEAD_SLOT)
        qh = qf[:, sl]
        qh = (qh * cq_t + pltpu.roll(qh, 64, 1) * sq_t) * _rms(qh, MLA_QK)
        q_out[0, hd] = qh.astype(BF16)
        kh = kf[:, sl] + kr
        kh = (kh * ck_t + pltpu.roll(kh, 64, 1) * sk_t) * _rms(kh, MLA_QK)
        k_out[0, hd] = kh.astype(BF16)


def _mla_qkv(lat, gq, w_uq, gkv, w_uk, w_vt, tabs):
    B, S, _ = lat.shape
    tm = ROW_TILE
    vw = MLA_HEADS * MLA_V
    const = lambda b, s: (0, 0)
    tab_spec = pl.BlockSpec((tm, HEAD_SLOT), lambda b, s: (s, 0))
    return pl.pallas_call(
        _mla_qkv_kernel,
        grid=(B, S // tm),
        in_specs=[
            pl.BlockSpec((1, tm, HC_OFF), lambda b, s: (b, s, 0)),
            pl.BlockSpec((1, MLA_Q_RANK), const),
            pl.BlockSpec((MLA_Q_RANK, MLA_HEADS * HEAD_SLOT), const),
            pl.BlockSpec((1, MLA_KV_RANK), const),
            pl.BlockSpec((MLA_KV_RANK, MLA_HEADS * HEAD_SLOT), const),
            pl.BlockSpec((vw, MLA_KV_RANK), const),
            tab_spec, tab_spec, tab_spec, tab_spec,
        ],
        out_specs=[
            pl.BlockSpec((1, MLA_HEADS, tm, HEAD_SLOT), lambda b, s: (b, 0, s, 0)),
            pl.BlockSpec((1, MLA_HEADS, tm, HEAD_SLOT), lambda b, s: (b, 0, s, 0)),
            pl.BlockSpec((1, tm // ATT_TK, vw, ATT_TK), lambda b, s: (b, s, 0, 0)),
        ],
        out_shape=[
            jax.ShapeDtypeStruct((B, MLA_HEADS, S, HEAD_SLOT), BF16),
            jax.ShapeDtypeStruct((B, MLA_HEADS, S, HEAD_SLOT), BF16),
            jax.ShapeDtypeStruct((B, S // ATT_TK, vw, ATT_TK), BF16),
        ],
        compiler_params=pltpu.CompilerParams(
            dimension_semantics=("parallel", "parallel"), vmem_limit_bytes=VMEM_LIMIT),
        name="mla_qkv",
    )(lat, gq, w_uq, gkv, w_uk, w_vt, *tabs)


def _attn_kernel(nfull_ref, nkv_ref, q_ref, k_ref, vt_ref, posq_ref, posk_ref, o_ref,
                 p_sc, acc_sc, *maybe_s_sc, online):
    qi = pl.program_id(2)
    n_full = nfull_ref[qi]
    n_kv = nkv_ref[qi]
    tq = q_ref.shape[2]
    nk = vt_ref.shape[1]
    heads = range(ATT_HEADS)
    qts = [q_ref[0, hd].astype(F32).T.astype(BF16) for hd in heads]
    posq = posq_ref[0]

    def scores(ki):
        off = pl.multiple_of(ki * ATT_TK, ATT_TK)
        return [_dot(k_ref[0, hd, pl.ds(off, ATT_TK), :], qts[hd]) for hd in heads]

    def pv(ki, hd):
        return _dot(vt_ref[0, ki, hd * MLA_V:(hd + 1) * MLA_V, :], p_sc[hd])

    def mask_of(ki):
        off = pl.multiple_of(ki * ATT_TK, ATT_TK)
        return posk_ref[pl.ds(off, ATT_TK), :] <= posq

    def step_plain(ki, carry, masked):
        ss = scores(ki)
        pvs = [pv(jnp.maximum(ki - 1, 0), hd) for hd in heads]
        if masked:
            mask = mask_of(ki)
        new = []
        for hd in heads:
            s = jnp.where(mask, ss[hd], MASK_VALUE) if masked else ss[hd]
            p = jnp.exp2(s)
            new.append(carry[hd] + jnp.sum(p, axis=0, keepdims=True))
            acc_sc[hd] = acc_sc[hd] + pvs[hd]
            p_sc[hd] = p.astype(BF16)
        return tuple(new)

    def pair_plain(j, carry):
        k0 = 2 * j
        ss_a = scores(k0)
        pvs_prev = [pv(jnp.maximum(k0 - 1, 0), hd) for hd in heads]
        ss_b = scores(k0 + 1)
        pa, la = [], []
        for hd in heads:
            p = jnp.exp2(ss_a[hd])
            la.append(jnp.sum(p, axis=0, keepdims=True))
            pa.append(p.astype(BF16))
        pvs_a = [_dot(vt_ref[0, k0, hd * MLA_V:(hd + 1) * MLA_V, :], pa[hd]) for hd in heads]
        new = []
        for hd in heads:
            p = jnp.exp2(ss_b[hd])
            new.append(carry[hd] + la[hd] + jnp.sum(p, axis=0, keepdims=True))
            acc_sc[hd] = acc_sc[hd] + pvs_prev[hd] + pvs_a[hd]
            p_sc[hd] = p.astype(BF16)
        return tuple(new)

    def step_online(ki, carry, masked):
        (s_sc,) = maybe_s_sc
        s_next = scores(jnp.minimum(ki + 1, nk - 1))
        pvs = [pv(jnp.maximum(ki - 1, 0), hd) for hd in heads]
        if masked:
            mask = mask_of(ki)
        new = []
        for hd in heads:
            m_old, l_old = carry[hd]
            s = s_sc[hd]
            if masked:
                s = jnp.where(mask, s, MASK_VALUE)
            m_new = jnp.maximum(m_old, jnp.max(s, axis=0, keepdims=True))
            alpha = jnp.exp2(m_old - m_new)
            p = jnp.exp2(s - m_new)
            l_new = alpha * l_old + jnp.sum(p, axis=0, keepdims=True)
            acc_sc[hd] = alpha * (acc_sc[hd] + pvs[hd])
            p_sc[hd] = p.astype(BF16)
            new.append((m_new, l_new))
        for hd in heads:
            s_sc[hd] = s_next[hd]
        return tuple(new)

    p_sc[...] = jnp.zeros(p_sc.shape, BF16)
    acc_sc[...] = jnp.zeros(acc_sc.shape, F32)
    if online:
        s0 = scores(0)
        for hd in heads:
            maybe_s_sc[0][hd] = s0[hd]
        init = tuple((jnp.full((1, tq), MASK_VALUE, F32), jnp.zeros((1, tq), F32))
                     for _ in heads)
        step = step_online
    else:
        init = tuple(jnp.zeros((1, tq), F32) for _ in heads)
        step = step_plain
    n_done = 0
    if not online:
        n_pairs = n_full // 2
        init = lax.fori_loop(0, n_pairs, pair_plain, init)
        n_done = 2 * n_pairs
    carry = lax.fori_loop(n_done, n_full, functools.partial(step, masked=False), init)
    carry = lax.fori_loop(n_full, n_kv, functools.partial(step, masked=True), carry)
    last = jnp.maximum(n_kv - 1, 0)
    ls = [c[1] for c in carry] if online else carry
    o_t = jnp.concatenate([(acc_sc[hd] + pv(last, hd)) / ls[hd] for hd in heads], axis=0)
    o_ref[0] = o_t.T.astype(BF16)


def _attention(q, k, vt, n_full, n_kv, posq, posk, online):
    B, H, S, _ = q.shape
    tq = ATT_TQ
    nh = ATT_HEADS
    nk = S // ATT_TK
    scratch = [pltpu.VMEM((nh, ATT_TK, tq), BF16), pltpu.VMEM((nh, MLA_V, tq), F32)]
    if online:
        scratch.append(pltpu.VMEM((nh, ATT_TK, tq), F32))
    grid_spec = pltpu.PrefetchScalarGridSpec(
        num_scalar_prefetch=2,
        grid=(B, H // nh, S // tq),
        in_specs=[
            pl.BlockSpec((1, nh, tq, HEAD_SLOT), lambda b, hp, qi, *_: (b, hp, qi, 0)),
            pl.BlockSpec((1, nh, S, HEAD_SLOT), lambda b, hp, qi, *_: (b, hp, 0, 0)),
            pl.BlockSpec((1, nk, nh * MLA_V, ATT_TK), lambda b, hp, qi, *_: (b, 0, hp, 0)),
            pl.BlockSpec((1, 1, tq), lambda b, hp, qi, *_: (qi, 0, 0)),
            pl.BlockSpec((S, 1), lambda b, hp, qi, *_: (0, 0)),
        ],
        out_specs=pl.BlockSpec((1, tq, nh * MLA_V), lambda b, hp, qi, *_: (b, qi, hp)),
        scratch_shapes=scratch,
    )
    return pl.pallas_call(
        functools.partial(_attn_kernel, online=online),
        grid_spec=grid_spec,
        out_shape=jax.ShapeDtypeStruct((B, S, H * MLA_V), BF16),
        compiler_params=pltpu.CompilerParams(
            dimension_semantics=("parallel", "parallel", "arbitrary"),
            vmem_limit_bytes=VMEM_LIMIT),
        name="mla_attention_online" if online else "mla_attention",
    )(n_full, n_kv, q, k, vt, posq, posk)


def _hgrn_conv_kernel(hc_ref, lbl_ref, onorm_ref, convw_ref, y_ref,
                      st_sc, ubuf_sc, o_sc, *, layer):
    sb = pl.program_id(1)
    R = HG_ROWS
    C = HG_CHUNK
    W = HG_WIDTH

    @pl.when(sb == 0)
    def _():
        st_sc[...] = jnp.zeros(st_sc.shape, F32)
        ubuf_sc[0:8, :] = jnp.zeros((8, CONV_WIDTH), F32)

    lg = lbl_ref[...]
    e = jnp.exp(lg - jnp.max(lg, axis=0, keepdims=True))
    soft = e / jnp.sum(e, axis=0, keepdims=True)
    lb = jnp.zeros((1, W), F32)
    for i in range(1, layer + 1):
        lb = lb + soft[i:i + 1, :]
    lb = jnp.maximum(lb, 0.0)
    log_lb = jnp.log(lb)
    log_1m = jnp.log1p(-lb)

    ri = lax.broadcasted_iota(jnp.int32, (C, C), 0)
    ci = lax.broadcasted_iota(jnp.int32, (C, C), 1)
    sub_shift = HG_SUB.bit_length() - 1
    dk_shift = HG_DK.bit_length() - 1
    tri = jnp.where(ci <= ri, 1.0, 0.0)
    tri_blk = jnp.where((ci >> sub_shift) == (ri >> sub_shift), tri, 0.0)
    cum_mat = jnp.concatenate([tri, tri_blk], axis=0).astype(BF16)
    tt = ri & (HG_SUB - 1)
    lane_w = lax.broadcasted_iota(jnp.int32, (1, W), 1)
    head_masks = [jnp.where((lane_w >> dk_shift) == hh, 1.0, 0.0) for hh in range(HG_HEADS)]
    rw = lax.broadcasted_iota(jnp.int32, (W, W), 0)
    cw = lax.broadcasted_iota(jnp.int32, (W, W), 1)
    bd_mask = (rw >> dk_shift) == (cw >> dk_shift)
    n_sub = C // HG_SUB

    def group(gi, carry):
        base = gi * (HG_GROUP * C)
        starts = [pl.multiple_of(base + j * C, C) for j in range(HG_GROUP)]
        chunks = range(HG_GROUP)
        gates, splits = [], []
        for r0 in starts:
            zq = hc_ref[0, pl.ds(r0, C), 0:W]
            zf = hc_ref[0, pl.ds(r0, C), W:2 * W]
            vi = hc_ref[0, pl.ds(r0, C), 2 * W:3 * W]
            q = zq * _sigmoid(zq)
            log_sig = jnp.minimum(zf, 0.0) - jnp.log1p(jnp.exp(-jnp.abs(zf)))
            c2 = log_1m + log_sig
            log_f = jnp.maximum(log_lb, c2) + jnp.log1p(jnp.exp(-jnp.abs(log_lb - c2)))
            kk = (1.0 - lb) * _sigmoid(-zf)
            f_hi = log_f.astype(BF16)
            r1 = log_f - f_hi.astype(F32)
            f_mid = r1.astype(BF16)
            f_lo = (r1 - f_mid.astype(F32)).astype(BF16)
            gates.append((q, kk, vi.astype(BF16)))
            splits.append((f_hi, f_mid, f_lo))
        cums = [_dot(cum_mat, s[0]) + _dot(cum_mat, s[1]) + _dot(cum_mat, s[2]) for s in splits]
        qbs, b_lasts, lhss, kds, kdls = [], [], [], [], []
        for j in chunks:
            q, kk, _ = gates[j]
            b = cums[j][:C]
            g = cums[j][C:]
            b_last = b[C - 1:C, :]
            rref = b - g
            qd = q * jnp.exp(g)
            qbs.append((q * jnp.exp(b)).astype(BF16))
            b_lasts.append(b_last)
            kdls.append((kk * jnp.exp(b_last - b)).astype(BF16))
            lhs_j, kd_j = [], []
            for i in range(n_sub):
                rr = rref[i * HG_SUB:i * HG_SUB + 1, :]
                kd_j.append((kk * jnp.exp(jnp.minimum(rr - b, EXP_CLAMP))).astype(BF16))
                qi = qd[i * HG_SUB:(i + 1) * HG_SUB]
                lhs_j.append(jnp.concatenate([qi * hm for hm in head_masks],
                                             axis=0).astype(BF16))
            lhss.append(lhs_j)
            kds.append(kd_j)
        attn = [[_dot_nt(lhss[j][i], kds[j][i]) for i in range(n_sub)] for j in chunks]
        upds = [_dot_tn(gates[j][2], kdls[j]) for j in chunks]
        attn = [[jnp.where(ci <= i * HG_SUB + tt, attn[j][i], 0.0).astype(BF16)
                 for i in range(n_sub)] for j in chunks]
        pvs = [[_dot(attn[j][i], gates[j][2]) for i in range(n_sub)] for j in chunks]
        st = st_sc[...]
        for j in chunks:
            o_parts = []
            for i in range(n_sub):
                pv = pvs[j][i]
                oi = pv[0:HG_SUB] * head_masks[0]
                for hh in range(1, HG_HEADS):
                    oi = oi + pv[hh * HG_SUB:(hh + 1) * HG_SUB] * head_masks[hh]
                o_parts.append(oi)
            o_inter = _dot_nt(qbs[j], st.astype(BF16))
            o_sc[pl.ds(starts[j], C), :] = o_inter + jnp.concatenate(o_parts, axis=0)
            st = st * jnp.exp(b_lasts[j]) + jnp.where(bd_mask, upds[j], 0.0)
        st_sc[...] = st
        return carry

    lax.fori_loop(0, R // (C * HG_GROUP), group, 0)

    o = o_sc[...]
    o2 = o * o
    o2_hi = o2.astype(BF16)
    o2_lo = (o2 - o2_hi.astype(F32)).astype(BF16)
    ones_bd = jnp.where(bd_mask, 1.0, 0.0).astype(BF16)
    ms = (_dot(o2_hi, ones_bd) + _dot(o2_lo, ones_bd)) * (1.0 / HG_DK)
    zg = hc_ref[0, :, 3 * W:4 * W]
    y_hg = o * lax.rsqrt(ms + EPS) * onorm_ref[...] * (zg * _sigmoid(zg))
    y_ref[0, :, 0:W] = y_hg.astype(BF16)

    cb = hc_ref[0, :, 4 * W:4 * W + CONV_WIDTH]
    cc = hc_ref[0, :, 4 * W + CONV_WIDTH:4 * W + 2 * CONV_WIDTH]
    cx = hc_ref[0, :, 4 * W + 2 * CONV_WIDTH:4 * W + 3 * CONV_WIDTH]
    u = cc * cx
    ubuf_sc[8:8 + R, :] = u
    u1 = ubuf_sc[7:7 + R, :]
    u2 = ubuf_sc[6:6 + R, :]
    wc = convw_ref[...]
    y_cv = cb * (u2 * wc[0:1, :] + u1 * wc[1:2, :] + u * wc[2:3, :])
    y_ref[0, :, W:W + CONV_WIDTH] = y_cv.astype(BF16)
    ubuf_sc[0:8, :] = ubuf_sc[R:R + 8, :]


def _hgrn_conv(hc, lb_logits, onorm_g, conv_w, layer):
    B, S, _ = hc.shape
    R = HG_ROWS
    const = lambda b, s: (0, 0)
    return pl.pallas_call(
        functools.partial(_hgrn_conv_kernel, layer=layer),
        grid=(B, S // R),
        in_specs=[
            pl.BlockSpec((1, R, HC_WIDTH), lambda b, s: (b, s, 0)),
            pl.BlockSpec((DEPTH, HG_WIDTH), const),
            pl.BlockSpec((1, HG_WIDTH), const),
            pl.BlockSpec((CONV_K, CONV_WIDTH), const),
        ],
        out_specs=pl.BlockSpec((1, R, HG_WIDTH + CONV_WIDTH), lambda b, s: (b, s, 0)),
        out_shape=jax.ShapeDtypeStruct((B, S, HG_WIDTH + CONV_WIDTH), BF16),
        scratch_shapes=[
            pltpu.VMEM((HG_WIDTH, HG_WIDTH), F32),
            pltpu.VMEM((R + 8, CONV_WIDTH), F32),
            pltpu.VMEM((R, HG_WIDTH), F32),
        ],
        compiler_params=pltpu.CompilerParams(
            dimension_semantics=("parallel", "arbitrary"), vmem_limit_bytes=VMEM_LIMIT),
        name="hgrn_conv",
    )(hc, lb_logits, onorm_g, conv_w)


def _mem_kv_kernel(mem_ref, g_ref, w_ref, gk_ref, k_out, v_out):
    m = mem_ref[0]
    mn = m * _rms(m) * g_ref[0]
    kv = _dot(mn.astype(BF16), w_ref[0])
    hw = X_HEADS * X_HEAD_DIM
    gk = gk_ref[0]
    for hd in range(X_HEADS):
        sl = slice(hd * X_HEAD_DIM, (hd + 1) * X_HEAD_DIM)
        kh = kv[:, sl]
        k_out[0, 0, :, sl] = (kh * _rms(kh) * gk).astype(BF16)
    v_out[0, 0] = kv[:, hw:].astype(BF16)


def _mem_kv(mem, mem_norm_g, w_xkv, xk_norm_g):
    B, M, _ = mem.shape
    L = w_xkv.shape[0]
    hw = X_HEADS * X_HEAD_DIM
    out_spec = pl.BlockSpec((1, 1, M, hw), lambda l, b: (l, b, 0, 0))
    return pl.pallas_call(
        _mem_kv_kernel,
        grid=(L, B),
        in_specs=[
            pl.BlockSpec((1, M, D_MODEL), lambda l, b: (b, 0, 0)),
            pl.BlockSpec((1, 1, D_MODEL), lambda l, b: (l, 0, 0)),
            pl.BlockSpec((1, D_MODEL, 2 * hw), lambda l, b: (l, 0, 0)),
            pl.BlockSpec((1, 1, X_HEAD_DIM), lambda l, b: (l, 0, 0)),
        ],
        out_specs=[out_spec, out_spec],
        out_shape=[jax.ShapeDtypeStruct((L, B, M, hw), BF16)] * 2,
        compiler_params=pltpu.CompilerParams(
            dimension_semantics=("parallel", "parallel"), vmem_limit_bytes=VMEM_LIMIT),
        name="mem_kv",
    )(mem, mem_norm_g, w_xkv, xk_norm_g)


def _out_xattn_kernel(x_ref, ya_ref, yb_ref, wo_ref, g_ref, wq_ref, gq_ref,
                      k_ref, v_ref, wxo_ref, o_ref):
    half = wo_ref.shape[0] // 2
    x1 = x_ref[0] + _dot(ya_ref[0], wo_ref[:half, :]) + _dot(yb_ref[0], wo_ref[half:, :])
    h = x1 * _rms(x1) * g_ref[...]
    q = _dot(h.astype(BF16), wq_ref[...])
    gq = gq_ref[...] * (1.0 / math.sqrt(X_HEAD_DIM))
    outs = []
    for hd in range(X_HEADS):
        sl = slice(hd * X_HEAD_DIM, (hd + 1) * X_HEAD_DIM)
        qh = q[:, sl]
        qh = (qh * _rms(qh) * gq).astype(BF16)
        s = _dot_nt(qh, k_ref[0, 0, :, sl])
        p = jnp.exp(s - jnp.max(s, axis=1, keepdims=True))
        l = jnp.sum(p, axis=1, keepdims=True)
        outs.append(_dot(p.astype(BF16), v_ref[0, 0, :, sl]) / l)
    o = jnp.concatenate(outs, axis=1).astype(BF16)
    o_ref[0] = x1 + _dot(o, wxo_ref[...])


def _out_xattn(x, y_mla, y_hc, w_out, g, w_xq, gq, k_mem, v_mem, w_xo, layer):
    B, S, _ = x.shape
    tm = ROW_TILE
    hw = X_HEADS * X_HEAD_DIM
    const = lambda b, s: (0, 0)
    row = lambda w: pl.BlockSpec((1, tm, w), lambda b, s: (b, s, 0))
    mem_spec = pl.BlockSpec((1, 1, MEM_LEN, hw), lambda b, s: (layer, b, 0, 0))
    return pl.pallas_call(
        _out_xattn_kernel,
        grid=(B, S // tm),
        in_specs=[
            row(D_MODEL), row(y_mla.shape[-1]), row(y_hc.shape[-1]),
            pl.BlockSpec((D_MODEL, D_MODEL), const),
            pl.BlockSpec((1, D_MODEL), const),
            pl.BlockSpec((D_MODEL, hw), const),
            pl.BlockSpec((1, X_HEAD_DIM), const),
            mem_spec, mem_spec,
            pl.BlockSpec((hw, D_MODEL), const),
        ],
        out_specs=row(D_MODEL),
        out_shape=jax.ShapeDtypeStruct((B, S, D_MODEL), F32),
        compiler_params=pltpu.CompilerParams(
            dimension_semantics=("parallel", "parallel"), vmem_limit_bytes=VMEM_LIMIT),
        name="out_xattn",
    )(x, y_mla, y_hc, w_out, g, w_xq, gq, k_mem, v_mem, w_xo)


def _mlp_kernel(x_ref, g_ref, wu_ref, wd_ref, o_ref):
    x = x_ref[0]
    h = (x * _rms(x) * g_ref[...]).astype(BF16)
    acc = x
    step = D_MODEL
    for c in range(D_FF // step):
        u = _dot(h, wu_ref[:, c * step:(c + 1) * step])
        a = jnp.square(jnp.maximum(u, 0.0)).astype(BF16)
        acc = acc + _dot(a, wd_ref[c * step:(c + 1) * step, :])
    o_ref[0] = acc


def _mlp(x, g, w_up, w_down):
    B, S, _ = x.shape
    tm = ROW_TILE
    const = lambda b, s: (0, 0)
    row = pl.BlockSpec((1, tm, D_MODEL), lambda b, s: (b, s, 0))
    return pl.pallas_call(
        _mlp_kernel,
        grid=(B, S // tm),
        in_specs=[row, pl.BlockSpec((1, D_MODEL), const),
                  pl.BlockSpec((D_MODEL, D_FF), const),
                  pl.BlockSpec((D_FF, D_MODEL), const)],
        out_specs=row,
        out_shape=jax.ShapeDtypeStruct((B, S, D_MODEL), F32),
        compiler_params=pltpu.CompilerParams(
            dimension_semantics=("parallel", "parallel"), vmem_limit_bytes=VMEM_LIMIT),
        name="mlp",
    )(x, g, w_up, w_down)


def _scatter_lanes(w_cols):
    idx = jnp.asarray(np.where(_LANE_MAP >= 0, _LANE_MAP, 0), jnp.int32)
    valid = jnp.asarray(_LANE_MAP >= 0)
    return jnp.where(valid, jnp.take(w_cols, idx, axis=-1), 0.0)


def _prep_layer(l, positions, w_in, w_uq, w_ukv, mla_qn_g, mla_kn_g):
    o_kr = MLA_Q_RANK + MLA_KV_RANK
    w = w_in[l]
    kr_src = jnp.concatenate(
        [jnp.zeros((D_MODEL, MLA_NOPE), F32), w[:, o_kr:o_kr + MLA_ROPE]], axis=1)
    w_in_p = jnp.concatenate(
        [w[:, :o_kr], _scatter_lanes(kr_src), w[:, o_kr + MLA_ROPE:]], axis=1).astype(BF16)

    wq = w_uq[l].reshape(MLA_Q_RANK, MLA_HEADS, MLA_QK)
    w_uq_p = _scatter_lanes(wq).reshape(MLA_Q_RANK, MLA_HEADS * HEAD_SLOT).astype(BF16)

    wkv = w_ukv[l].reshape(MLA_KV_RANK, MLA_HEADS, MLA_NOPE + MLA_V)
    k_src = jnp.concatenate(
        [wkv[..., :MLA_NOPE], jnp.zeros((MLA_KV_RANK, MLA_HEADS, MLA_ROPE), F32)], axis=-1)
    w_uk_p = _scatter_lanes(k_src).reshape(MLA_KV_RANK, MLA_HEADS * HEAD_SLOT).astype(BF16)
    w_vt = wkv[..., MLA_NOPE:].reshape(MLA_KV_RANK, MLA_HEADS * MLA_V).T.astype(BF16)

    inv_freq = ROPE_BASE ** (-jnp.arange(0, MLA_ROPE, 2, dtype=F32) / MLA_ROPE)
    ang = positions.astype(F32)[:, None] * inv_freq[None, :]
    cos, sin = jnp.cos(ang), jnp.sin(ang)
    S = positions.shape[0]
    pad = HEAD_SLOT // 2 - ROPE_HALF
    c_tab = jnp.concatenate([cos, jnp.ones((S, pad), F32), cos, jnp.ones((S, pad), F32)], axis=1)
    s_tab = jnp.concatenate([-sin, jnp.zeros((S, pad), F32), sin, jnp.zeros((S, pad), F32)], axis=1)

    def tables(gain, scale):
        g = _scatter_lanes(gain)[None, :] * scale
        return c_tab * g, s_tab * jnp.roll(g, HEAD_SLOT // 2, axis=1)

    cq_t, sq_t = tables(mla_qn_g[l], math.log2(math.e) / math.sqrt(MLA_QK))
    ck_t, sk_t = tables(mla_kn_g[l], 1.0)
    return w_in_p, w_uq_p, w_uk_p, w_vt, (cq_t, sq_t, ck_t, sk_t)


def kernel(x, mem, positions, mix_norm_g, w_in, mla_q_norm_g, mla_kv_norm_g, w_uq, w_ukv,
           mla_qn_g, mla_kn_g, hgrn_lb_logits, hgrn_o_norm_g, conv_w, w_out,
           xattn_norm_g, mem_norm_g, w_xq, w_xkv, xq_norm_g, xk_norm_g, w_xo,
           mlp_norm_g, w_up, w_down):
    B, S, _ = x.shape
    L = w_in.shape[0]
    nq, nk = S // ATT_TQ, S // ATT_TK
    pq = positions.reshape(nq, ATT_TQ)
    pk = positions.reshape(nk, ATT_TK)
    vis = jnp.min(pk, axis=1)[None, :] <= jnp.max(pq, axis=1)[:, None]
    n_kv = jnp.max(jnp.where(vis, jnp.arange(1, nk + 1, dtype=jnp.int32)[None, :], 0),
                   axis=1).astype(jnp.int32)
    full = jnp.max(pk, axis=1)[None, :] <= jnp.min(pq, axis=1)[:, None]
    n_full = jnp.sum(jnp.cumprod(full.astype(jnp.int32), axis=1), axis=1).astype(jnp.int32)
    posq = positions.reshape(nq, 1, ATT_TQ)
    posk = positions.reshape(S, 1)

    k_mem, v_mem = _mem_kv(mem, mem_norm_g.reshape(L, 1, D_MODEL), w_xkv.astype(BF16),
                           xk_norm_g.reshape(L, 1, X_HEAD_DIM))
    for l in range(L):
        w_in_p, w_uq_p, w_uk_p, w_vt, tabs = _prep_layer(l, positions, w_in, w_uq, w_ukv,
                                                         mla_qn_g, mla_kn_g)
        lat, hc = _in_proj(x, mix_norm_g[l][None], w_in_p)
        q, k, vt = _mla_qkv(lat, mla_q_norm_g[l][None], w_uq_p, mla_kv_norm_g[l][None],
                            w_uk_p, w_vt, tabs)
        score_bound = (math.log2(math.e) * math.sqrt(MLA_QK)
                       * jnp.max(jnp.abs(mla_qn_g[l])) * jnp.max(jnp.abs(mla_kn_g[l])))
        y_mla = lax.cond(score_bound <= ATT_PLAIN_MAX_SCORE,
                         functools.partial(_attention, online=False),
                         functools.partial(_attention, online=True),
                         q, k, vt, n_full, n_kv, posq, posk)
        y_hc = _hgrn_conv(hc, hgrn_lb_logits, hgrn_o_norm_g[l][None], conv_w[l], l)
        x = _out_xattn(x, y_mla, y_hc, w_out[l].astype(BF16), xattn_norm_g[l][None],
                       w_xq[l].astype(BF16), xq_norm_g[l][None], k_mem, v_mem,
                       w_xo[l].astype(BF16), l)
        x = _mlp(x, mlp_norm_g[l][None], w_up[l].astype(BF16), w_down[l].astype(BF16))
    return x
```

```python
import functools
import math

import numpy as np
import jax
import jax.numpy as jnp
from jax import lax
from jax.experimental import pallas as pl
from jax.experimental.pallas import tpu as pltpu

F32 = jnp.float32
BF16 = jnp.bfloat16

D_MODEL = 1024
DEPTH = 4
MEM_LEN = 256
EPS = 1e-6

MLA_HEADS = 8
MLA_NOPE = 64
MLA_ROPE = 32
MLA_V = 64
MLA_QK = MLA_NOPE + MLA_ROPE
MLA_Q_RANK = 384
MLA_KV_RANK = 256
ROPE_BASE = 10000.0

HG_HEADS = 4
HG_DK = 64
HG_CHUNK = 64
HG_SUB = 16
HG_GROUP = 4
HG_WIDTH = 256
CONV_WIDTH = 256
CONV_K = 3

X_HEADS = 4
X_HEAD_DIM = 128
D_FF = 4 * D_MODEL

LANES = 128
HEAD_SLOT = LANES
ROPE_HALF = MLA_ROPE // 2
N_IN_PAD = MLA_Q_RANK + MLA_KV_RANK + HEAD_SLOT + 4 * HG_WIDTH + 3 * CONV_WIDTH
HC_OFF = MLA_Q_RANK + MLA_KV_RANK + HEAD_SLOT
HC_WIDTH = N_IN_PAD - HC_OFF

ROW_TILE = 512
ATT_TQ = 256
ATT_TK = 256
ATT_HEADS = 4
HG_ROWS = 512
MASK_VALUE = -1e30
ATT_PLAIN_MAX_SCORE = 64.0
EXP_CLAMP = 60.0
VMEM_LIMIT = 56 * 1024 * 1024


def _head_lane_map():
    m = -np.ones((HEAD_SLOT,), np.int64)
    m[0:ROPE_HALF] = MLA_NOPE + np.arange(ROPE_HALF)
    m[ROPE_HALF:64] = np.arange(64 - ROPE_HALF)
    m[64:64 + ROPE_HALF] = MLA_NOPE + ROPE_HALF + np.arange(ROPE_HALF)
    m[64 + ROPE_HALF:96] = (64 - ROPE_HALF) + np.arange(ROPE_HALF)
    return m


_LANE_MAP = _head_lane_map()


def _rms(x, n=None):
    n = x.shape[-1] if n is None else n
    return lax.rsqrt(jnp.sum(x * x, axis=-1, keepdims=True) * (1.0 / n) + EPS)


def _sigmoid(x):
    return 1.0 / (1.0 + jnp.exp(-x))


def _dot(a, b):
    return jnp.dot(a, b, preferred_element_type=F32)


def _dot_nt(a, b):
    return lax.dot_general(a, b, (((1,), (1,)), ((), ())), preferred_element_type=F32)


def _dot_tn(a, b):
    return lax.dot_general(a, b, (((0,), (0,)), ((), ())), preferred_element_type=F32)


def _in_proj_kernel(x_ref, g_ref, w_in_ref, lat_out, hc_out):
    x = x_ref[0]
    h = (x * _rms(x) * g_ref[...]).astype(BF16)
    lat_out[0] = _dot(h, w_in_ref[:, :HC_OFF])
    hc_out[0] = _dot(h, w_in_ref[:, HC_OFF:])


def _in_proj(x, g, w_in):
    B, S, _ = x.shape
    tm = ROW_TILE
    const = lambda b, s: (0, 0)
    return pl.pallas_call(
        _in_proj_kernel,
        grid=(B, S // tm),
        in_specs=[
            pl.BlockSpec((1, tm, D_MODEL), lambda b, s: (b, s, 0)),
            pl.BlockSpec((1, D_MODEL), const),
            pl.BlockSpec((D_MODEL, N_IN_PAD), const),
        ],
        out_specs=[
            pl.BlockSpec((1, tm, HC_OFF), lambda b, s: (b, s, 0)),
            pl.BlockSpec((1, tm, HC_WIDTH), lambda b, s: (b, s, 0)),
        ],
        out_shape=[
            jax.ShapeDtypeStruct((B, S, HC_OFF), F32),
            jax.ShapeDtypeStruct((B, S, HC_WIDTH), F32),
        ],
        compiler_params=pltpu.CompilerParams(
            dimension_semantics=("parallel", "parallel"), vmem_limit_bytes=VMEM_LIMIT),
        name="in_proj",
    )(x, g, w_in)


def _mla_qkv_kernel(lat_ref, gq_ref, w_uq_ref, gkv_ref, w_uk_ref, w_vt_ref,
                    cq_tab, sq_tab, ck_tab, sk_tab, q_out, k_out, vt_out):
    cq = lat_ref[0, :, :MLA_Q_RANK]
    ckv = lat_ref[0, :, MLA_Q_RANK:MLA_Q_RANK + MLA_KV_RANK]
    kr = lat_ref[0, :, MLA_Q_RANK + MLA_KV_RANK:]
    cqn = (cq * _rms(cq) * gq_ref[...]).astype(BF16)
    ckvn = (ckv * _rms(ckv) * gkv_ref[...]).astype(BF16)
    hw = MLA_HEADS * HEAD_SLOT
    qf = _dot(cqn, w_uq_ref[...])
    kf = _dot(ckvn, w_uk_ref[...])
    vt = _dot_nt(w_vt_ref[...], ckvn).astype(BF16)
    for j in range(vt_out.shape[1]):
        vt_out[0, j] = vt[:, j * ATT_TK:(j + 1) * ATT_TK]
    kr_sw = pltpu.roll(kr, HEAD_SLOT // 2, 1)
    cq_t, sq_t, ck_t, sk_t = cq_tab[...], sq_tab[...], ck_tab[...], sk_tab[...]
    for hd in range(MLA_HEADS):
        sl = slice(hd * HEAD_SLOT, (hd + 1) * HEAD_SLOT)
        sw = slice(hw + hd * HEAD_SLOT, hw + (hd + 1) * HEAD_SLOT)
        qh = qf[:, sl]
        qh = (qh * cq_t + qf[:, sw] * sq_t) * _rms(qh, MLA_QK)
        q_out[0, hd] = qh.astype(BF16)
        kh = kf[:, sl] + kr
        kh = (kh * ck_t + (kf[:, sw] + kr_sw) * sk_t) * _rms(kh, MLA_QK)
        k_out[0, hd] = kh.astype(BF16)


def _mla_qkv(lat, gq, w_uq, gkv, w_uk, w_vt, tabs):
    B, S, _ = lat.shape
    tm = ROW_TILE
    vw = MLA_HEADS * MLA_V
    const = lambda b, s: (0, 0)
    tab_spec = pl.BlockSpec((tm, HEAD_SLOT), lambda b, s: (s, 0))
    return pl.pallas_call(
        _mla_qkv_kernel,
        grid=(B, S // tm),
        in_specs=[
            pl.BlockSpec((1, tm, HC_OFF), lambda b, s: (b, s, 0)),
            pl.BlockSpec((1, MLA_Q_RANK), const),
            pl.BlockSpec((MLA_Q_RANK, 2 * MLA_HEADS * HEAD_SLOT), const),
            pl.BlockSpec((1, MLA_KV_RANK), const),
            pl.BlockSpec((MLA_KV_RANK, 2 * MLA_HEADS * HEAD_SLOT), const),
            pl.BlockSpec((vw, MLA_KV_RANK), const),
            tab_spec, tab_spec, tab_spec, tab_spec,
        ],
        out_specs=[
            pl.BlockSpec((1, MLA_HEADS, tm, HEAD_SLOT), lambda b, s: (b, 0, s, 0)),
            pl.BlockSpec((1, MLA_HEADS, tm, HEAD_SLOT), lambda b, s: (b, 0, s, 0)),
            pl.BlockSpec((1, tm // ATT_TK, vw, ATT_TK), lambda b, s: (b, s, 0, 0)),
        ],
        out_shape=[
            jax.ShapeDtypeStruct((B, MLA_HEADS, S, HEAD_SLOT), BF16),
            jax.ShapeDtypeStruct((B, MLA_HEADS, S, HEAD_SLOT), BF16),
            jax.ShapeDtypeStruct((B, S // ATT_TK, vw, ATT_TK), BF16),
        ],
        compiler_params=pltpu.CompilerParams(
            dimension_semantics=("parallel", "parallel"), vmem_limit_bytes=VMEM_LIMIT),
        name="mla_qkv",
    )(lat, gq, w_uq, gkv, w_uk, w_vt, *tabs)


def _attn_kernel(nfull_ref, nkv_ref, q_ref, k_ref, vt_ref, posq_ref, posk_ref, o_ref,
                 p_sc, acc_sc, *maybe_s_sc, online):
    qi = pl.program_id(2)
    n_full = nfull_ref[qi]
    n_kv = nkv_ref[qi]
    tq = q_ref.shape[2]
    nk = vt_ref.shape[1]
    heads = range(ATT_HEADS)
    qts = [q_ref[0, hd].astype(F32).T.astype(BF16) for hd in heads]
    posq = posq_ref[0]

    def scores(ki):
        off = pl.multiple_of(ki * ATT_TK, ATT_TK)
        return [_dot(k_ref[0, hd, pl.ds(off, ATT_TK), :], qts[hd]) for hd in heads]

    def pv(ki, hd):
        return _dot(vt_ref[0, ki, hd * MLA_V:(hd + 1) * MLA_V, :], p_sc[hd])

    def mask_of(ki):
        off = pl.multiple_of(ki * ATT_TK, ATT_TK)
        return posk_ref[pl.ds(off, ATT_TK), :] <= posq

    def step_plain(ki, carry, masked):
        ss = scores(ki)
        pvs = [pv(jnp.maximum(ki - 1, 0), hd) for hd in heads]
        if masked:
            mask = mask_of(ki)
        new = []
        for hd in heads:
            s = jnp.where(mask, ss[hd], MASK_VALUE) if masked else ss[hd]
            p = jnp.exp2(s)
            new.append(carry[hd] + jnp.sum(p, axis=0, keepdims=True))
            acc_sc[hd] = acc_sc[hd] + pvs[hd]
            p_sc[hd] = p.astype(BF16)
        return tuple(new)

    def pair_plain(j, carry):
        k0 = 2 * j
        ss_a = scores(k0)
        pvs_prev = [pv(jnp.maximum(k0 - 1, 0), hd) for hd in heads]
        ss_b = scores(k0 + 1)
        pa, la = [], []
        for hd in heads:
            p = jnp.exp2(ss_a[hd])
            la.append(jnp.sum(p, axis=0, keepdims=True))
            pa.append(p.astype(BF16))
        pvs_a = [_dot(vt_ref[0, k0, hd * MLA_V:(hd + 1) * MLA_V, :], pa[hd]) for hd in heads]
        new = []
        for hd in heads:
            p = jnp.exp2(ss_b[hd])
            new.append(carry[hd] + la[hd] + jnp.sum(p, axis=0, keepdims=True))
            acc_sc[hd] = acc_sc[hd] + pvs_prev[hd] + pvs_a[hd]
            p_sc[hd] = p.astype(BF16)
        return tuple(new)

    def step_online(ki, carry, masked):
        (s_sc,) = maybe_s_sc
        s_next = scores(jnp.minimum(ki + 1, nk - 1))
        pvs = [pv(jnp.maximum(ki - 1, 0), hd) for hd in heads]
        if masked:
            mask = mask_of(ki)
        new = []
        for hd in heads:
            m_old, l_old = carry[hd]
            s = s_sc[hd]
            if masked:
                s = jnp.where(mask, s, MASK_VALUE)
            m_new = jnp.maximum(m_old, jnp.max(s, axis=0, keepdims=True))
            alpha = jnp.exp2(m_old - m_new)
            p = jnp.exp2(s - m_new)
            l_new = alpha * l_old + jnp.sum(p, axis=0, keepdims=True)
            acc_sc[hd] = alpha * (acc_sc[hd] + pvs[hd])
            p_sc[hd] = p.astype(BF16)
            new.append((m_new, l_new))
        for hd in heads:
            s_sc[hd] = s_next[hd]
        return tuple(new)

    p_sc[...] = jnp.zeros(p_sc.shape, BF16)
    acc_sc[...] = jnp.zeros(acc_sc.shape, F32)
    if online:
        s0 = scores(0)
        for hd in heads:
            maybe_s_sc[0][hd] = s0[hd]
        init = tuple((jnp.full((1, tq), MASK_VALUE, F32), jnp.zeros((1, tq), F32))
                     for _ in heads)
        step = step_online
    else:
        init = tuple(jnp.zeros((1, tq), F32) for _ in heads)
        step = step_plain
    n_done = 0
    if not online:
        n_pairs = n_full // 2
        init = lax.fori_loop(0, n_pairs, pair_plain, init)
        n_done = 2 * n_pairs
    carry = lax.fori_loop(n_done, n_full, functools.partial(step, masked=False), init)
    carry = lax.fori_loop(n_full, n_kv, functools.partial(step, masked=True), carry)
    last = jnp.maximum(n_kv - 1, 0)
    ls = [c[1] for c in carry] if online else carry
    o_t = jnp.concatenate([(acc_sc[hd] + pv(last, hd)) / ls[hd] for hd in heads], axis=0)
    o_ref[0] = o_t.T.astype(BF16)


def _attention(q, k, vt, n_full, n_kv, posq, posk, online):
    B, H, S, _ = q.shape
    tq = ATT_TQ
    nh = ATT_HEADS
    nk = S // ATT_TK
    scratch = [pltpu.VMEM((nh, ATT_TK, tq), BF16), pltpu.VMEM((nh, MLA_V, tq), F32)]
    if online:
        scratch.append(pltpu.VMEM((nh, ATT_TK, tq), F32))
    grid_spec = pltpu.PrefetchScalarGridSpec(
        num_scalar_prefetch=2,
        grid=(B, H // nh, S // tq),
        in_specs=[
            pl.BlockSpec((1, nh, tq, HEAD_SLOT), lambda b, hp, qi, *_: (b, hp, qi, 0)),
            pl.BlockSpec((1, nh, S, HEAD_SLOT), lambda b, hp, qi, *_: (b, hp, 0, 0)),
            pl.BlockSpec((1, nk, nh * MLA_V, ATT_TK), lambda b, hp, qi, *_: (b, 0, hp, 0)),
            pl.BlockSpec((1, 1, tq), lambda b, hp, qi, *_: (qi, 0, 0)),
            pl.BlockSpec((S, 1), lambda b, hp, qi, *_: (0, 0)),
        ],
        out_specs=pl.BlockSpec((1, tq, nh * MLA_V), lambda b, hp, qi, *_: (b, qi, hp)),
        scratch_shapes=scratch,
    )
    return pl.pallas_call(
        functools.partial(_attn_kernel, online=online),
        grid_spec=grid_spec,
        out_shape=jax.ShapeDtypeStruct((B, S, H * MLA_V), BF16),
        compiler_params=pltpu.CompilerParams(
            dimension_semantics=("parallel", "parallel", "arbitrary"),
            vmem_limit_bytes=VMEM_LIMIT),
        name="mla_attention_online" if online else "mla_attention",
    )(n_full, n_kv, q, k, vt, posq, posk)


def _hgrn_conv_kernel(hc_ref, lbl_ref, onorm_ref, convw_ref, y_ref,
                      st_sc, ubuf_sc, o_sc, *, layer):
    sb = pl.program_id(1)
    R = HG_ROWS
    C = HG_CHUNK
    W = HG_WIDTH

    @pl.when(sb == 0)
    def _():
        st_sc[...] = jnp.zeros(st_sc.shape, F32)
        ubuf_sc[0:8, :] = jnp.zeros((8, CONV_WIDTH), F32)

    lg = lbl_ref[...]
    e = jnp.exp(lg - jnp.max(lg, axis=0, keepdims=True))
    soft = e / jnp.sum(e, axis=0, keepdims=True)
    lb = jnp.zeros((1, W), F32)
    for i in range(1, layer + 1):
        lb = lb + soft[i:i + 1, :]
    lb = jnp.maximum(lb, 0.0)
    log_lb = jnp.log(lb)
    log_1m = jnp.log1p(-lb)

    ri = lax.broadcasted_iota(jnp.int32, (C, C), 0)
    ci = lax.broadcasted_iota(jnp.int32, (C, C), 1)
    sub_shift = HG_SUB.bit_length() - 1
    dk_shift = HG_DK.bit_length() - 1
    tri = jnp.where(ci <= ri, 1.0, 0.0)
    tri_blk = jnp.where((ci >> sub_shift) == (ri >> sub_shift), tri, 0.0)
    cum_mat = jnp.concatenate([tri, tri_blk], axis=0).astype(BF16)
    tt = ri & (HG_SUB - 1)
    lane_w = lax.broadcasted_iota(jnp.int32, (1, W), 1)
    head_masks = [jnp.where((lane_w >> dk_shift) == hh, 1.0, 0.0) for hh in range(HG_HEADS)]
    rw = lax.broadcasted_iota(jnp.int32, (W, W), 0)
    cw = lax.broadcasted_iota(jnp.int32, (W, W), 1)
    bd_mask = (rw >> dk_shift) == (cw >> dk_shift)
    n_sub = C // HG_SUB

    def group(gi, carry):
        base = gi * (HG_GROUP * C)
        starts = [pl.multiple_of(base + j * C, C) for j in range(HG_GROUP)]
        chunks = range(HG_GROUP)
        gates, splits = [], []
        for r0 in starts:
            zq = hc_ref[0, pl.ds(r0, C), 0:W]
            zf = hc_ref[0, pl.ds(r0, C), W:2 * W]
            vi = hc_ref[0, pl.ds(r0, C), 2 * W:3 * W]
            q = zq * _sigmoid(zq)
            log_sig = jnp.minimum(zf, 0.0) - jnp.log1p(jnp.exp(-jnp.abs(zf)))
            c2 = log_1m + log_sig
            log_f = jnp.maximum(log_lb, c2) + jnp.log1p(jnp.exp(-jnp.abs(log_lb - c2)))
            kk = (1.0 - lb) * _sigmoid(-zf)
            f_hi = log_f.astype(BF16)
            r1 = log_f - f_hi.astype(F32)
            f_mid = r1.astype(BF16)
            f_lo = (r1 - f_mid.astype(F32)).astype(BF16)
            gates.append((q, kk, vi.astype(BF16)))
            splits.append((f_hi, f_mid, f_lo))
        cums = [_dot(cum_mat, s[0]) + _dot(cum_mat, s[1]) + _dot(cum_mat, s[2]) for s in splits]
        qbs, b_lasts, lhss, kds, kdls = [], [], [], [], []
        for j in chunks:
            q, kk, _ = gates[j]
            b = cums[j][:C]
            g = cums[j][C:]
            b_last = b[C - 1:C, :]
            rref = b - g
            qd = q * jnp.exp(g)
            qbs.append((q * jnp.exp(b)).astype(BF16))
            b_lasts.append(b_last)
            kdls.append((kk * jnp.exp(b_last - b)).astype(BF16))
            lhs_j, kd_j = [], []
            for i in range(n_sub):
                rr = rref[i * HG_SUB:i * HG_SUB + 1, :]
                kd_j.append((kk * jnp.exp(jnp.minimum(rr - b, EXP_CLAMP))).astype(BF16))
                qi = qd[i * HG_SUB:(i + 1) * HG_SUB]
                lhs_j.append(jnp.concatenate([qi * hm for hm in head_masks],
                                             axis=0).astype(BF16))
            lhss.append(lhs_j)
            kds.append(kd_j)
        attn = [[_dot_nt(lhss[j][i], kds[j][i]) for i in range(n_sub)] for j in chunks]
        upds = [_dot_tn(gates[j][2], kdls[j]) for j in chunks]
        attn = [[jnp.where(ci <= i * HG_SUB + tt, attn[j][i], 0.0).astype(BF16)
                 for i in range(n_sub)] for j in chunks]
        pvs = [[_dot(attn[j][i], gates[j][2]) for i in range(n_sub)] for j in chunks]
        st = st_sc[...]
        for j in chunks:
            o_parts = []
            for i in range(n_sub):
                pv = pvs[j][i]
                oi = pv[0:HG_SUB] * head_masks[0]
                for hh in range(1, HG_HEADS):
                    oi = oi + pv[hh * HG_SUB:(hh + 1) * HG_SUB] * head_masks[hh]
                o_parts.append(oi)
            o_inter = _dot_nt(qbs[j], st.astype(BF16))
            o_sc[pl.ds(starts[j], C), :] = o_inter + jnp.concatenate(o_parts, axis=0)
            st = st * jnp.exp(b_lasts[j]) + jnp.where(bd_mask, upds[j], 0.0)
        st_sc[...] = st
        return carry

    lax.fori_loop(0, R // (C * HG_GROUP), group, 0)

    o = o_sc[...]
    o2 = o * o
    o2_hi = o2.astype(BF16)
    o2_lo = (o2 - o2_hi.astype(F32)).astype(BF16)
    ones_bd = jnp.where(bd_mask, 1.0, 0.0).astype(BF16)
    ms = (_dot(o2_hi, ones_bd) + _dot(o2_lo, ones_bd)) * (1.0 / HG_DK)
    zg = hc_ref[0, :, 3 * W:4 * W]
    y_hg = o * lax.rsqrt(ms + EPS) * onorm_ref[...] * (zg * _sigmoid(zg))
    y_ref[0, :, 0:W] = y_hg.astype(BF16)

    cb = hc_ref[0, :, 4 * W:4 * W + CONV_WIDTH]
    cc = hc_ref[0, :, 4 * W + CONV_WIDTH:4 * W + 2 * CONV_WIDTH]
    cx = hc_ref[0, :, 4 * W + 2 * CONV_WIDTH:4 * W + 3 * CONV_WIDTH]
    u = cc * cx
    ubuf_sc[8:8 + R, :] = u
    u1 = ubuf_sc[7:7 + R, :]
    u2 = ubuf_sc[6:6 + R, :]
    wc = convw_ref[...]
    y_cv = cb * (u2 * wc[0:1, :] + u1 * wc[1:2, :] + u * wc[2:3, :])
    y_ref[0, :, W:W + CONV_WIDTH] = y_cv.astype(BF16)
    ubuf_sc[0:8, :] = ubuf_sc[R:R + 8, :]


def _hgrn_conv(hc, lb_logits, onorm_g, conv_w, layer):
    B, S, _ = hc.shape
    R = HG_ROWS
    const = lambda b, s: (0, 0)
    return pl.pallas_call(
        functools.partial(_hgrn_conv_kernel, layer=layer),
        grid=(B, S // R),
        in_specs=[
            pl.BlockSpec((1, R, HC_WIDTH), lambda b, s: (b, s, 0)),
            pl.BlockSpec((DEPTH, HG_WIDTH), const),
            pl.BlockSpec((1, HG_WIDTH), const),
            pl.BlockSpec((CONV_K, CONV_WIDTH), const),
        ],
        out_specs=pl.BlockSpec((1, R, HG_WIDTH + CONV_WIDTH), lambda b, s: (b, s, 0)),
        out_shape=jax.ShapeDtypeStruct((B, S, HG_WIDTH + CONV_WIDTH), BF16),
        scratch_shapes=[
            pltpu.VMEM((HG_WIDTH, HG_WIDTH), F32),
            pltpu.VMEM((R + 8, CONV_WIDTH), F32),
            pltpu.VMEM((R, HG_WIDTH), F32),
        ],
        compiler_params=pltpu.CompilerParams(
            dimension_semantics=("parallel", "arbitrary"), vmem_limit_bytes=VMEM_LIMIT),
        name="hgrn_conv",
    )(hc, lb_logits, onorm_g, conv_w)


def _mem_kv_kernel(mem_ref, g_ref, w_ref, gk_ref, k_out, v_out):
    m = mem_ref[0]
    mn = m * _rms(m) * g_ref[0]
    kv = _dot(mn.astype(BF16), w_ref[0])
    hw = X_HEADS * X_HEAD_DIM
    gk = gk_ref[0]
    for hd in range(X_HEADS):
        sl = slice(hd * X_HEAD_DIM, (hd + 1) * X_HEAD_DIM)
        kh = kv[:, sl]
        k_out[0, 0, :, sl] = (kh * _rms(kh) * gk).astype(BF16)
    v_out[0, 0] = kv[:, hw:].astype(BF16)


def _mem_kv(mem, mem_norm_g, w_xkv, xk_norm_g):
    B, M, _ = mem.shape
    L = w_xkv.shape[0]
    hw = X_HEADS * X_HEAD_DIM
    out_spec = pl.BlockSpec((1, 1, M, hw), lambda l, b: (l, b, 0, 0))
    return pl.pallas_call(
        _mem_kv_kernel,
        grid=(L, B),
        in_specs=[
            pl.BlockSpec((1, M, D_MODEL), lambda l, b: (b, 0, 0)),
            pl.BlockSpec((1, 1, D_MODEL), lambda l, b: (l, 0, 0)),
            pl.BlockSpec((1, D_MODEL, 2 * hw), lambda l, b: (l, 0, 0)),
            pl.BlockSpec((1, 1, X_HEAD_DIM), lambda l, b: (l, 0, 0)),
        ],
        out_specs=[out_spec, out_spec],
        out_shape=[jax.ShapeDtypeStruct((L, B, M, hw), BF16)] * 2,
        compiler_params=pltpu.CompilerParams(
            dimension_semantics=("parallel", "parallel"), vmem_limit_bytes=VMEM_LIMIT),
        name="mem_kv",
    )(mem, mem_norm_g, w_xkv, xk_norm_g)


def _out_xattn_kernel(x_ref, ya_ref, yb_ref, wo_ref, g_ref, wq_ref, gq_ref,
                      k_ref, v_ref, wxo_ref, o_ref):
    half = wo_ref.shape[0] // 2
    x1 = x_ref[0] + _dot(ya_ref[0], wo_ref[:half, :]) + _dot(yb_ref[0], wo_ref[half:, :])
    h = x1 * _rms(x1) * g_ref[...]
    q = _dot(h.astype(BF16), wq_ref[...])
    gq = gq_ref[...] * (1.0 / math.sqrt(X_HEAD_DIM))
    outs = []
    for hd in range(X_HEADS):
        sl = slice(hd * X_HEAD_DIM, (hd + 1) * X_HEAD_DIM)
        qh = q[:, sl]
        qh = (qh * _rms(qh) * gq).astype(BF16)
        s = _dot_nt(qh, k_ref[0, 0, :, sl])
        p = jnp.exp(s - jnp.max(s, axis=1, keepdims=True))
        l = jnp.sum(p, axis=1, keepdims=True)
        outs.append(_dot(p.astype(BF16), v_ref[0, 0, :, sl]) / l)
    o = jnp.concatenate(outs, axis=1).astype(BF16)
    o_ref[0] = x1 + _dot(o, wxo_ref[...])


def _out_xattn(x, y_mla, y_hc, w_out, g, w_xq, gq, k_mem, v_mem, w_xo, layer):
    B, S, _ = x.shape
    tm = ROW_TILE
    hw = X_HEADS * X_HEAD_DIM
    const = lambda b, s: (0, 0)
    row = lambda w: pl.BlockSpec((1, tm, w), lambda b, s: (b, s, 0))
    mem_spec = pl.BlockSpec((1, 1, MEM_LEN, hw), lambda b, s: (layer, b, 0, 0))
    return pl.pallas_call(
        _out_xattn_kernel,
        grid=(B, S // tm),
        in_specs=[
            row(D_MODEL), row(y_mla.shape[-1]), row(y_hc.shape[-1]),
            pl.BlockSpec((D_MODEL, D_MODEL), const),
            pl.BlockSpec((1, D_MODEL), const),
            pl.BlockSpec((D_MODEL, hw), const),
            pl.BlockSpec((1, X_HEAD_DIM), const),
            mem_spec, mem_spec,
            pl.BlockSpec((hw, D_MODEL), const),
        ],
        out_specs=row(D_MODEL),
        out_shape=jax.ShapeDtypeStruct((B, S, D_MODEL), F32),
        compiler_params=pltpu.CompilerParams(
            dimension_semantics=("parallel", "parallel"), vmem_limit_bytes=VMEM_LIMIT),
        name="out_xattn",
    )(x, y_mla, y_hc, w_out, g, w_xq, gq, k_mem, v_mem, w_xo)


def _mlp_kernel(x_ref, g_ref, wu_ref, wd_ref, o_ref):
    x = x_ref[0]
    h = (x * _rms(x) * g_ref[...]).astype(BF16)
    acc = x
    step = D_MODEL
    for c in range(D_FF // step):
        u = _dot(h, wu_ref[:, c * step:(c + 1) * step])
        a = jnp.square(jnp.maximum(u, 0.0)).astype(BF16)
        acc = acc + _dot(a, wd_ref[c * step:(c + 1) * step, :])
    o_ref[0] = acc


def _mlp(x, g, w_up, w_down):
    B, S, _ = x.shape
    tm = ROW_TILE
    const = lambda b, s: (0, 0)
    row = pl.BlockSpec((1, tm, D_MODEL), lambda b, s: (b, s, 0))
    return pl.pallas_call(
        _mlp_kernel,
        grid=(B, S // tm),
        in_specs=[row, pl.BlockSpec((1, D_MODEL), const),
                  pl.BlockSpec((D_MODEL, D_FF), const),
                  pl.BlockSpec((D_FF, D_MODEL), const)],
        out_specs=row,
        out_shape=jax.ShapeDtypeStruct((B, S, D_MODEL), F32),
        compiler_params=pltpu.CompilerParams(
            dimension_semantics=("parallel", "parallel"), vmem_limit_bytes=VMEM_LIMIT),
        name="mlp",
    )(x, g, w_up, w_down)


def _scatter_lanes(w_cols):
    idx = jnp.asarray(np.where(_LANE_MAP >= 0, _LANE_MAP, 0), jnp.int32)
    valid = jnp.asarray(_LANE_MAP >= 0)
    return jnp.where(valid, jnp.take(w_cols, idx, axis=-1), 0.0)


def _prep_layer(l, positions, w_in, w_uq, w_ukv, mla_qn_g, mla_kn_g):
    o_kr = MLA_Q_RANK + MLA_KV_RANK
    w = w_in[l]
    kr_src = jnp.concatenate(
        [jnp.zeros((D_MODEL, MLA_NOPE), F32), w[:, o_kr:o_kr + MLA_ROPE]], axis=1)
    w_in_p = jnp.concatenate(
        [w[:, :o_kr], _scatter_lanes(kr_src), w[:, o_kr + MLA_ROPE:]], axis=1).astype(BF16)

    def with_swapped_halves(w_slots):
        rank = w_slots.shape[0]
        both = jnp.concatenate([w_slots, jnp.roll(w_slots, HEAD_SLOT // 2, axis=-1)], axis=1)
        return both.reshape(rank, 2 * MLA_HEADS * HEAD_SLOT).astype(BF16)

    wq = w_uq[l].reshape(MLA_Q_RANK, MLA_HEADS, MLA_QK)
    w_uq_p = with_swapped_halves(_scatter_lanes(wq))

    wkv = w_ukv[l].reshape(MLA_KV_RANK, MLA_HEADS, MLA_NOPE + MLA_V)
    k_src = jnp.concatenate(
        [wkv[..., :MLA_NOPE], jnp.zeros((MLA_KV_RANK, MLA_HEADS, MLA_ROPE), F32)], axis=-1)
    w_uk_p = with_swapped_halves(_scatter_lanes(k_src))
    w_vt = wkv[..., MLA_NOPE:].reshape(MLA_KV_RANK, MLA_HEADS * MLA_V).T.astype(BF16)

    inv_freq = ROPE_BASE ** (-jnp.arange(0, MLA_ROPE, 2, dtype=F32) / MLA_ROPE)
    ang = positions.astype(F32)[:, None] * inv_freq[None, :]
    cos, sin = jnp.cos(ang), jnp.sin(ang)
    S = positions.shape[0]
    pad = HEAD_SLOT // 2 - ROPE_HALF
    c_tab = jnp.concatenate([cos, jnp.ones((S, pad), F32), cos, jnp.ones((S, pad), F32)], axis=1)
    s_tab = jnp.concatenate([-sin, jnp.zeros((S, pad), F32), sin, jnp.zeros((S, pad), F32)], axis=1)

    def tables(gain, scale):
        g = _scatter_lanes(gain)[None, :] * scale
        return c_tab * g, s_tab * jnp.roll(g, HEAD_SLOT // 2, axis=1)

    cq_t, sq_t = tables(mla_qn_g[l], math.log2(math.e) / math.sqrt(MLA_QK))
    ck_t, sk_t = tables(mla_kn_g[l], 1.0)
    return w_in_p, w_uq_p, w_uk_p, w_vt, (cq_t, sq_t, ck_t, sk_t)


def kernel(x, mem, positions, mix_norm_g, w_in, mla_q_norm_g, mla_kv_norm_g, w_uq, w_ukv,
           mla_qn_g, mla_kn_g, hgrn_lb_logits, hgrn_o_norm_g, conv_w, w_out,
           xattn_norm_g, mem_norm_g, w_xq, w_xkv, xq_norm_g, xk_norm_g, w_xo,
           mlp_norm_g, w_up, w_down):
    B, S, _ = x.shape
    L = w_in.shape[0]
    nq, nk = S // ATT_TQ, S // ATT_TK
    pq = positions.reshape(nq, ATT_TQ)
    pk = positions.reshape(nk, ATT_TK)
    vis = jnp.min(pk, axis=1)[None, :] <= jnp.max(pq, axis=1)[:, None]
    n_kv = jnp.max(jnp.where(vis, jnp.arange(1, nk + 1, dtype=jnp.int32)[None, :], 0),
                   axis=1).astype(jnp.int32)
    full = jnp.max(pk, axis=1)[None, :] <= jnp.min(pq, axis=1)[:, None]
    n_full = jnp.sum(jnp.cumprod(full.astype(jnp.int32), axis=1), axis=1).astype(jnp.int32)
    posq = positions.reshape(nq, 1, ATT_TQ)
    posk = positions.reshape(S, 1)

    k_mem, v_mem = _mem_kv(mem, mem_norm_g.reshape(L, 1, D_MODEL), w_xkv.astype(BF16),
                           xk_norm_g.reshape(L, 1, X_HEAD_DIM))
    for l in range(L):
        w_in_p, w_uq_p, w_uk_p, w_vt, tabs = _prep_layer(l, positions, w_in, w_uq, w_ukv,
                                                         mla_qn_g, mla_kn_g)
        lat, hc = _in_proj(x, mix_norm_g[l][None], w_in_p)
        q, k, vt = _mla_qkv(lat, mla_q_norm_g[l][None], w_uq_p, mla_kv_norm_g[l][None],
                            w_uk_p, w_vt, tabs)
        score_bound = (math.log2(math.e) * math.sqrt(MLA_QK)
                       * jnp.max(jnp.abs(mla_qn_g[l])) * jnp.max(jnp.abs(mla_kn_g[l])))
        y_mla = lax.cond(score_bound <= ATT_PLAIN_MAX_SCORE,
                         functools.partial(_attention, online=False),
                         functools.partial(_attention, online=True),
                         q, k, vt, n_full, n_kv, posq, posk)
        y_hc = _hgrn_conv(hc, hgrn_lb_logits, hgrn_o_norm_g[l][None], conv_w[l], l)
        x = _out_xattn(x, y_mla, y_hc, w_out[l].astype(BF16), xattn_norm_g[l][None],
                       w_xq[l].astype(BF16), xq_norm_g[l][None], k_mem, v_mem,
                       w_xo[l].astype(BF16), l)
        x = _mlp(x, mlp_norm_g[l][None], w_up[l].astype(BF16), w_down[l].astype(BF16))
    return x
```

```python
import functools
import math

import numpy as np
import jax
import jax.numpy as jnp
from jax import lax
from jax.experimental import pallas as pl
from jax.experimental.pallas import tpu as pltpu

F32 = jnp.float32
BF16 = jnp.bfloat16

D_MODEL = 1024
DEPTH = 4
MEM_LEN = 256
EPS = 1e-6

MLA_HEADS = 8
MLA_NOPE = 64
MLA_ROPE = 32
MLA_V = 64
MLA_QK = MLA_NOPE + MLA_ROPE
MLA_Q_RANK = 384
MLA_KV_RANK = 256
ROPE_BASE = 10000.0

HG_HEADS = 4
HG_DK = 64
HG_CHUNK = 64
HG_SUB = 16
HG_GROUP = 4
HG_WIDTH = 256
CONV_WIDTH = 256
CONV_K = 3

X_HEADS = 4
X_HEAD_DIM = 128
D_FF = 4 * D_MODEL

LANES = 128
HEAD_SLOT = LANES
ROPE_HALF = MLA_ROPE // 2
N_IN_PAD = MLA_Q_RANK + MLA_KV_RANK + HEAD_SLOT + 4 * HG_WIDTH + 3 * CONV_WIDTH
HC_OFF = MLA_Q_RANK + MLA_KV_RANK + HEAD_SLOT
HC_WIDTH = N_IN_PAD - HC_OFF

ROW_TILE = 512
ATT_TQ = 256
ATT_TK = 256
ATT_HEADS = 8
ATT_CHAINS = 4
HG_ROWS = 512
MASK_VALUE = -1e30
ATT_PLAIN_MAX_SCORE = 64.0
EXP_CLAMP = 60.0
VMEM_LIMIT = 56 * 1024 * 1024


def _head_lane_map():
    m = -np.ones((HEAD_SLOT,), np.int64)
    m[0:ROPE_HALF] = MLA_NOPE + np.arange(ROPE_HALF)
    m[ROPE_HALF:64] = np.arange(64 - ROPE_HALF)
    m[64:64 + ROPE_HALF] = MLA_NOPE + ROPE_HALF + np.arange(ROPE_HALF)
    m[64 + ROPE_HALF:96] = (64 - ROPE_HALF) + np.arange(ROPE_HALF)
    return m


_LANE_MAP = _head_lane_map()


def _rms(x, n=None):
    n = x.shape[-1] if n is None else n
    return lax.rsqrt(jnp.sum(x * x, axis=-1, keepdims=True) * (1.0 / n) + EPS)


def _sigmoid(x):
    return 1.0 / (1.0 + jnp.exp(-x))


def _dot(a, b):
    return jnp.dot(a, b, preferred_element_type=F32)


def _dot_nt(a, b):
    return lax.dot_general(a, b, (((1,), (1,)), ((), ())), preferred_element_type=F32)


def _dot_tn(a, b):
    return lax.dot_general(a, b, (((0,), (0,)), ((), ())), preferred_element_type=F32)


def _in_proj_kernel(x_ref, g_ref, w_in_ref, lat_out, hc_out):
    x = x_ref[0]
    h = (x * _rms(x) * g_ref[...]).astype(BF16)
    lat_out[0] = _dot(h, w_in_ref[:, :HC_OFF])
    hc_out[0] = _dot(h, w_in_ref[:, HC_OFF:])


def _in_proj(x, g, w_in):
    B, S, _ = x.shape
    tm = ROW_TILE
    const = lambda b, s: (0, 0)
    return pl.pallas_call(
        _in_proj_kernel,
        grid=(B, S // tm),
        in_specs=[
            pl.BlockSpec((1, tm, D_MODEL), lambda b, s: (b, s, 0)),
            pl.BlockSpec((1, D_MODEL), const),
            pl.BlockSpec((D_MODEL, N_IN_PAD), const),
        ],
        out_specs=[
            pl.BlockSpec((1, tm, HC_OFF), lambda b, s: (b, s, 0)),
            pl.BlockSpec((1, tm, HC_WIDTH), lambda b, s: (b, s, 0)),
        ],
        out_shape=[
            jax.ShapeDtypeStruct((B, S, HC_OFF), F32),
            jax.ShapeDtypeStruct((B, S, HC_WIDTH), F32),
        ],
        compiler_params=pltpu.CompilerParams(
            dimension_semantics=("parallel", "parallel"), vmem_limit_bytes=VMEM_LIMIT),
        name="in_proj",
    )(x, g, w_in)


def _mla_qkv_kernel(lat_ref, gq_ref, w_uq_ref, gkv_ref, w_uk_ref, w_vt_ref,
                    cq_tab, sq_tab, ck_tab, sk_tab, q_out, k_out, vt_out):
    cq = lat_ref[0, :, :MLA_Q_RANK]
    ckv = lat_ref[0, :, MLA_Q_RANK:MLA_Q_RANK + MLA_KV_RANK]
    kr = lat_ref[0, :, MLA_Q_RANK + MLA_KV_RANK:]
    cqn = (cq * _rms(cq) * gq_ref[...]).astype(BF16)
    ckvn = (ckv * _rms(ckv) * gkv_ref[...]).astype(BF16)
    hw = MLA_HEADS * HEAD_SLOT
    qf = _dot(cqn, w_uq_ref[...])
    kf = _dot(ckvn, w_uk_ref[...])
    vt = _dot_nt(w_vt_ref[...], ckvn).astype(BF16)
    for j in range(vt_out.shape[1]):
        vt_out[0, j] = vt[:, j * ATT_TK:(j + 1) * ATT_TK]
    kr_sw = pltpu.roll(kr, HEAD_SLOT // 2, 1)
    cq_t, sq_t, ck_t, sk_t = cq_tab[...], sq_tab[...], ck_tab[...], sk_tab[...]
    for hd in range(MLA_HEADS):
        sl = slice(hd * HEAD_SLOT, (hd + 1) * HEAD_SLOT)
        sw = slice(hw + hd * HEAD_SLOT, hw + (hd + 1) * HEAD_SLOT)
        qh = qf[:, sl]
        qh = (qh * cq_t + qf[:, sw] * sq_t) * _rms(qh, MLA_QK)
        q_out[0, hd] = qh.astype(BF16)
        kh = kf[:, sl] + kr
        kh = (kh * ck_t + (kf[:, sw] + kr_sw) * sk_t) * _rms(kh, MLA_QK)
        k_out[0, hd] = kh.astype(BF16)


def _mla_qkv(lat, gq, w_uq, gkv, w_uk, w_vt, tabs):
    B, S, _ = lat.shape
    tm = ROW_TILE
    vw = MLA_HEADS * MLA_V
    const = lambda b, s: (0, 0)
    tab_spec = pl.BlockSpec((tm, HEAD_SLOT), lambda b, s: (s, 0))
    return pl.pallas_call(
        _mla_qkv_kernel,
        grid=(B, S // tm),
        in_specs=[
            pl.BlockSpec((1, tm, HC_OFF), lambda b, s: (b, s, 0)),
            pl.BlockSpec((1, MLA_Q_RANK), const),
            pl.BlockSpec((MLA_Q_RANK, 2 * MLA_HEADS * HEAD_SLOT), const),
            pl.BlockSpec((1, MLA_KV_RANK), const),
            pl.BlockSpec((MLA_KV_RANK, 2 * MLA_HEADS * HEAD_SLOT), const),
            pl.BlockSpec((vw, MLA_KV_RANK), const),
            tab_spec, tab_spec, tab_spec, tab_spec,
        ],
        out_specs=[
            pl.BlockSpec((1, MLA_HEADS, tm, HEAD_SLOT), lambda b, s: (b, 0, s, 0)),
            pl.BlockSpec((1, MLA_HEADS, tm, HEAD_SLOT), lambda b, s: (b, 0, s, 0)),
            pl.BlockSpec((1, tm // ATT_TK, vw, ATT_TK), lambda b, s: (b, s, 0, 0)),
        ],
        out_shape=[
            jax.ShapeDtypeStruct((B, MLA_HEADS, S, HEAD_SLOT), BF16),
            jax.ShapeDtypeStruct((B, MLA_HEADS, S, HEAD_SLOT), BF16),
            jax.ShapeDtypeStruct((B, S // ATT_TK, vw, ATT_TK), BF16),
        ],
        compiler_params=pltpu.CompilerParams(
            dimension_semantics=("parallel", "parallel"), vmem_limit_bytes=VMEM_LIMIT),
        name="mla_qkv",
    )(lat, gq, w_uq, gkv, w_uk, w_vt, *tabs)


def _attn_kernel(nfull_ref, nkv_ref, q_ref, k_ref, vt_ref, posq_ref, posk_ref, o_ref,
                 p_sc, acc_sc, *maybe_s_sc, online):
    qi = pl.program_id(2)
    n_full = nfull_ref[qi]
    n_kv = nkv_ref[qi]
    tq = q_ref.shape[2]
    nk = vt_ref.shape[1]
    heads = range(ATT_HEADS)
    qts = [q_ref[0, hd].astype(F32).T.astype(BF16) for hd in heads]
    posq = posq_ref[0]

    groups = [list(heads)[i:i + ATT_CHAINS] for i in range(0, ATT_HEADS, ATT_CHAINS)]

    def score(ki, hd):
        off = pl.multiple_of(ki * ATT_TK, ATT_TK)
        return _dot(k_ref[0, hd, pl.ds(off, ATT_TK), :], qts[hd])

    def pv(ki, hd, p=None):
        p = p_sc[hd] if p is None else p
        return _dot(vt_ref[0, ki, hd * MLA_V:(hd + 1) * MLA_V, :], p)

    def issue(ki_scores, ki_pv):
        ss, pvs = {}, {}
        for g in groups:
            for hd in g:
                ss[hd] = score(ki_scores, hd)
            for hd in g:
                pvs[hd] = pv(ki_pv, hd)
        return ss, pvs

    def mask_of(ki):
        off = pl.multiple_of(ki * ATT_TK, ATT_TK)
        return posk_ref[pl.ds(off, ATT_TK), :] <= posq

    def step_plain(ki, carry, masked):
        ss, pvs = issue(ki, jnp.maximum(ki - 1, 0))
        if masked:
            mask = mask_of(ki)
        new = []
        for hd in heads:
            s = jnp.where(mask, ss[hd], MASK_VALUE) if masked else ss[hd]
            p = jnp.exp2(s)
            new.append(carry[hd] + jnp.sum(p, axis=0, keepdims=True))
            acc_sc[hd] = acc_sc[hd] + pvs[hd]
            p_sc[hd] = p.astype(BF16)
        return tuple(new)

    def pair_plain(j, carry):
        k0 = 2 * j
        ss_a, pvs_prev = issue(k0, jnp.maximum(k0 - 1, 0))
        ss_b, la, pvs_a = {}, {}, {}
        for g in groups:
            for hd in g:
                ss_b[hd] = score(k0 + 1, hd)
            for hd in g:
                p = jnp.exp2(ss_a[hd])
                la[hd] = jnp.sum(p, axis=0, keepdims=True)
                pvs_a[hd] = pv(k0, hd, p.astype(BF16))
        new = []
        for hd in heads:
            p = jnp.exp2(ss_b[hd])
            new.append(carry[hd] + la[hd] + jnp.sum(p, axis=0, keepdims=True))
            acc_sc[hd] = acc_sc[hd] + pvs_prev[hd] + pvs_a[hd]
            p_sc[hd] = p.astype(BF16)
        return tuple(new)

    def step_online(ki, carry, masked):
        (s_sc,) = maybe_s_sc
        s_next, pvs = issue(jnp.minimum(ki + 1, nk - 1), jnp.maximum(ki - 1, 0))
        if masked:
            mask = mask_of(ki)
        new = []
        for hd in heads:
            m_old, l_old = carry[hd]
            s = s_sc[hd]
            if masked:
                s = jnp.where(mask, s, MASK_VALUE)
            m_new = jnp.maximum(m_old, jnp.max(s, axis=0, keepdims=True))
            alpha = jnp.exp2(m_old - m_new)
            p = jnp.exp2(s - m_new)
            l_new = alpha * l_old + jnp.sum(p, axis=0, keepdims=True)
            acc_sc[hd] = alpha * (acc_sc[hd] + pvs[hd])
            p_sc[hd] = p.astype(BF16)
            new.append((m_new, l_new))
        for hd in heads:
            s_sc[hd] = s_next[hd]
        return tuple(new)

    p_sc[...] = jnp.zeros(p_sc.shape, BF16)
    acc_sc[...] = jnp.zeros(acc_sc.shape, F32)
    if online:
        for hd in heads:
            maybe_s_sc[0][hd] = score(0, hd)
        init = tuple((jnp.full((1, tq), MASK_VALUE, F32), jnp.zeros((1, tq), F32))
                     for _ in heads)
        step = step_online
    else:
        init = tuple(jnp.zeros((1, tq), F32) for _ in heads)
        step = step_plain
    n_done = 0
    if not online:
        n_pairs = n_full // 2
        init = lax.fori_loop(0, n_pairs, pair_plain, init)
        n_done = 2 * n_pairs
    carry = lax.fori_loop(n_done, n_full, functools.partial(step, masked=False), init)
    carry = lax.fori_loop(n_full, n_kv, functools.partial(step, masked=True), carry)
    last = jnp.maximum(n_kv - 1, 0)
    ls = [c[1] for c in carry] if online else carry
    o_t = jnp.concatenate([(acc_sc[hd] + pv(last, hd)) / ls[hd] for hd in heads], axis=0)
    o_ref[0] = o_t.T.astype(BF16)


def _attention(q, k, vt, n_full, n_kv, posq, posk, online):
    B, H, S, _ = q.shape
    tq = ATT_TQ
    nh = ATT_HEADS
    nk = S // ATT_TK
    scratch = [pltpu.VMEM((nh, ATT_TK, tq), BF16), pltpu.VMEM((nh, MLA_V, tq), F32)]
    if online:
        scratch.append(pltpu.VMEM((nh, ATT_TK, tq), F32))
    grid_spec = pltpu.PrefetchScalarGridSpec(
        num_scalar_prefetch=2,
        grid=(B, H // nh, S // tq),
        in_specs=[
            pl.BlockSpec((1, nh, tq, HEAD_SLOT), lambda b, hp, qi, *_: (b, hp, qi, 0)),
            pl.BlockSpec((1, nh, S, HEAD_SLOT), lambda b, hp, qi, *_: (b, hp, 0, 0)),
            pl.BlockSpec((1, nk, nh * MLA_V, ATT_TK), lambda b, hp, qi, *_: (b, 0, hp, 0)),
            pl.BlockSpec((1, 1, tq), lambda b, hp, qi, *_: (qi, 0, 0)),
            pl.BlockSpec((S, 1), lambda b, hp, qi, *_: (0, 0)),
        ],
        out_specs=pl.BlockSpec((1, tq, nh * MLA_V), lambda b, hp, qi, *_: (b, qi, hp)),
        scratch_shapes=scratch,
    )
    return pl.pallas_call(
        functools.partial(_attn_kernel, online=online),
        grid_spec=grid_spec,
        out_shape=jax.ShapeDtypeStruct((B, S, H * MLA_V), BF16),
        compiler_params=pltpu.CompilerParams(
            dimension_semantics=("parallel", "parallel", "arbitrary"),
            vmem_limit_bytes=VMEM_LIMIT),
        name="mla_attention_online" if online else "mla_attention",
    )(n_full, n_kv, q, k, vt, posq, posk)


def _hgrn_conv_kernel(hc_ref, lbl_ref, onorm_ref, convw_ref, y_ref,
                      st_sc, ubuf_sc, o_sc, *, layer):
    sb = pl.program_id(1)
    R = HG_ROWS
    C = HG_CHUNK
    W = HG_WIDTH

    @pl.when(sb == 0)
    def _():
        st_sc[...] = jnp.zeros(st_sc.shape, F32)
        ubuf_sc[0:8, :] = jnp.zeros((8, CONV_WIDTH), F32)

    lg = lbl_ref[...]
    e = jnp.exp(lg - jnp.max(lg, axis=0, keepdims=True))
    soft = e / jnp.sum(e, axis=0, keepdims=True)
    lb = jnp.zeros((1, W), F32)
    for i in range(1, layer + 1):
        lb = lb + soft[i:i + 1, :]
    lb = jnp.maximum(lb, 0.0)
    log_lb = jnp.log(lb)
    log_1m = jnp.log1p(-lb)

    ri = lax.broadcasted_iota(jnp.int32, (C, C), 0)
    ci = lax.broadcasted_iota(jnp.int32, (C, C), 1)
    sub_shift = HG_SUB.bit_length() - 1
    dk_shift = HG_DK.bit_length() - 1
    tri = jnp.where(ci <= ri, 1.0, 0.0)
    tri_blk = jnp.where((ci >> sub_shift) == (ri >> sub_shift), tri, 0.0)
    cum_mat = jnp.concatenate([tri, tri_blk], axis=0).astype(BF16)
    tt = ri & (HG_SUB - 1)
    lane_w = lax.broadcasted_iota(jnp.int32, (1, W), 1)
    head_masks = [jnp.where((lane_w >> dk_shift) == hh, 1.0, 0.0) for hh in range(HG_HEADS)]
    rw = lax.broadcasted_iota(jnp.int32, (W, W), 0)
    cw = lax.broadcasted_iota(jnp.int32, (W, W), 1)
    bd_mask = (rw >> dk_shift) == (cw >> dk_shift)
    n_sub = C // HG_SUB

    def group(gi, carry):
        base = gi * (HG_GROUP * C)
        starts = [pl.multiple_of(base + j * C, C) for j in range(HG_GROUP)]
        chunks = range(HG_GROUP)
        gates, splits = [], []
        for r0 in starts:
            zq = hc_ref[0, pl.ds(r0, C), 0:W]
            zf = hc_ref[0, pl.ds(r0, C), W:2 * W]
            vi = hc_ref[0, pl.ds(r0, C), 2 * W:3 * W]
            q = zq * _sigmoid(zq)
            log_sig = jnp.minimum(zf, 0.0) - jnp.log1p(jnp.exp(-jnp.abs(zf)))
            c2 = log_1m + log_sig
            log_f = jnp.maximum(log_lb, c2) + jnp.log1p(jnp.exp(-jnp.abs(log_lb - c2)))
            kk = (1.0 - lb) * _sigmoid(-zf)
            f_hi = log_f.astype(BF16)
            r1 = log_f - f_hi.astype(F32)
            f_mid = r1.astype(BF16)
            f_lo = (r1 - f_mid.astype(F32)).astype(BF16)
            gates.append((q, kk, vi.astype(BF16)))
            splits.append((f_hi, f_mid, f_lo))
        cums = [_dot(cum_mat, s[0]) + _dot(cum_mat, s[1]) + _dot(cum_mat, s[2]) for s in splits]
        qbs, b_lasts, lhss, kds, kdls = [], [], [], [], []
        for j in chunks:
            q, kk, _ = gates[j]
            b = cums[j][:C]
            g = cums[j][C:]
            b_last = b[C - 1:C, :]
            rref = b - g
            qd = q * jnp.exp(g)
            qbs.append((q * jnp.exp(b)).astype(BF16))
            b_lasts.append(b_last)
            kdls.append((kk * jnp.exp(b_last - b)).astype(BF16))
            lhs_j, kd_j = [], []
            for i in range(n_sub):
                rr = rref[i * HG_SUB:i * HG_SUB + 1, :]
                kd_j.append((kk * jnp.exp(jnp.minimum(rr - b, EXP_CLAMP))).astype(BF16))
                qi = qd[i * HG_SUB:(i + 1) * HG_SUB]
                lhs_j.append(jnp.concatenate([qi * hm for hm in head_masks],
                                             axis=0).astype(BF16))
            lhss.append(lhs_j)
            kds.append(kd_j)
        attn = [[_dot_nt(lhss[j][i], kds[j][i]) for i in range(n_sub)] for j in chunks]
        upds = [_dot_tn(gates[j][2], kdls[j]) for j in chunks]
        attn = [[jnp.where(ci <= i * HG_SUB + tt, attn[j][i], 0.0).astype(BF16)
                 for i in range(n_sub)] for j in chunks]
        pvs = [[_dot(attn[j][i], gates[j][2]) for i in range(n_sub)] for j in chunks]
        st = st_sc[...]
        for j in chunks:
            o_parts = []
            for i in range(n_sub):
                pv = pvs[j][i]
                oi = pv[0:HG_SUB] * head_masks[0]
                for hh in range(1, HG_HEADS):
                    oi = oi + pv[hh * HG_SUB:(hh + 1) * HG_SUB] * head_masks[hh]
                o_parts.append(oi)
            o_inter = _dot_nt(qbs[j], st.astype(BF16))
            o_sc[pl.ds(starts[j], C), :] = o_inter + jnp.concatenate(o_parts, axis=0)
            st = st * jnp.exp(b_lasts[j]) + jnp.where(bd_mask, upds[j], 0.0)
        st_sc[...] = st
        return carry

    lax.fori_loop(0, R // (C * HG_GROUP), group, 0)

    o = o_sc[...]
    o2 = o * o
    o2_hi = o2.astype(BF16)
    o2_lo = (o2 - o2_hi.astype(F32)).astype(BF16)
    ones_bd = jnp.where(bd_mask, 1.0, 0.0).astype(BF16)
    ms = (_dot(o2_hi, ones_bd) + _dot(o2_lo, ones_bd)) * (1.0 / HG_DK)
    zg = hc_ref[0, :, 3 * W:4 * W]
    y_hg = o * lax.rsqrt(ms + EPS) * onorm_ref[...] * (zg * _sigmoid(zg))
    y_ref[0, :, 0:W] = y_hg.astype(BF16)

    cb = hc_ref[0, :, 4 * W:4 * W + CONV_WIDTH]
    cc = hc_ref[0, :, 4 * W + CONV_WIDTH:4 * W + 2 * CONV_WIDTH]
    cx = hc_ref[0, :, 4 * W + 2 * CONV_WIDTH:4 * W + 3 * CONV_WIDTH]
    u = cc * cx
    ubuf_sc[8:8 + R, :] = u
    u1 = ubuf_sc[7:7 + R, :]
    u2 = ubuf_sc[6:6 + R, :]
    wc = convw_ref[...]
    y_cv = cb * (u2 * wc[0:1, :] + u1 * wc[1:2, :] + u * wc[2:3, :])
    y_ref[0, :, W:W + CONV_WIDTH] = y_cv.astype(BF16)
    ubuf_sc[0:8, :] = ubuf_sc[R:R + 8, :]


def _hgrn_conv(hc, lb_logits, onorm_g, conv_w, layer):
    B, S, _ = hc.shape
    R = HG_ROWS
    const = lambda b, s: (0, 0)
    return pl.pallas_call(
        functools.partial(_hgrn_conv_kernel, layer=layer),
        grid=(B, S // R),
        in_specs=[
            pl.BlockSpec((1, R, HC_WIDTH), lambda b, s: (b, s, 0)),
            pl.BlockSpec((DEPTH, HG_WIDTH), const),
            pl.BlockSpec((1, HG_WIDTH), const),
            pl.BlockSpec((CONV_K, CONV_WIDTH), const),
        ],
        out_specs=pl.BlockSpec((1, R, HG_WIDTH + CONV_WIDTH), lambda b, s: (b, s, 0)),
        out_shape=jax.ShapeDtypeStruct((B, S, HG_WIDTH + CONV_WIDTH), BF16),
        scratch_shapes=[
            pltpu.VMEM((HG_WIDTH, HG_WIDTH), F32),
            pltpu.VMEM((R + 8, CONV_WIDTH), F32),
            pltpu.VMEM((R, HG_WIDTH), F32),
        ],
        compiler_params=pltpu.CompilerParams(
            dimension_semantics=("parallel", "arbitrary"), vmem_limit_bytes=VMEM_LIMIT),
        name="hgrn_conv",
    )(hc, lb_logits, onorm_g, conv_w)


def _mem_kv_kernel(mem_ref, g_ref, w_ref, gk_ref, k_out, v_out):
    m = mem_ref[0]
    mn = m * _rms(m) * g_ref[0]
    kv = _dot(mn.astype(BF16), w_ref[0])
    hw = X_HEADS * X_HEAD_DIM
    gk = gk_ref[0]
    for hd in range(X_HEADS):
        sl = slice(hd * X_HEAD_DIM, (hd + 1) * X_HEAD_DIM)
        kh = kv[:, sl]
        k_out[0, 0, :, sl] = (kh * _rms(kh) * gk).astype(BF16)
    v_out[0, 0] = kv[:, hw:].astype(BF16)


def _mem_kv(mem, mem_norm_g, w_xkv, xk_norm_g):
    B, M, _ = mem.shape
    L = w_xkv.shape[0]
    hw = X_HEADS * X_HEAD_DIM
    out_spec = pl.BlockSpec((1, 1, M, hw), lambda l, b: (l, b, 0, 0))
    return pl.pallas_call(
        _mem_kv_kernel,
        grid=(L, B),
        in_specs=[
            pl.BlockSpec((1, M, D_MODEL), lambda l, b: (b, 0, 0)),
            pl.BlockSpec((1, 1, D_MODEL), lambda l, b: (l, 0, 0)),
            pl.BlockSpec((1, D_MODEL, 2 * hw), lambda l, b: (l, 0, 0)),
            pl.BlockSpec((1, 1, X_HEAD_DIM), lambda l, b: (l, 0, 0)),
        ],
        out_specs=[out_spec, out_spec],
        out_shape=[jax.ShapeDtypeStruct((L, B, M, hw), BF16)] * 2,
        compiler_params=pltpu.CompilerParams(
            dimension_semantics=("parallel", "parallel"), vmem_limit_bytes=VMEM_LIMIT),
        name="mem_kv",
    )(mem, mem_norm_g, w_xkv, xk_norm_g)


def _out_xattn_kernel(x_ref, ya_ref, yb_ref, wo_ref, g_ref, wq_ref, gq_ref,
                      k_ref, v_ref, wxo_ref, o_ref):
    half = wo_ref.shape[0] // 2
    x1 = x_ref[0] + _dot(ya_ref[0], wo_ref[:half, :]) + _dot(yb_ref[0], wo_ref[half:, :])
    h = x1 * _rms(x1) * g_ref[...]
    q = _dot(h.astype(BF16), wq_ref[...])
    gq = gq_ref[...] * (1.0 / math.sqrt(X_HEAD_DIM))
    outs = []
    for hd in range(X_HEADS):
        sl = slice(hd * X_HEAD_DIM, (hd + 1) * X_HEAD_DIM)
        qh = q[:, sl]
        qh = (qh * _rms(qh) * gq).astype(BF16)
        s = _dot_nt(qh, k_ref[0, 0, :, sl])
        p = jnp.exp(s - jnp.max(s, axis=1, keepdims=True))
        l = jnp.sum(p, axis=1, keepdims=True)
        outs.append(_dot(p.astype(BF16), v_ref[0, 0, :, sl]) / l)
    o = jnp.concatenate(outs, axis=1).astype(BF16)
    o_ref[0] = x1 + _dot(o, wxo_ref[...])


def _out_xattn(x, y_mla, y_hc, w_out, g, w_xq, gq, k_mem, v_mem, w_xo, layer):
    B, S, _ = x.shape
    tm = ROW_TILE
    hw = X_HEADS * X_HEAD_DIM
    const = lambda b, s: (0, 0)
    row = lambda w: pl.BlockSpec((1, tm, w), lambda b, s: (b, s, 0))
    mem_spec = pl.BlockSpec((1, 1, MEM_LEN, hw), lambda b, s: (layer, b, 0, 0))
    return pl.pallas_call(
        _out_xattn_kernel,
        grid=(B, S // tm),
        in_specs=[
            row(D_MODEL), row(y_mla.shape[-1]), row(y_hc.shape[-1]),
            pl.BlockSpec((D_MODEL, D_MODEL), const),
            pl.BlockSpec((1, D_MODEL), const),
            pl.BlockSpec((D_MODEL, hw), const),
            pl.BlockSpec((1, X_HEAD_DIM), const),
            mem_spec, mem_spec,
            pl.BlockSpec((hw, D_MODEL), const),
        ],
        out_specs=row(D_MODEL),
        out_shape=jax.ShapeDtypeStruct((B, S, D_MODEL), F32),
        compiler_params=pltpu.CompilerParams(
            dimension_semantics=("parallel", "parallel"), vmem_limit_bytes=VMEM_LIMIT),
        name="out_xattn",
    )(x, y_mla, y_hc, w_out, g, w_xq, gq, k_mem, v_mem, w_xo)


def _mlp_kernel(x_ref, g_ref, wu_ref, wd_ref, o_ref):
    x = x_ref[0]
    h = (x * _rms(x) * g_ref[...]).astype(BF16)
    acc = x
    step = D_MODEL
    for c in range(D_FF // step):
        u = _dot(h, wu_ref[:, c * step:(c + 1) * step])
        a = jnp.square(jnp.maximum(u, 0.0)).astype(BF16)
        acc = acc + _dot(a, wd_ref[c * step:(c + 1) * step, :])
    o_ref[0] = acc


def _mlp(x, g, w_up, w_down):
    B, S, _ = x.shape
    tm = ROW_TILE
    const = lambda b, s: (0, 0)
    row = pl.BlockSpec((1, tm, D_MODEL), lambda b, s: (b, s, 0))
    return pl.pallas_call(
        _mlp_kernel,
        grid=(B, S // tm),
        in_specs=[row, pl.BlockSpec((1, D_MODEL), const),
                  pl.BlockSpec((D_MODEL, D_FF), const),
                  pl.BlockSpec((D_FF, D_MODEL), const)],
        out_specs=row,
        out_shape=jax.ShapeDtypeStruct((B, S, D_MODEL), F32),
        compiler_params=pltpu.CompilerParams(
            dimension_semantics=("parallel", "parallel"), vmem_limit_bytes=VMEM_LIMIT),
        name="mlp",
    )(x, g, w_up, w_down)


def _scatter_lanes(w_cols):
    idx = jnp.asarray(np.where(_LANE_MAP >= 0, _LANE_MAP, 0), jnp.int32)
    valid = jnp.asarray(_LANE_MAP >= 0)
    return jnp.where(valid, jnp.take(w_cols, idx, axis=-1), 0.0)


def _prep_layer(l, positions, w_in, w_uq, w_ukv, mla_qn_g, mla_kn_g):
    o_kr = MLA_Q_RANK + MLA_KV_RANK
    w = w_in[l]
    kr_src = jnp.concatenate(
        [jnp.zeros((D_MODEL, MLA_NOPE), F32), w[:, o_kr:o_kr + MLA_ROPE]], axis=1)
    w_in_p = jnp.concatenate(
        [w[:, :o_kr], _scatter_lanes(kr_src), w[:, o_kr + MLA_ROPE:]], axis=1).astype(BF16)

    def with_swapped_halves(w_slots):
        rank = w_slots.shape[0]
        both = jnp.concatenate([w_slots, jnp.roll(w_slots, HEAD_SLOT // 2, axis=-1)], axis=1)
        return both.reshape(rank, 2 * MLA_HEADS * HEAD_SLOT).astype(BF16)

    wq = w_uq[l].reshape(MLA_Q_RANK, MLA_HEADS, MLA_QK)
    w_uq_p = with_swapped_halves(_scatter_lanes(wq))

    wkv = w_ukv[l].reshape(MLA_KV_RANK, MLA_HEADS, MLA_NOPE + MLA_V)
    k_src = jnp.concatenate(
        [wkv[..., :MLA_NOPE], jnp.zeros((MLA_KV_RANK, MLA_HEADS, MLA_ROPE), F32)], axis=-1)
    w_uk_p = with_swapped_halves(_scatter_lanes(k_src))
    w_vt = wkv[..., MLA_NOPE:].reshape(MLA_KV_RANK, MLA_HEADS * MLA_V).T.astype(BF16)

    inv_freq = ROPE_BASE ** (-jnp.arange(0, MLA_ROPE, 2, dtype=F32) / MLA_ROPE)
    ang = positions.astype(F32)[:, None] * inv_freq[None, :]
    cos, sin = jnp.cos(ang), jnp.sin(ang)
    S = positions.shape[0]
    pad = HEAD_SLOT // 2 - ROPE_HALF
    c_tab = jnp.concatenate([cos, jnp.ones((S, pad), F32), cos, jnp.ones((S, pad), F32)], axis=1)
    s_tab = jnp.concatenate([-sin, jnp.zeros((S, pad), F32), sin, jnp.zeros((S, pad), F32)], axis=1)

    def tables(gain, scale):
        g = _scatter_lanes(gain)[None, :] * scale
        return c_tab * g, s_tab * jnp.roll(g, HEAD_SLOT // 2, axis=1)

    cq_t, sq_t = tables(mla_qn_g[l], math.log2(math.e) / math.sqrt(MLA_QK))
    ck_t, sk_t = tables(mla_kn_g[l], 1.0)
    return w_in_p, w_uq_p, w_uk_p, w_vt, (cq_t, sq_t, ck_t, sk_t)


def kernel(x, mem, positions, mix_norm_g, w_in, mla_q_norm_g, mla_kv_norm_g, w_uq, w_ukv,
           mla_qn_g, mla_kn_g, hgrn_lb_logits, hgrn_o_norm_g, conv_w, w_out,
           xattn_norm_g, mem_norm_g, w_xq, w_xkv, xq_norm_g, xk_norm_g, w_xo,
           mlp_norm_g, w_up, w_down):
    B, S, _ = x.shape
    L = w_in.shape[0]
    nq, nk = S // ATT_TQ, S // ATT_TK
    pq = positions.reshape(nq, ATT_TQ)
    pk = positions.reshape(nk, ATT_TK)
    vis = jnp.min(pk, axis=1)[None, :] <= jnp.max(pq, axis=1)[:, None]
    n_kv = jnp.max(jnp.where(vis, jnp.arange(1, nk + 1, dtype=jnp.int32)[None, :], 0),
                   axis=1).astype(jnp.int32)
    full = jnp.max(pk, axis=1)[None, :] <= jnp.min(pq, axis=1)[:, None]
    n_full = jnp.sum(jnp.cumprod(full.astype(jnp.int32), axis=1), axis=1).astype(jnp.int32)
    posq = positions.reshape(nq, 1, ATT_TQ)
    posk = positions.reshape(S, 1)

    k_mem, v_mem = _mem_kv(mem, mem_norm_g.reshape(L, 1, D_MODEL), w_xkv.astype(BF16),
                           xk_norm_g.reshape(L, 1, X_HEAD_DIM))
    for l in range(L):
        w_in_p, w_uq_p, w_uk_p, w_vt, tabs = _prep_layer(l, positions, w_in, w_uq, w_ukv,
                                                         mla_qn_g, mla_kn_g)
        lat, hc = _in_proj(x, mix_norm_g[l][None], w_in_p)
        q, k, vt = _mla_qkv(lat, mla_q_norm_g[l][None], w_uq_p, mla_kv_norm_g[l][None],
                            w_uk_p, w_vt, tabs)
        score_bound = (math.log2(math.e) * math.sqrt(MLA_QK)
                       * jnp.max(jnp.abs(mla_qn_g[l])) * jnp.max(jnp.abs(mla_kn_g[l])))
        y_mla = lax.cond(score_bound <= ATT_PLAIN_MAX_SCORE,
                         functools.partial(_attention, online=False),
                         functools.partial(_attention, online=True),
                         q, k, vt, n_full, n_kv, posq, posk)
        y_hc = _hgrn_conv(hc, hgrn_lb_logits, hgrn_o_norm_g[l][None], conv_w[l], l)
        x = _out_xattn(x, y_mla, y_hc, w_out[l].astype(BF16), xattn_norm_g[l][None],
                       w_xq[l].astype(BF16), xq_norm_g[l][None], k_mem, v_mem,
                       w_xo[l].astype(BF16), l)
        x = _mlp(x, mlp_norm_g[l][None], w_up[l].astype(BF16), w_down[l].astype(BF16))
    return x
```

```python
import functools
import math

import numpy as np
import jax
import jax.numpy as jnp
from jax import lax
from jax.experimental import pallas as pl
from jax.experimental.pallas import tpu as pltpu

F32 = jnp.float32
BF16 = jnp.bfloat16

D_MODEL = 1024
DEPTH = 4
MEM_LEN = 256
EPS = 1e-6

MLA_HEADS = 8
MLA_NOPE = 64
MLA_ROPE = 32
MLA_V = 64
MLA_QK = MLA_NOPE + MLA_ROPE
MLA_Q_RANK = 384
MLA_KV_RANK = 256
ROPE_BASE = 10000.0

HG_HEADS = 4
HG_DK = 64
HG_CHUNK = 64
HG_SUB = 16
HG_GROUP = 4
HG_WIDTH = 256
CONV_WIDTH = 256
CONV_K = 3

X_HEADS = 4
X_HEAD_DIM = 128
D_FF = 4 * D_MODEL

LANES = 128
HEAD_SLOT = LANES
ROPE_HALF = MLA_ROPE // 2
N_IN_PAD = MLA_Q_RANK + MLA_KV_RANK + HEAD_SLOT + 4 * HG_WIDTH + 3 * CONV_WIDTH
HC_OFF = MLA_Q_RANK + MLA_KV_RANK + HEAD_SLOT
HC_WIDTH = N_IN_PAD - HC_OFF

ROW_TILE = 512
ATT_TQ = 256
ATT_TK = 256
ATT_HEADS = 8
ATT_CHAINS = 4
ATT_UNROLLS = (4, 2)
HG_ROWS = 512
MASK_VALUE = -1e30
ATT_PLAIN_MAX_SCORE = 64.0
EXP_CLAMP = 60.0
VMEM_LIMIT = 56 * 1024 * 1024


def _head_lane_map():
    m = -np.ones((HEAD_SLOT,), np.int64)
    m[0:ROPE_HALF] = MLA_NOPE + np.arange(ROPE_HALF)
    m[ROPE_HALF:64] = np.arange(64 - ROPE_HALF)
    m[64:64 + ROPE_HALF] = MLA_NOPE + ROPE_HALF + np.arange(ROPE_HALF)
    m[64 + ROPE_HALF:96] = (64 - ROPE_HALF) + np.arange(ROPE_HALF)
    return m


_LANE_MAP = _head_lane_map()


def _rms(x, n=None):
    n = x.shape[-1] if n is None else n
    return lax.rsqrt(jnp.sum(x * x, axis=-1, keepdims=True) * (1.0 / n) + EPS)


def _sigmoid(x):
    return 1.0 / (1.0 + jnp.exp(-x))


def _dot(a, b):
    return jnp.dot(a, b, preferred_element_type=F32)


def _dot_nt(a, b):
    return lax.dot_general(a, b, (((1,), (1,)), ((), ())), preferred_element_type=F32)


def _dot_tn(a, b):
    return lax.dot_general(a, b, (((0,), (0,)), ((), ())), preferred_element_type=F32)


def _in_proj_kernel(x_ref, g_ref, w_in_ref, lat_out, hc_out):
    x = x_ref[0]
    h = (x * _rms(x) * g_ref[...]).astype(BF16)
    lat_out[0] = _dot(h, w_in_ref[:, :HC_OFF])
    hc_out[0] = _dot(h, w_in_ref[:, HC_OFF:])


def _in_proj(x, g, w_in):
    B, S, _ = x.shape
    tm = ROW_TILE
    const = lambda b, s: (0, 0)
    return pl.pallas_call(
        _in_proj_kernel,
        grid=(B, S // tm),
        in_specs=[
            pl.BlockSpec((1, tm, D_MODEL), lambda b, s: (b, s, 0)),
            pl.BlockSpec((1, D_MODEL), const),
            pl.BlockSpec((D_MODEL, N_IN_PAD), const),
        ],
        out_specs=[
            pl.BlockSpec((1, tm, HC_OFF), lambda b, s: (b, s, 0)),
            pl.BlockSpec((1, tm, HC_WIDTH), lambda b, s: (b, s, 0)),
        ],
        out_shape=[
            jax.ShapeDtypeStruct((B, S, HC_OFF), F32),
            jax.ShapeDtypeStruct((B, S, HC_WIDTH), F32),
        ],
        compiler_params=pltpu.CompilerParams(
            dimension_semantics=("parallel", "parallel"), vmem_limit_bytes=VMEM_LIMIT),
        name="in_proj",
    )(x, g, w_in)


def _mla_qkv_kernel(lat_ref, gq_ref, w_uq_ref, gkv_ref, w_uk_ref, w_vt_ref,
                    cq_tab, sq_tab, ck_tab, sk_tab, q_out, k_out, vt_out):
    cq = lat_ref[0, :, :MLA_Q_RANK]
    ckv = lat_ref[0, :, MLA_Q_RANK:MLA_Q_RANK + MLA_KV_RANK]
    kr = lat_ref[0, :, MLA_Q_RANK + MLA_KV_RANK:]
    cqn = (cq * _rms(cq) * gq_ref[...]).astype(BF16)
    ckvn = (ckv * _rms(ckv) * gkv_ref[...]).astype(BF16)
    hw = MLA_HEADS * HEAD_SLOT
    qf = _dot(cqn, w_uq_ref[...])
    kf = _dot(ckvn, w_uk_ref[...])
    vt = _dot_nt(w_vt_ref[...], ckvn).astype(BF16)
    for j in range(vt_out.shape[1]):
        vt_out[0, j] = vt[:, j * ATT_TK:(j + 1) * ATT_TK]
    kr_sw = pltpu.roll(kr, HEAD_SLOT // 2, 1)
    cq_t, sq_t, ck_t, sk_t = cq_tab[...], sq_tab[...], ck_tab[...], sk_tab[...]
    for hd in range(MLA_HEADS):
        sl = slice(hd * HEAD_SLOT, (hd + 1) * HEAD_SLOT)
        sw = slice(hw + hd * HEAD_SLOT, hw + (hd + 1) * HEAD_SLOT)
        qh = qf[:, sl]
        qh = (qh * cq_t + qf[:, sw] * sq_t) * _rms(qh, MLA_QK)
        q_out[0, hd] = qh.T.astype(BF16)
        kh = kf[:, sl] + kr
        kh = (kh * ck_t + (kf[:, sw] + kr_sw) * sk_t) * _rms(kh, MLA_QK)
        k_out[0, hd] = kh.astype(BF16)


def _mla_qkv(lat, gq, w_uq, gkv, w_uk, w_vt, tabs):
    B, S, _ = lat.shape
    tm = ROW_TILE
    vw = MLA_HEADS * MLA_V
    const = lambda b, s: (0, 0)
    tab_spec = pl.BlockSpec((tm, HEAD_SLOT), lambda b, s: (s, 0))
    return pl.pallas_call(
        _mla_qkv_kernel,
        grid=(B, S // tm),
        in_specs=[
            pl.BlockSpec((1, tm, HC_OFF), lambda b, s: (b, s, 0)),
            pl.BlockSpec((1, MLA_Q_RANK), const),
            pl.BlockSpec((MLA_Q_RANK, 2 * MLA_HEADS * HEAD_SLOT), const),
            pl.BlockSpec((1, MLA_KV_RANK), const),
            pl.BlockSpec((MLA_KV_RANK, 2 * MLA_HEADS * HEAD_SLOT), const),
            pl.BlockSpec((vw, MLA_KV_RANK), const),
            tab_spec, tab_spec, tab_spec, tab_spec,
        ],
        out_specs=[
            pl.BlockSpec((1, MLA_HEADS, HEAD_SLOT, tm), lambda b, s: (b, 0, 0, s)),
            pl.BlockSpec((1, MLA_HEADS, tm, HEAD_SLOT), lambda b, s: (b, 0, s, 0)),
            pl.BlockSpec((1, tm // ATT_TK, vw, ATT_TK), lambda b, s: (b, s, 0, 0)),
        ],
        out_shape=[
            jax.ShapeDtypeStruct((B, MLA_HEADS, HEAD_SLOT, S), BF16),
            jax.ShapeDtypeStruct((B, MLA_HEADS, S, HEAD_SLOT), BF16),
            jax.ShapeDtypeStruct((B, S // ATT_TK, vw, ATT_TK), BF16),
        ],
        compiler_params=pltpu.CompilerParams(
            dimension_semantics=("parallel", "parallel"), vmem_limit_bytes=VMEM_LIMIT),
        name="mla_qkv",
    )(lat, gq, w_uq, gkv, w_uk, w_vt, *tabs)


def _attn_kernel(nfull_ref, nkv_ref, q_ref, k_ref, vt_ref, posq_ref, posk_ref, o_ref,
                 p_sc, acc_sc, *maybe_s_sc, online):
    qi = pl.program_id(2)
    n_full = nfull_ref[qi]
    n_kv = nkv_ref[qi]
    tq = q_ref.shape[3]
    nk = vt_ref.shape[1]
    heads = range(ATT_HEADS)
    qts = [q_ref[0, hd] for hd in heads]
    posq = posq_ref[0]

    groups = [list(heads)[i:i + ATT_CHAINS] for i in range(0, ATT_HEADS, ATT_CHAINS)]

    def score(ki, hd):
        off = pl.multiple_of(ki * ATT_TK, ATT_TK)
        return _dot(k_ref[0, hd, pl.ds(off, ATT_TK), :], qts[hd])

    def pv(ki, hd, p=None):
        p = p_sc[hd] if p is None else p
        return _dot(vt_ref[0, ki, hd * MLA_V:(hd + 1) * MLA_V, :], p)

    def issue(ki_scores, ki_pv):
        ss, pvs = {}, {}
        for g in groups:
            for hd in g:
                ss[hd] = score(ki_scores, hd)
            for hd in g:
                pvs[hd] = pv(ki_pv, hd)
        return ss, pvs

    def mask_of(ki):
        off = pl.multiple_of(ki * ATT_TK, ATT_TK)
        return posk_ref[pl.ds(off, ATT_TK), :] <= posq

    def step_plain(ki, carry, masked):
        ss, pvs = issue(ki, jnp.maximum(ki - 1, 0))
        if masked:
            mask = mask_of(ki)
        new = []
        for hd in heads:
            s = jnp.where(mask, ss[hd], MASK_VALUE) if masked else ss[hd]
            p = jnp.exp2(s)
            new.append(carry[hd] + jnp.sum(p, axis=0, keepdims=True))
            acc_sc[hd] = acc_sc[hd] + pvs[hd]
            p_sc[hd] = p.astype(BF16)
        return tuple(new)

    def multi_plain(j, carry, n_tiles, first):
        k0 = first + n_tiles * j
        ss, pvs = issue(k0, jnp.maximum(k0 - 1, 0))
        l_add = {hd: 0.0 for hd in heads}
        for hd in heads:
            acc_sc[hd] = acc_sc[hd] + pvs[hd]
        for u in range(1, n_tiles):
            ss_next = {}
            for g in groups:
                for hd in g:
                    ss_next[hd] = score(k0 + u, hd)
                for hd in g:
                    p = jnp.exp2(ss[hd])
                    l_add[hd] = l_add[hd] + jnp.sum(p, axis=0, keepdims=True)
                    acc_sc[hd] = acc_sc[hd] + pv(k0 + u - 1, hd, p.astype(BF16))
            ss = ss_next
        new = []
        for hd in heads:
            p = jnp.exp2(ss[hd])
            new.append(carry[hd] + l_add[hd] + jnp.sum(p, axis=0, keepdims=True))
            p_sc[hd] = p.astype(BF16)
        return tuple(new)

    def step_online(ki, carry, masked):
        (s_sc,) = maybe_s_sc
        s_next, pvs = issue(jnp.minimum(ki + 1, nk - 1), jnp.maximum(ki - 1, 0))
        if masked:
            mask = mask_of(ki)
        new = []
        for hd in heads:
            m_old, l_old = carry[hd]
            s = s_sc[hd]
            if masked:
                s = jnp.where(mask, s, MASK_VALUE)
            m_new = jnp.maximum(m_old, jnp.max(s, axis=0, keepdims=True))
            alpha = jnp.exp2(m_old - m_new)
            p = jnp.exp2(s - m_new)
            l_new = alpha * l_old + jnp.sum(p, axis=0, keepdims=True)
            acc_sc[hd] = alpha * (acc_sc[hd] + pvs[hd])
            p_sc[hd] = p.astype(BF16)
            new.append((m_new, l_new))
        for hd in heads:
            s_sc[hd] = s_next[hd]
        return tuple(new)

    p_sc[...] = jnp.zeros(p_sc.shape, BF16)
    acc_sc[...] = jnp.zeros(acc_sc.shape, F32)
    if online:
        for hd in heads:
            maybe_s_sc[0][hd] = score(0, hd)
        init = tuple((jnp.full((1, tq), MASK_VALUE, F32), jnp.zeros((1, tq), F32))
                     for _ in heads)
        step = step_online
    else:
        init = tuple(jnp.zeros((1, tq), F32) for _ in heads)
        step = step_plain
    n_done = 0
    if not online:
        for n_tiles in ATT_UNROLLS:
            n_steps = (n_full - n_done) // n_tiles
            init = lax.fori_loop(
                0, n_steps, functools.partial(multi_plain, n_tiles=n_tiles, first=n_done), init)
            n_done = n_done + n_tiles * n_steps
    carry = lax.fori_loop(n_done, n_full, functools.partial(step, masked=False), init)
    carry = lax.fori_loop(n_full, n_kv, functools.partial(step, masked=True), carry)
    last = jnp.maximum(n_kv - 1, 0)
    ls = [c[1] for c in carry] if online else carry
    o_t = jnp.concatenate([(acc_sc[hd] + pv(last, hd)) / ls[hd] for hd in heads], axis=0)
    o_ref[0] = o_t.T.astype(BF16)


def _attention(q, k, vt, n_full, n_kv, posq, posk, online):
    B, H, _, S = q.shape
    tq = ATT_TQ
    nh = ATT_HEADS
    nk = S // ATT_TK
    scratch = [pltpu.VMEM((nh, ATT_TK, tq), BF16), pltpu.VMEM((nh, MLA_V, tq), F32)]
    if online:
        scratch.append(pltpu.VMEM((nh, ATT_TK, tq), F32))
    grid_spec = pltpu.PrefetchScalarGridSpec(
        num_scalar_prefetch=2,
        grid=(B, H // nh, S // tq),
        in_specs=[
            pl.BlockSpec((1, nh, HEAD_SLOT, tq), lambda b, hp, qi, *_: (b, hp, 0, qi)),
            pl.BlockSpec((1, nh, S, HEAD_SLOT), lambda b, hp, qi, *_: (b, hp, 0, 0)),
            pl.BlockSpec((1, nk, nh * MLA_V, ATT_TK), lambda b, hp, qi, *_: (b, 0, hp, 0)),
            pl.BlockSpec((1, 1, tq), lambda b, hp, qi, *_: (qi, 0, 0)),
            pl.BlockSpec((S, 1), lambda b, hp, qi, *_: (0, 0)),
        ],
        out_specs=pl.BlockSpec((1, tq, nh * MLA_V), lambda b, hp, qi, *_: (b, qi, hp)),
        scratch_shapes=scratch,
    )
    return pl.pallas_call(
        functools.partial(_attn_kernel, online=online),
        grid_spec=grid_spec,
        out_shape=jax.ShapeDtypeStruct((B, S, H * MLA_V), BF16),
        compiler_params=pltpu.CompilerParams(
            dimension_semantics=("parallel", "parallel", "arbitrary"),
            vmem_limit_bytes=VMEM_LIMIT),
        name="mla_attention_online" if online else "mla_attention",
    )(n_full, n_kv, q, k, vt, posq, posk)


def _hgrn_conv_kernel(hc_ref, lbl_ref, onorm_ref, convw_ref, y_ref,
                      st_sc, ubuf_sc, o_sc, *, layer):
    sb = pl.program_id(1)
    R = HG_ROWS
    C = HG_CHUNK
    W = HG_WIDTH

    @pl.when(sb == 0)
    def _():
        st_sc[...] = jnp.zeros(st_sc.shape, F32)
        ubuf_sc[0:8, :] = jnp.zeros((8, CONV_WIDTH), F32)

    lg = lbl_ref[...]
    e = jnp.exp(lg - jnp.max(lg, axis=0, keepdims=True))
    soft = e / jnp.sum(e, axis=0, keepdims=True)
    lb = jnp.zeros((1, W), F32)
    for i in range(1, layer + 1):
        lb = lb + soft[i:i + 1, :]
    lb = jnp.maximum(lb, 0.0)
    log_lb = jnp.log(lb)
    log_1m = jnp.log1p(-lb)

    ri = lax.broadcasted_iota(jnp.int32, (C, C), 0)
    ci = lax.broadcasted_iota(jnp.int32, (C, C), 1)
    sub_shift = HG_SUB.bit_length() - 1
    dk_shift = HG_DK.bit_length() - 1
    tri = jnp.where(ci <= ri, 1.0, 0.0)
    tri_blk = jnp.where((ci >> sub_shift) == (ri >> sub_shift), tri, 0.0)
    cum_mat = jnp.concatenate([tri, tri_blk], axis=0).astype(BF16)
    tt = ri & (HG_SUB - 1)
    lane_w = lax.broadcasted_iota(jnp.int32, (1, W), 1)
    head_masks = [jnp.where((lane_w >> dk_shift) == hh, 1.0, 0.0) for hh in range(HG_HEADS)]
    rw = lax.broadcasted_iota(jnp.int32, (W, W), 0)
    cw = lax.broadcasted_iota(jnp.int32, (W, W), 1)
    bd_mask = (rw >> dk_shift) == (cw >> dk_shift)
    n_sub = C // HG_SUB

    def group(gi, carry):
        base = gi * (HG_GROUP * C)
        starts = [pl.multiple_of(base + j * C, C) for j in range(HG_GROUP)]
        chunks = range(HG_GROUP)
        gates, splits = [], []
        for r0 in starts:
            zq = hc_ref[0, pl.ds(r0, C), 0:W]
            zf = hc_ref[0, pl.ds(r0, C), W:2 * W]
            vi = hc_ref[0, pl.ds(r0, C), 2 * W:3 * W]
            q = zq * _sigmoid(zq)
            t = jnp.exp(-jnp.abs(zf))
            d = 1.0 + t
            r = 1.0 / d
            log_sig = jnp.minimum(zf, 0.0) - jnp.log(d)
            c2 = log_1m + log_sig
            log_f = jnp.maximum(log_lb, c2) + jnp.log(1.0 + jnp.exp(-jnp.abs(log_lb - c2)))
            kk = (1.0 - lb) * jnp.where(zf >= 0.0, t * r, r)
            f_hi = log_f.astype(BF16)
            r1 = log_f - f_hi.astype(F32)
            f_mid = r1.astype(BF16)
            f_lo = (r1 - f_mid.astype(F32)).astype(BF16)
            gates.append((q, kk, vi.astype(BF16)))
            splits.append((f_hi, f_mid, f_lo))
        cums = [_dot(cum_mat, s[0]) + _dot(cum_mat, s[1]) + _dot(cum_mat, s[2]) for s in splits]
        qbs, b_lasts, lhss, kds, kdls = [], [], [], [], []
        for j in chunks:
            q, kk, _ = gates[j]
            b = cums[j][:C]
            g = cums[j][C:]
            b_last = b[C - 1:C, :]
            rref = b - g
            qd = q * jnp.exp(g)
            qbs.append((q * jnp.exp(b)).astype(BF16))
            b_lasts.append(b_last)
            kdls.append((kk * jnp.exp(b_last - b)).astype(BF16))
            lhs_j, kd_j = [], []
            for i in range(n_sub):
                rr = rref[i * HG_SUB:i * HG_SUB + 1, :]
                kd_j.append((kk * jnp.exp(jnp.minimum(rr - b, EXP_CLAMP))).astype(BF16))
                qi = qd[i * HG_SUB:(i + 1) * HG_SUB]
                lhs_j.append(jnp.concatenate([qi * hm for hm in head_masks],
                                             axis=0).astype(BF16))
            lhss.append(lhs_j)
            kds.append(kd_j)
        attn = [[_dot_nt(lhss[j][i], kds[j][i]) for i in range(n_sub)] for j in chunks]
        upds = [_dot_tn(gates[j][2], kdls[j]) for j in chunks]
        attn = [[jnp.where(ci <= i * HG_SUB + tt, attn[j][i], 0.0).astype(BF16)
                 for i in range(n_sub)] for j in chunks]
        pvs = [[_dot(attn[j][i], gates[j][2]) for i in range(n_sub)] for j in chunks]
        st = st_sc[...]
        for j in chunks:
            o_parts = []
            for i in range(n_sub):
                pv = pvs[j][i]
                oi = pv[0:HG_SUB] * head_masks[0]
                for hh in range(1, HG_HEADS):
                    oi = oi + pv[hh * HG_SUB:(hh + 1) * HG_SUB] * head_masks[hh]
                o_parts.append(oi)
            o_inter = _dot_nt(qbs[j], st.astype(BF16))
            o_sc[pl.ds(starts[j], C), :] = o_inter + jnp.concatenate(o_parts, axis=0)
            st = st * jnp.exp(b_lasts[j]) + jnp.where(bd_mask, upds[j], 0.0)
        st_sc[...] = st
        return carry

    lax.fori_loop(0, R // (C * HG_GROUP), group, 0)

    o = o_sc[...]
    o2 = o * o
    o2_hi = o2.astype(BF16)
    o2_lo = (o2 - o2_hi.astype(F32)).astype(BF16)
    ones_bd = jnp.where(bd_mask, 1.0, 0.0).astype(BF16)
    ms = (_dot(o2_hi, ones_bd) + _dot(o2_lo, ones_bd)) * (1.0 / HG_DK)
    zg = hc_ref[0, :, 3 * W:4 * W]
    y_hg = o * lax.rsqrt(ms + EPS) * onorm_ref[...] * (zg * _sigmoid(zg))
    y_ref[0, :, 0:W] = y_hg.astype(BF16)

    cb = hc_ref[0, :, 4 * W:4 * W + CONV_WIDTH]
    cc = hc_ref[0, :, 4 * W + CONV_WIDTH:4 * W + 2 * CONV_WIDTH]
    cx = hc_ref[0, :, 4 * W + 2 * CONV_WIDTH:4 * W + 3 * CONV_WIDTH]
    u = cc * cx
    ubuf_sc[8:8 + R, :] = u
    u1 = ubuf_sc[7:7 + R, :]
    u2 = ubuf_sc[6:6 + R, :]
    wc = convw_ref[...]
    y_cv = cb * (u2 * wc[0:1, :] + u1 * wc[1:2, :] + u * wc[2:3, :])
    y_ref[0, :, W:W + CONV_WIDTH] = y_cv.astype(BF16)
    ubuf_sc[0:8, :] = ubuf_sc[R:R + 8, :]


def _hgrn_conv(hc, lb_logits, onorm_g, conv_w, layer):
    B, S, _ = hc.shape
    R = HG_ROWS
    const = lambda b, s: (0, 0)
    return pl.pallas_call(
        functools.partial(_hgrn_conv_kernel, layer=layer),
        grid=(B, S // R),
        in_specs=[
            pl.BlockSpec((1, R, HC_WIDTH), lambda b, s: (b, s, 0)),
            pl.BlockSpec((DEPTH, HG_WIDTH), const),
            pl.BlockSpec((1, HG_WIDTH), const),
            pl.BlockSpec((CONV_K, CONV_WIDTH), const),
        ],
        out_specs=pl.BlockSpec((1, R, HG_WIDTH + CONV_WIDTH), lambda b, s: (b, s, 0)),
        out_shape=jax.ShapeDtypeStruct((B, S, HG_WIDTH + CONV_WIDTH), BF16),
        scratch_shapes=[
            pltpu.VMEM((HG_WIDTH, HG_WIDTH), F32),
            pltpu.VMEM((R + 8, CONV_WIDTH), F32),
            pltpu.VMEM((R, HG_WIDTH), F32),
        ],
        compiler_params=pltpu.CompilerParams(
            dimension_semantics=("parallel", "arbitrary"), vmem_limit_bytes=VMEM_LIMIT),
        name="hgrn_conv",
    )(hc, lb_logits, onorm_g, conv_w)


def _mem_kv_kernel(mem_ref, g_ref, w_ref, gk_ref, k_out, v_out):
    m = mem_ref[0]
    mn = m * _rms(m) * g_ref[0]
    kv = _dot(mn.astype(BF16), w_ref[0])
    hw = X_HEADS * X_HEAD_DIM
    gk = gk_ref[0]
    for hd in range(X_HEADS):
        sl = slice(hd * X_HEAD_DIM, (hd + 1) * X_HEAD_DIM)
        kh = kv[:, sl]
        k_out[0, 0, :, sl] = (kh * _rms(kh) * gk).astype(BF16)
    v_out[0, 0] = kv[:, hw:].astype(BF16)


def _mem_kv(mem, mem_norm_g, w_xkv, xk_norm_g):
    B, M, _ = mem.shape
    L = w_xkv.shape[0]
    hw = X_HEADS * X_HEAD_DIM
    out_spec = pl.BlockSpec((1, 1, M, hw), lambda l, b: (l, b, 0, 0))
    return pl.pallas_call(
        _mem_kv_kernel,
        grid=(L, B),
        in_specs=[
            pl.BlockSpec((1, M, D_MODEL), lambda l, b: (b, 0, 0)),
            pl.BlockSpec((1, 1, D_MODEL), lambda l, b: (l, 0, 0)),
            pl.BlockSpec((1, D_MODEL, 2 * hw), lambda l, b: (l, 0, 0)),
            pl.BlockSpec((1, 1, X_HEAD_DIM), lambda l, b: (l, 0, 0)),
        ],
        out_specs=[out_spec, out_spec],
        out_shape=[jax.ShapeDtypeStruct((L, B, M, hw), BF16)] * 2,
        compiler_params=pltpu.CompilerParams(
            dimension_semantics=("parallel", "parallel"), vmem_limit_bytes=VMEM_LIMIT),
        name="mem_kv",
    )(mem, mem_norm_g, w_xkv, xk_norm_g)


def _out_xattn_kernel(x_ref, ya_ref, yb_ref, wo_ref, g_ref, wq_ref, gq_ref,
                      k_ref, v_ref, wxo_ref, o_ref):
    half = wo_ref.shape[0] // 2
    x1 = x_ref[0] + _dot(ya_ref[0], wo_ref[:half, :]) + _dot(yb_ref[0], wo_ref[half:, :])
    h = x1 * _rms(x1) * g_ref[...]
    q = _dot(h.astype(BF16), wq_ref[...])
    gq = gq_ref[...] * (1.0 / math.sqrt(X_HEAD_DIM))
    outs = []
    for hd in range(X_HEADS):
        sl = slice(hd * X_HEAD_DIM, (hd + 1) * X_HEAD_DIM)
        qh = q[:, sl]
        qh = (qh * _rms(qh) * gq).astype(BF16)
        s = _dot_nt(qh, k_ref[0, 0, :, sl])
        p = jnp.exp(s - jnp.max(s, axis=1, keepdims=True))
        l = jnp.sum(p, axis=1, keepdims=True)
        outs.append(_dot(p.astype(BF16), v_ref[0, 0, :, sl]) / l)
    o = jnp.concatenate(outs, axis=1).astype(BF16)
    o_ref[0] = x1 + _dot(o, wxo_ref[...])


def _out_xattn(x, y_mla, y_hc, w_out, g, w_xq, gq, k_mem, v_mem, w_xo, layer):
    B, S, _ = x.shape
    tm = ROW_TILE
    hw = X_HEADS * X_HEAD_DIM
    const = lambda b, s: (0, 0)
    row = lambda w: pl.BlockSpec((1, tm, w), lambda b, s: (b, s, 0))
    mem_spec = pl.BlockSpec((1, 1, MEM_LEN, hw), lambda b, s: (layer, b, 0, 0))
    return pl.pallas_call(
        _out_xattn_kernel,
        grid=(B, S // tm),
        in_specs=[
            row(D_MODEL), row(y_mla.shape[-1]), row(y_hc.shape[-1]),
            pl.BlockSpec((D_MODEL, D_MODEL), const),
            pl.BlockSpec((1, D_MODEL), const),
            pl.BlockSpec((D_MODEL, hw), const),
            pl.BlockSpec((1, X_HEAD_DIM), const),
            mem_spec, mem_spec,
            pl.BlockSpec((hw, D_MODEL), const),
        ],
        out_specs=row(D_MODEL),
        out_shape=jax.ShapeDtypeStruct((B, S, D_MODEL), F32),
        compiler_params=pltpu.CompilerParams(
            dimension_semantics=("parallel", "parallel"), vmem_limit_bytes=VMEM_LIMIT),
        name="out_xattn",
    )(x, y_mla, y_hc, w_out, g, w_xq, gq, k_mem, v_mem, w_xo)


def _mlp_kernel(x_ref, g_ref, wu_ref, wd_ref, o_ref):
    x = x_ref[0]
    h = (x * _rms(x) * g_ref[...]).astype(BF16)
    acc = x
    step = D_MODEL
    for c in range(D_FF // step):
        u = _dot(h, wu_ref[:, c * step:(c + 1) * step])
        a = jnp.square(jnp.maximum(u, 0.0)).astype(BF16)
        acc = acc + _dot(a, wd_ref[c * step:(c + 1) * step, :])
    o_ref[0] = acc


def _mlp(x, g, w_up, w_down):
    B, S, _ = x.shape
    tm = ROW_TILE
    const = lambda b, s: (0, 0)
    row = pl.BlockSpec((1, tm, D_MODEL), lambda b, s: (b, s, 0))
    return pl.pallas_call(
        _mlp_kernel,
        grid=(B, S // tm),
        in_specs=[row, pl.BlockSpec((1, D_MODEL), const),
                  pl.BlockSpec((D_MODEL, D_FF), const),
                  pl.BlockSpec((D_FF, D_MODEL), const)],
        out_specs=row,
        out_shape=jax.ShapeDtypeStruct((B, S, D_MODEL), F32),
        compiler_params=pltpu.CompilerParams(
            dimension_semantics=("parallel", "parallel"), vmem_limit_bytes=VMEM_LIMIT),
        name="mlp",
    )(x, g, w_up, w_down)


def _scatter_lanes(w_cols):
    idx = jnp.asarray(np.where(_LANE_MAP >= 0, _LANE_MAP, 0), jnp.int32)
    valid = jnp.asarray(_LANE_MAP >= 0)
    return jnp.where(valid, jnp.take(w_cols, idx, axis=-1), 0.0)


def _prep_layer(l, positions, w_in, w_uq, w_ukv, mla_qn_g, mla_kn_g):
    o_kr = MLA_Q_RANK + MLA_KV_RANK
    w = w_in[l]
    kr_src = jnp.concatenate(
        [jnp.zeros((D_MODEL, MLA_NOPE), F32), w[:, o_kr:o_kr + MLA_ROPE]], axis=1)
    w_in_p = jnp.concatenate(
        [w[:, :o_kr], _scatter_lanes(kr_src), w[:, o_kr + MLA_ROPE:]], axis=1).astype(BF16)

    def with_swapped_halves(w_slots):
        rank = w_slots.shape[0]
        both = jnp.concatenate([w_slots, jnp.roll(w_slots, HEAD_SLOT // 2, axis=-1)], axis=1)
        return both.reshape(rank, 2 * MLA_HEADS * HEAD_SLOT).astype(BF16)

    wq = w_uq[l].reshape(MLA_Q_RANK, MLA_HEADS, MLA_QK)
    w_uq_p = with_swapped_halves(_scatter_lanes(wq))

    wkv = w_ukv[l].reshape(MLA_KV_RANK, MLA_HEADS, MLA_NOPE + MLA_V)
    k_src = jnp.concatenate(
        [wkv[..., :MLA_NOPE], jnp.zeros((MLA_KV_RANK, MLA_HEADS, MLA_ROPE), F32)], axis=-1)
    w_uk_p = with_swapped_halves(_scatter_lanes(k_src))
    w_vt = wkv[..., MLA_NOPE:].reshape(MLA_KV_RANK, MLA_HEADS * MLA_V).T.astype(BF16)

    inv_freq = ROPE_BASE ** (-jnp.arange(0, MLA_ROPE, 2, dtype=F32) / MLA_ROPE)
    ang = positions.astype(F32)[:, None] * inv_freq[None, :]
    cos, sin = jnp.cos(ang), jnp.sin(ang)
    S = positions.shape[0]
    pad = HEAD_SLOT // 2 - ROPE_HALF
    c_tab = jnp.concatenate([cos, jnp.ones((S, pad), F32), cos, jnp.ones((S, pad), F32)], axis=1)
    s_tab = jnp.concatenate([-sin, jnp.zeros((S, pad), F32), sin, jnp.zeros((S, pad), F32)], axis=1)

    def tables(gain, scale):
        g = _scatter_lanes(gain)[None, :] * scale
        return c_tab * g, s_tab * jnp.roll(g, HEAD_SLOT // 2, axis=1)

    cq_t, sq_t = tables(mla_qn_g[l], math.log2(math.e) / math.sqrt(MLA_QK))
    ck_t, sk_t = tables(mla_kn_g[l], 1.0)
    return w_in_p, w_uq_p, w_uk_p, w_vt, (cq_t, sq_t, ck_t, sk_t)


def kernel(x, mem, positions, mix_norm_g, w_in, mla_q_norm_g, mla_kv_norm_g, w_uq, w_ukv,
           mla_qn_g, mla_kn_g, hgrn_lb_logits, hgrn_o_norm_g, conv_w, w_out,
           xattn_norm_g, mem_norm_g, w_xq, w_xkv, xq_norm_g, xk_norm_g, w_xo,
           mlp_norm_g, w_up, w_down):
    B, S, _ = x.shape
    L = w_in.shape[0]
    nq, nk = S // ATT_TQ, S // ATT_TK
    pq = positions.reshape(nq, ATT_TQ)
    pk = positions.reshape(nk, ATT_TK)
    vis = jnp.min(pk, axis=1)[None, :] <= jnp.max(pq, axis=1)[:, None]
    n_kv = jnp.max(jnp.where(vis, jnp.arange(1, nk + 1, dtype=jnp.int32)[None, :], 0),
                   axis=1).astype(jnp.int32)
    full = jnp.max(pk, axis=1)[None, :] <= jnp.min(pq, axis=1)[:, None]
    n_full = jnp.sum(jnp.cumprod(full.astype(jnp.int32), axis=1), axis=1).astype(jnp.int32)
    posq = positions.reshape(nq, 1, ATT_TQ)
    posk = positions.reshape(S, 1)

    k_mem, v_mem = _mem_kv(mem, mem_norm_g.reshape(L, 1, D_MODEL), w_xkv.astype(BF16),
                           xk_norm_g.reshape(L, 1, X_HEAD_DIM))
    for l in range(L):
        w_in_p, w_uq_p, w_uk_p, w_vt, tabs = _prep_layer(l, positions, w_in, w_uq, w_ukv,
                                                         mla_qn_g, mla_kn_g)
        lat, hc = _in_proj(x, mix_norm_g[l][None], w_in_p)
        q, k, vt = _mla_qkv(lat, mla_q_norm_g[l][None], w_uq_p, mla_kv_norm_g[l][None],
                            w_uk_p, w_vt, tabs)
        score_bound = (math.log2(math.e) * math.sqrt(MLA_QK)
                       * jnp.max(jnp.abs(mla_qn_g[l])) * jnp.max(jnp.abs(mla_kn_g[l])))
        y_mla = lax.cond(score_bound <= ATT_PLAIN_MAX_SCORE,
                         functools.partial(_attention, online=False),
                         functools.partial(_attention, online=True),
                         q, k, vt, n_full, n_kv, posq, posk)
        y_hc = _hgrn_conv(hc, hgrn_lb_logits, hgrn_o_norm_g[l][None], conv_w[l], l)
        x = _out_xattn(x, y_mla, y_hc, w_out[l].astype(BF16), xattn_norm_g[l][None],
                       w_xq[l].astype(BF16), xq_norm_g[l][None], k_mem, v_mem,
                       w_xo[l].astype(BF16), l)
        x = _mlp(x, mlp_norm_g[l][None], w_up[l].astype(BF16), w_down[l].astype(BF16))
    return x
```

```python
import functools
import math

import numpy as np
import jax
import jax.numpy as jnp
from jax import lax
from jax.experimental import pallas as pl
from jax.experimental.pallas import tpu as pltpu

F32 = jnp.float32
BF16 = jnp.bfloat16

D_MODEL = 1024
DEPTH = 4
MEM_LEN = 256
EPS = 1e-6

MLA_HEADS = 8
MLA_NOPE = 64
MLA_ROPE = 32
MLA_V = 64
MLA_QK = MLA_NOPE + MLA_ROPE
MLA_Q_RANK = 384
MLA_KV_RANK = 256
ROPE_BASE = 10000.0

HG_HEADS = 4
HG_DK = 64
HG_CHUNK = 64
HG_SUB = 16
HG_GROUP = 8
HG_WIDTH = 256
CONV_WIDTH = 256
CONV_K = 3

X_HEADS = 4
X_HEAD_DIM = 128
D_FF = 4 * D_MODEL

LANES = 128
HEAD_SLOT = LANES
ROPE_HALF = MLA_ROPE // 2
N_IN_PAD = MLA_Q_RANK + MLA_KV_RANK + HEAD_SLOT + 4 * HG_WIDTH + 3 * CONV_WIDTH
HC_OFF = MLA_Q_RANK + MLA_KV_RANK + HEAD_SLOT
HC_WIDTH = N_IN_PAD - HC_OFF

ROW_TILE = 512
ATT_TQ = 256
ATT_TK = 256
ATT_HEADS = 8
ATT_CHAINS = 4
ATT_UNROLLS = (4, 2)
HG_ROWS = 512
MASK_VALUE = -1e30
ATT_PLAIN_MAX_SCORE = 64.0
EXP_CLAMP = 60.0
VMEM_LIMIT = 56 * 1024 * 1024


def _head_lane_map():
    m = -np.ones((HEAD_SLOT,), np.int64)
    m[0:ROPE_HALF] = MLA_NOPE + np.arange(ROPE_HALF)
    m[ROPE_HALF:64] = np.arange(64 - ROPE_HALF)
    m[64:64 + ROPE_HALF] = MLA_NOPE + ROPE_HALF + np.arange(ROPE_HALF)
    m[64 + ROPE_HALF:96] = (64 - ROPE_HALF) + np.arange(ROPE_HALF)
    return m


_LANE_MAP = _head_lane_map()


def _rms(x, n=None):
    n = x.shape[-1] if n is None else n
    return lax.rsqrt(jnp.sum(x * x, axis=-1, keepdims=True) * (1.0 / n) + EPS)


def _sigmoid(x):
    return 1.0 / (1.0 + jnp.exp(-x))


def _dot(a, b):
    return jnp.dot(a, b, preferred_element_type=F32)


def _dot_nt(a, b):
    return lax.dot_general(a, b, (((1,), (1,)), ((), ())), preferred_element_type=F32)


def _dot_tn(a, b):
    return lax.dot_general(a, b, (((0,), (0,)), ((), ())), preferred_element_type=F32)


def _in_proj_kernel(x_ref, g_ref, w_in_ref, lat_out, hc_out):
    x = x_ref[0]
    h = (x * _rms(x) * g_ref[...]).astype(BF16)
    lat_out[0] = _dot(h, w_in_ref[:, :HC_OFF])
    hc_out[0] = _dot(h, w_in_ref[:, HC_OFF:])


def _in_proj(x, g, w_in):
    B, S, _ = x.shape
    tm = ROW_TILE
    const = lambda b, s: (0, 0)
    return pl.pallas_call(
        _in_proj_kernel,
        grid=(B, S // tm),
        in_specs=[
            pl.BlockSpec((1, tm, D_MODEL), lambda b, s: (b, s, 0)),
            pl.BlockSpec((1, D_MODEL), const),
            pl.BlockSpec((D_MODEL, N_IN_PAD), const),
        ],
        out_specs=[
            pl.BlockSpec((1, tm, HC_OFF), lambda b, s: (b, s, 0)),
            pl.BlockSpec((1, tm, HC_WIDTH), lambda b, s: (b, s, 0)),
        ],
        out_shape=[
            jax.ShapeDtypeStruct((B, S, HC_OFF), F32),
            jax.ShapeDtypeStruct((B, S, HC_WIDTH), F32),
        ],
        compiler_params=pltpu.CompilerParams(
            dimension_semantics=("parallel", "parallel"), vmem_limit_bytes=VMEM_LIMIT),
        name="in_proj",
    )(x, g, w_in)


def _mla_qkv_kernel(lat_ref, gq_ref, w_uq_ref, gkv_ref, w_uk_ref, w_vt_ref,
                    cq_tab, sq_tab, ck_tab, sk_tab, q_out, k_out, vt_out):
    cq = lat_ref[0, :, :MLA_Q_RANK]
    ckv = lat_ref[0, :, MLA_Q_RANK:MLA_Q_RANK + MLA_KV_RANK]
    kr = lat_ref[0, :, MLA_Q_RANK + MLA_KV_RANK:]
    cqn = (cq * _rms(cq) * gq_ref[...]).astype(BF16)
    ckvn = (ckv * _rms(ckv) * gkv_ref[...]).astype(BF16)
    hw = MLA_HEADS * HEAD_SLOT
    qf = _dot(cqn, w_uq_ref[...])
    kf = _dot(ckvn, w_uk_ref[...])
    vt = _dot_nt(w_vt_ref[...], ckvn).astype(BF16)
    for j in range(vt_out.shape[1]):
        vt_out[0, j] = vt[:, j * ATT_TK:(j + 1) * ATT_TK]
    kr_sw = pltpu.roll(kr, HEAD_SLOT // 2, 1)
    cq_t, sq_t, ck_t, sk_t = cq_tab[...], sq_tab[...], ck_tab[...], sk_tab[...]
    for hd in range(MLA_HEADS):
        sl = slice(hd * HEAD_SLOT, (hd + 1) * HEAD_SLOT)
        sw = slice(hw + hd * HEAD_SLOT, hw + (hd + 1) * HEAD_SLOT)
        qh = qf[:, sl]
        qh = (qh * cq_t + qf[:, sw] * sq_t) * _rms(qh, MLA_QK)
        q_out[0, hd] = qh.T.astype(BF16)
        kh = kf[:, sl] + kr
        kh = (kh * ck_t + (kf[:, sw] + kr_sw) * sk_t) * _rms(kh, MLA_QK)
        k_out[0, hd] = kh.astype(BF16)


def _mla_qkv(lat, gq, w_uq, gkv, w_uk, w_vt, tabs):
    B, S, _ = lat.shape
    tm = ROW_TILE
    vw = MLA_HEADS * MLA_V
    const = lambda b, s: (0, 0)
    tab_spec = pl.BlockSpec((tm, HEAD_SLOT), lambda b, s: (s, 0))
    return pl.pallas_call(
        _mla_qkv_kernel,
        grid=(B, S // tm),
        in_specs=[
            pl.BlockSpec((1, tm, HC_OFF), lambda b, s: (b, s, 0)),
            pl.BlockSpec((1, MLA_Q_RANK), const),
            pl.BlockSpec((MLA_Q_RANK, 2 * MLA_HEADS * HEAD_SLOT), const),
            pl.BlockSpec((1, MLA_KV_RANK), const),
            pl.BlockSpec((MLA_KV_RANK, 2 * MLA_HEADS * HEAD_SLOT), const),
            pl.BlockSpec((vw, MLA_KV_RANK), const),
            tab_spec, tab_spec, tab_spec, tab_spec,
        ],
        out_specs=[
            pl.BlockSpec((1, MLA_HEADS, HEAD_SLOT, tm), lambda b, s: (b, 0, 0, s)),
            pl.BlockSpec((1, MLA_HEADS, tm, HEAD_SLOT), lambda b, s: (b, 0, s, 0)),
            pl.BlockSpec((1, tm // ATT_TK, vw, ATT_TK), lambda b, s: (b, s, 0, 0)),
        ],
        out_shape=[
            jax.ShapeDtypeStruct((B, MLA_HEADS, HEAD_SLOT, S), BF16),
            jax.ShapeDtypeStruct((B, MLA_HEADS, S, HEAD_SLOT), BF16),
            jax.ShapeDtypeStruct((B, S // ATT_TK, vw, ATT_TK), BF16),
        ],
        compiler_params=pltpu.CompilerParams(
            dimension_semantics=("parallel", "parallel"), vmem_limit_bytes=VMEM_LIMIT),
        name="mla_qkv",
    )(lat, gq, w_uq, gkv, w_uk, w_vt, *tabs)


def _attn_kernel(nfull_ref, nkv_ref, q_ref, k_ref, vt_ref, posq_ref, posk_ref, o_ref,
                 p_sc, acc_sc, *maybe_s_sc, online):
    qi = pl.program_id(2)
    n_full = nfull_ref[qi]
    n_kv = nkv_ref[qi]
    tq = q_ref.shape[3]
    nk = vt_ref.shape[1]
    heads = range(ATT_HEADS)
    qts = [q_ref[0, hd] for hd in heads]
    posq = posq_ref[0]

    groups = [list(heads)[i:i + ATT_CHAINS] for i in range(0, ATT_HEADS, ATT_CHAINS)]

    def score(ki, hd):
        off = pl.multiple_of(ki * ATT_TK, ATT_TK)
        return _dot(k_ref[0, hd, pl.ds(off, ATT_TK), :], qts[hd])

    def pv(ki, hd, p=None):
        p = p_sc[hd] if p is None else p
        return _dot(vt_ref[0, ki, hd * MLA_V:(hd + 1) * MLA_V, :], p)

    def issue(ki_scores, ki_pv):
        ss, pvs = {}, {}
        for g in groups:
            for hd in g:
                ss[hd] = score(ki_scores, hd)
            for hd in g:
                pvs[hd] = pv(ki_pv, hd)
        return ss, pvs

    def mask_of(ki):
        off = pl.multiple_of(ki * ATT_TK, ATT_TK)
        return posk_ref[pl.ds(off, ATT_TK), :] <= posq

    def step_plain(ki, carry, masked):
        ss, pvs = issue(ki, jnp.maximum(ki - 1, 0))
        if masked:
            mask = mask_of(ki)
        new = []
        for hd in heads:
            s = jnp.where(mask, ss[hd], MASK_VALUE) if masked else ss[hd]
            p = jnp.exp2(s)
            new.append(carry[hd] + jnp.sum(p, axis=0, keepdims=True))
            acc_sc[hd] = acc_sc[hd] + pvs[hd]
            p_sc[hd] = p.astype(BF16)
        return tuple(new)

    def multi_plain(j, carry, n_tiles, first):
        k0 = first + n_tiles * j
        ss, pvs = issue(k0, jnp.maximum(k0 - 1, 0))
        l_add = {hd: 0.0 for hd in heads}
        for hd in heads:
            acc_sc[hd] = acc_sc[hd] + pvs[hd]
        for u in range(1, n_tiles):
            ss_next = {}
            for g in groups:
                for hd in g:
                    ss_next[hd] = score(k0 + u, hd)
                for hd in g:
                    p = jnp.exp2(ss[hd])
                    l_add[hd] = l_add[hd] + jnp.sum(p, axis=0, keepdims=True)
                    acc_sc[hd] = acc_sc[hd] + pv(k0 + u - 1, hd, p.astype(BF16))
            ss = ss_next
        new = []
        for hd in heads:
            p = jnp.exp2(ss[hd])
            new.append(carry[hd] + l_add[hd] + jnp.sum(p, axis=0, keepdims=True))
            p_sc[hd] = p.astype(BF16)
        return tuple(new)

    def step_online(ki, carry, masked):
        (s_sc,) = maybe_s_sc
        s_next, pvs = issue(jnp.minimum(ki + 1, nk - 1), jnp.maximum(ki - 1, 0))
        if masked:
            mask = mask_of(ki)
        new = []
        for hd in heads:
            m_old, l_old = carry[hd]
            s = s_sc[hd]
            if masked:
                s = jnp.where(mask, s, MASK_VALUE)
            m_new = jnp.maximum(m_old, jnp.max(s, axis=0, keepdims=True))
            alpha = jnp.exp2(m_old - m_new)
            p = jnp.exp2(s - m_new)
            l_new = alpha * l_old + jnp.sum(p, axis=0, keepdims=True)
            acc_sc[hd] = alpha * (acc_sc[hd] + pvs[hd])
            p_sc[hd] = p.astype(BF16)
            new.append((m_new, l_new))
        for hd in heads:
            s_sc[hd] = s_next[hd]
        return tuple(new)

    p_sc[...] = jnp.zeros(p_sc.shape, BF16)
    acc_sc[...] = jnp.zeros(acc_sc.shape, F32)
    if online:
        for hd in heads:
            maybe_s_sc[0][hd] = score(0, hd)
        init = tuple((jnp.full((1, tq), MASK_VALUE, F32), jnp.zeros((1, tq), F32))
                     for _ in heads)
        step = step_online
    else:
        init = tuple(jnp.zeros((1, tq), F32) for _ in heads)
        step = step_plain
    n_done = 0
    if not online:
        for n_tiles in ATT_UNROLLS:
            n_steps = (n_full - n_done) // n_tiles
            init = lax.fori_loop(
                0, n_steps, functools.partial(multi_plain, n_tiles=n_tiles, first=n_done), init)
            n_done = n_done + n_tiles * n_steps
    carry = lax.fori_loop(n_done, n_full, functools.partial(step, masked=False), init)
    carry = lax.fori_loop(n_full, n_kv, functools.partial(step, masked=True), carry)
    last = jnp.maximum(n_kv - 1, 0)
    ls = [c[1] for c in carry] if online else carry
    o_t = jnp.concatenate([(acc_sc[hd] + pv(last, hd)) / ls[hd] for hd in heads], axis=0)
    o_ref[0] = o_t.T.astype(BF16)


def _attention(q, k, vt, n_full, n_kv, posq, posk, online):
    B, H, _, S = q.shape
    tq = ATT_TQ
    nh = ATT_HEADS
    nk = S // ATT_TK
    scratch = [pltpu.VMEM((nh, ATT_TK, tq), BF16), pltpu.VMEM((nh, MLA_V, tq), F32)]
    if online:
        scratch.append(pltpu.VMEM((nh, ATT_TK, tq), F32))
    grid_spec = pltpu.PrefetchScalarGridSpec(
        num_scalar_prefetch=2,
        grid=(B, H // nh, S // tq),
        in_specs=[
            pl.BlockSpec((1, nh, HEAD_SLOT, tq), lambda b, hp, qi, *_: (b, hp, 0, qi)),
            pl.BlockSpec((1, nh, S, HEAD_SLOT), lambda b, hp, qi, *_: (b, hp, 0, 0)),
            pl.BlockSpec((1, nk, nh * MLA_V, ATT_TK), lambda b, hp, qi, *_: (b, 0, hp, 0)),
            pl.BlockSpec((1, 1, tq), lambda b, hp, qi, *_: (qi, 0, 0)),
            pl.BlockSpec((S, 1), lambda b, hp, qi, *_: (0, 0)),
        ],
        out_specs=pl.BlockSpec((1, tq, nh * MLA_V), lambda b, hp, qi, *_: (b, qi, hp)),
        scratch_shapes=scratch,
    )
    return pl.pallas_call(
        functools.partial(_attn_kernel, online=online),
        grid_spec=grid_spec,
        out_shape=jax.ShapeDtypeStruct((B, S, H * MLA_V), BF16),
        compiler_params=pltpu.CompilerParams(
            dimension_semantics=("parallel", "parallel", "arbitrary"),
            vmem_limit_bytes=VMEM_LIMIT),
        name="mla_attention_online" if online else "mla_attention",
    )(n_full, n_kv, q, k, vt, posq, posk)


def _hgrn_conv_kernel(hc_ref, lbl_ref, onorm_ref, convw_ref, y_ref,
                      st_sc, ubuf_sc, o_sc, *, layer):
    sb = pl.program_id(1)
    R = HG_ROWS
    C = HG_CHUNK
    W = HG_WIDTH

    @pl.when(sb == 0)
    def _():
        st_sc[...] = jnp.zeros(st_sc.shape, F32)
        ubuf_sc[0:8, :] = jnp.zeros((8, CONV_WIDTH), F32)

    lg = lbl_ref[...]
    e = jnp.exp(lg - jnp.max(lg, axis=0, keepdims=True))
    soft = e / jnp.sum(e, axis=0, keepdims=True)
    lb = jnp.zeros((1, W), F32)
    for i in range(1, layer + 1):
        lb = lb + soft[i:i + 1, :]
    lb = jnp.maximum(lb, 0.0)
    log_lb = jnp.log(lb)
    log_1m = jnp.log1p(-lb)

    ri = lax.broadcasted_iota(jnp.int32, (C, C), 0)
    ci = lax.broadcasted_iota(jnp.int32, (C, C), 1)
    sub_shift = HG_SUB.bit_length() - 1
    dk_shift = HG_DK.bit_length() - 1
    tri = jnp.where(ci <= ri, 1.0, 0.0)
    tri_blk = jnp.where((ci >> sub_shift) == (ri >> sub_shift), tri, 0.0)
    cum_mat = jnp.concatenate([tri, tri_blk], axis=0).astype(BF16)
    tt = ri & (HG_SUB - 1)
    lane_w = lax.broadcasted_iota(jnp.int32, (1, W), 1)
    head_masks = [jnp.where((lane_w >> dk_shift) == hh, 1.0, 0.0) for hh in range(HG_HEADS)]
    rw = lax.broadcasted_iota(jnp.int32, (W, W), 0)
    cw = lax.broadcasted_iota(jnp.int32, (W, W), 1)
    bd_mask = (rw >> dk_shift) == (cw >> dk_shift)
    ones_bd = jnp.where(bd_mask, 1.0, 0.0).astype(BF16)
    n_sub = C // HG_SUB

    def group(gi, carry):
        base = gi * (HG_GROUP * C)
        starts = [pl.multiple_of(base + j * C, C) for j in range(HG_GROUP)]
        chunks = range(HG_GROUP)
        gates, splits = [], []
        for r0 in starts:
            zq = hc_ref[0, pl.ds(r0, C), 0:W]
            zf = hc_ref[0, pl.ds(r0, C), W:2 * W]
            vi = hc_ref[0, pl.ds(r0, C), 2 * W:3 * W]
            q = zq * _sigmoid(zq)
            t = jnp.exp(-jnp.abs(zf))
            d = 1.0 + t
            r = 1.0 / d
            log_sig = jnp.minimum(zf, 0.0) - jnp.log(d)
            c2 = log_1m + log_sig
            log_f = jnp.maximum(log_lb, c2) + jnp.log(1.0 + jnp.exp(-jnp.abs(log_lb - c2)))
            kk = (1.0 - lb) * jnp.where(zf >= 0.0, t * r, r)
            f_hi = log_f.astype(BF16)
            r1 = log_f - f_hi.astype(F32)
            f_mid = r1.astype(BF16)
            f_lo = (r1 - f_mid.astype(F32)).astype(BF16)
            gates.append((q, kk, vi.astype(BF16)))
            splits.append((f_hi, f_mid, f_lo))
        cums = [_dot(cum_mat, s[0]) + _dot(cum_mat, s[1]) + _dot(cum_mat, s[2]) for s in splits]

        def diag_direct(j, i):
            q, kk, vb = gates[j]
            rows = slice(i * HG_SUB, (i + 1) * HG_SUB)
            b_t, q_t, k_t = cums[j][:C][rows], q[rows], kk[rows]
            v_t = vb[rows].astype(F32)
            t_idx = lax.broadcasted_iota(jnp.int32, (HG_SUB, W), 0)
            slabs = []
            for s in range(HG_SUB):
                w = jnp.exp(jnp.minimum(b_t - b_t[s:s + 1, :], 0.0))
                slabs.append(jnp.where(t_idx >= s, q_t * k_t[s:s + 1, :] * w, 0.0))
            x = jnp.concatenate(slabs, axis=0).astype(BF16)
            a = _dot(x, ones_bd)
            o = a[0:HG_SUB] * v_t[0:1, :]
            for s in range(1, HG_SUB):
                o = o + a[s * HG_SUB:(s + 1) * HG_SUB] * v_t[s:s + 1, :]
            return o

        def finish(direct_diag):
            qbs, b_lasts, lhss, kds, kdls = [], [], [], [], []
            for j in chunks:
                q, kk, _ = gates[j]
                b = cums[j][:C]
                g = cums[j][C:]
                b_last = b[C - 1:C, :]
                rref = b - g
                qd = q * jnp.exp(g)
                qbs.append((q * jnp.exp(b)).astype(BF16))
                b_lasts.append(b_last)
                kdls.append((kk * jnp.exp(b_last - b)).astype(BF16))
                lhs_j, kd_j = [], []
                for i in range(n_sub):
                    rr = rref[i * HG_SUB:i * HG_SUB + 1, :]
                    kd_j.append((kk * jnp.exp(jnp.minimum(rr - b, EXP_CLAMP))).astype(BF16))
                    qi = qd[i * HG_SUB:(i + 1) * HG_SUB]
                    lhs_j.append(jnp.concatenate([qi * hm for hm in head_masks],
                                                 axis=0).astype(BF16))
                lhss.append(lhs_j)
                kds.append(kd_j)
            attn = [[_dot_nt(lhss[j][i], kds[j][i]) for i in range(n_sub)] for j in chunks]
            upds = [_dot_tn(gates[j][2], kdls[j]) for j in chunks]
            if direct_diag:
                attn = [[jnp.where(ci < i * HG_SUB, attn[j][i], 0.0).astype(BF16)
                         for i in range(n_sub)] for j in chunks]
            else:
                attn = [[jnp.where(ci <= i * HG_SUB + tt, attn[j][i], 0.0).astype(BF16)
                         for i in range(n_sub)] for j in chunks]
            pvs = [[_dot(attn[j][i], gates[j][2]) for i in range(n_sub)] for j in chunks]
            st = st_sc[...]
            for j in chunks:
                o_parts = []
                for i in range(n_sub):
                    pv = pvs[j][i]
                    oi = pv[0:HG_SUB] * head_masks[0]
                    for hh in range(1, HG_HEADS):
                        oi = oi + pv[hh * HG_SUB:(hh + 1) * HG_SUB] * head_masks[hh]
                    if direct_diag:
                        oi = oi + diag_direct(j, i)
                    o_parts.append(oi)
                o_inter = _dot_nt(qbs[j], st.astype(BF16))
                o_sc[pl.ds(starts[j], C), :] = o_inter + jnp.concatenate(o_parts, axis=0)
                st = st * jnp.exp(b_lasts[j]) + jnp.where(bd_mask, upds[j], 0.0)
            st_sc[...] = st
            return jnp.int32(0)

        g_min = cums[0][C:]
        for j in range(1, HG_GROUP):
            g_min = jnp.minimum(g_min, cums[j][C:])
        lax.cond(jnp.min(g_min) < -EXP_CLAMP,
                 functools.partial(finish, True), functools.partial(finish, False))
        return carry

    lax.fori_loop(0, R // (C * HG_GROUP), group, 0)

    o = o_sc[...]
    o2 = o * o
    o2_hi = o2.astype(BF16)
    o2_lo = (o2 - o2_hi.astype(F32)).astype(BF16)
    ms = (_dot(o2_hi, ones_bd) + _dot(o2_lo, ones_bd)) * (1.0 / HG_DK)
    zg = hc_ref[0, :, 3 * W:4 * W]
    y_hg = o * lax.rsqrt(ms + EPS) * onorm_ref[...] * (zg * _sigmoid(zg))
    y_ref[0, :, 0:W] = y_hg.astype(BF16)

    cb = hc_ref[0, :, 4 * W:4 * W + CONV_WIDTH]
    cc = hc_ref[0, :, 4 * W + CONV_WIDTH:4 * W + 2 * CONV_WIDTH]
    cx = hc_ref[0, :, 4 * W + 2 * CONV_WIDTH:4 * W + 3 * CONV_WIDTH]
    u = cc * cx
    ubuf_sc[8:8 + R, :] = u
    u1 = ubuf_sc[7:7 + R, :]
    u2 = ubuf_sc[6:6 + R, :]
    wc = convw_ref[...]
    y_cv = cb * (u2 * wc[0:1, :] + u1 * wc[1:2, :] + u * wc[2:3, :])
    y_ref[0, :, W:W + CONV_WIDTH] = y_cv.astype(BF16)
    ubuf_sc[0:8, :] = ubuf_sc[R:R + 8, :]


def _hgrn_conv(hc, lb_logits, onorm_g, conv_w, layer):
    B, S, _ = hc.shape
    R = HG_ROWS
    const = lambda b, s: (0, 0)
    return pl.pallas_call(
        functools.partial(_hgrn_conv_kernel, layer=layer),
        grid=(B, S // R),
        in_specs=[
            pl.BlockSpec((1, R, HC_WIDTH), lambda b, s: (b, s, 0)),
            pl.BlockSpec((DEPTH, HG_WIDTH), const),
            pl.BlockSpec((1, HG_WIDTH), const),
            pl.BlockSpec((CONV_K, CONV_WIDTH), const),
        ],
        out_specs=pl.BlockSpec((1, R, HG_WIDTH + CONV_WIDTH), lambda b, s: (b, s, 0)),
        out_shape=jax.ShapeDtypeStruct((B, S, HG_WIDTH + CONV_WIDTH), BF16),
        scratch_shapes=[
            pltpu.VMEM((HG_WIDTH, HG_WIDTH), F32),
            pltpu.VMEM((R + 8, CONV_WIDTH), F32),
            pltpu.VMEM((R, HG_WIDTH), F32),
        ],
        compiler_params=pltpu.CompilerParams(
            dimension_semantics=("parallel", "arbitrary"), vmem_limit_bytes=VMEM_LIMIT),
        name="hgrn_conv",
    )(hc, lb_logits, onorm_g, conv_w)


def _mem_kv_kernel(mem_ref, g_ref, w_ref, gk_ref, k_out, v_out):
    m = mem_ref[0]
    mn = m * _rms(m) * g_ref[0]
    kv = _dot(mn.astype(BF16), w_ref[0])
    hw = X_HEADS * X_HEAD_DIM
    gk = gk_ref[0]
    for hd in range(X_HEADS):
        sl = slice(hd * X_HEAD_DIM, (hd + 1) * X_HEAD_DIM)
        kh = kv[:, sl]
        k_out[0, 0, :, sl] = (kh * _rms(kh) * gk).astype(BF16)
    v_out[0, 0] = kv[:, hw:].astype(BF16)


def _mem_kv(mem, mem_norm_g, w_xkv, xk_norm_g):
    B, M, _ = mem.shape
    L = w_xkv.shape[0]
    hw = X_HEADS * X_HEAD_DIM
    out_spec = pl.BlockSpec((1, 1, M, hw), lambda l, b: (l, b, 0, 0))
    return pl.pallas_call(
        _mem_kv_kernel,
        grid=(L, B),
        in_specs=[
            pl.BlockSpec((1, M, D_MODEL), lambda l, b: (b, 0, 0)),
            pl.BlockSpec((1, 1, D_MODEL), lambda l, b: (l, 0, 0)),
            pl.BlockSpec((1, D_MODEL, 2 * hw), lambda l, b: (l, 0, 0)),
            pl.BlockSpec((1, 1, X_HEAD_DIM), lambda l, b: (l, 0, 0)),
        ],
        out_specs=[out_spec, out_spec],
        out_shape=[jax.ShapeDtypeStruct((L, B, M, hw), BF16)] * 2,
        compiler_params=pltpu.CompilerParams(
            dimension_semantics=("parallel", "parallel"), vmem_limit_bytes=VMEM_LIMIT),
        name="mem_kv",
    )(mem, mem_norm_g, w_xkv, xk_norm_g)


def _out_xattn_kernel(x_ref, ya_ref, yb_ref, wo_ref, g_ref, wq_ref, gq_ref,
                      k_ref, v_ref, wxo_ref, o_ref):
    half = wo_ref.shape[0] // 2
    x1 = x_ref[0] + _dot(ya_ref[0], wo_ref[:half, :]) + _dot(yb_ref[0], wo_ref[half:, :])
    h = x1 * _rms(x1) * g_ref[...]
    q = _dot(h.astype(BF16), wq_ref[...])
    gq = gq_ref[...] * (1.0 / math.sqrt(X_HEAD_DIM))
    outs = []
    for hd in range(X_HEADS):
        sl = slice(hd * X_HEAD_DIM, (hd + 1) * X_HEAD_DIM)
        qh = q[:, sl]
        qh = (qh * _rms(qh) * gq).astype(BF16)
        s = _dot_nt(qh, k_ref[0, 0, :, sl])
        p = jnp.exp(s - jnp.max(s, axis=1, keepdims=True))
        l = jnp.sum(p, axis=1, keepdims=True)
        outs.append(_dot(p.astype(BF16), v_ref[0, 0, :, sl]) / l)
    o = jnp.concatenate(outs, axis=1).astype(BF16)
    o_ref[0] = x1 + _dot(o, wxo_ref[...])


def _out_xattn(x, y_mla, y_hc, w_out, g, w_xq, gq, k_mem, v_mem, w_xo, layer):
    B, S, _ = x.shape
    tm = ROW_TILE
    hw = X_HEADS * X_HEAD_DIM
    const = lambda b, s: (0, 0)
    row = lambda w: pl.BlockSpec((1, tm, w), lambda b, s: (b, s, 0))
    mem_spec = pl.BlockSpec((1, 1, MEM_LEN, hw), lambda b, s: (layer, b, 0, 0))
    return pl.pallas_call(
        _out_xattn_kernel,
        grid=(B, S // tm),
        in_specs=[
            row(D_MODEL), row(y_mla.shape[-1]), row(y_hc.shape[-1]),
            pl.BlockSpec((D_MODEL, D_MODEL), const),
            pl.BlockSpec((1, D_MODEL), const),
            pl.BlockSpec((D_MODEL, hw), const),
            pl.BlockSpec((1, X_HEAD_DIM), const),
            mem_spec, mem_spec,
            pl.BlockSpec((hw, D_MODEL), const),
        ],
        out_specs=row(D_MODEL),
        out_shape=jax.ShapeDtypeStruct((B, S, D_MODEL), F32),
        compiler_params=pltpu.CompilerParams(
            dimension_semantics=("parallel", "parallel"), vmem_limit_bytes=VMEM_LIMIT),
        name="out_xattn",
    )(x, y_mla, y_hc, w_out, g, w_xq, gq, k_mem, v_mem, w_xo)


def _mlp_kernel(x_ref, g_ref, wu_ref, wd_ref, o_ref):
    x = x_ref[0]
    h = (x * _rms(x) * g_ref[...]).astype(BF16)
    acc = x
    step = D_MODEL
    for c in range(D_FF // step):
        u = _dot(h, wu_ref[:, c * step:(c + 1) * step])
        a = jnp.square(jnp.maximum(u, 0.0)).astype(BF16)
        acc = acc + _dot(a, wd_ref[c * step:(c + 1) * step, :])
    o_ref[0] = acc


def _mlp(x, g, w_up, w_down):
    B, S, _ = x.shape
    tm = ROW_TILE
    const = lambda b, s: (0, 0)
    row = pl.BlockSpec((1, tm, D_MODEL), lambda b, s: (b, s, 0))
    return pl.pallas_call(
        _mlp_kernel,
        grid=(B, S // tm),
        in_specs=[row, pl.BlockSpec((1, D_MODEL), const),
                  pl.BlockSpec((D_MODEL, D_FF), const),
                  pl.BlockSpec((D_FF, D_MODEL), const)],
        out_specs=row,
        out_shape=jax.ShapeDtypeStruct((B, S, D_MODEL), F32),
        compiler_params=pltpu.CompilerParams(
            dimension_semantics=("parallel", "parallel"), vmem_limit_bytes=VMEM_LIMIT),
        name="mlp",
    )(x, g, w_up, w_down)


def _scatter_lanes(w_cols):
    idx = jnp.asarray(np.where(_LANE_MAP >= 0, _LANE_MAP, 0), jnp.int32)
    valid = jnp.asarray(_LANE_MAP >= 0)
    return jnp.where(valid, jnp.take(w_cols, idx, axis=-1), 0.0)


def _prep_layer(l, positions, w_in, w_uq, w_ukv, mla_qn_g, mla_kn_g):
    o_kr = MLA_Q_RANK + MLA_KV_RANK
    w = w_in[l]
    kr_src = jnp.concatenate(
        [jnp.zeros((D_MODEL, MLA_NOPE), F32), w[:, o_kr:o_kr + MLA_ROPE]], axis=1)
    w_in_p = jnp.concatenate(
        [w[:, :o_kr], _scatter_lanes(kr_src), w[:, o_kr + MLA_ROPE:]], axis=1).astype(BF16)

    def with_swapped_halves(w_slots):
        rank = w_slots.shape[0]
        both = jnp.concatenate([w_slots, jnp.roll(w_slots, HEAD_SLOT // 2, axis=-1)], axis=1)
        return both.reshape(rank, 2 * MLA_HEADS * HEAD_SLOT).astype(BF16)

    wq = w_uq[l].reshape(MLA_Q_RANK, MLA_HEADS, MLA_QK)
    w_uq_p = with_swapped_halves(_scatter_lanes(wq))

    wkv = w_ukv[l].reshape(MLA_KV_RANK, MLA_HEADS, MLA_NOPE + MLA_V)
    k_src = jnp.concatenate(
        [wkv[..., :MLA_NOPE], jnp.zeros((MLA_KV_RANK, MLA_HEADS, MLA_ROPE), F32)], axis=-1)
    w_uk_p = with_swapped_halves(_scatter_lanes(k_src))
    w_vt = wkv[..., MLA_NOPE:].reshape(MLA_KV_RANK, MLA_HEADS * MLA_V).T.astype(BF16)

    inv_freq = ROPE_BASE ** (-jnp.arange(0, MLA_ROPE, 2, dtype=F32) / MLA_ROPE)
    ang = positions.astype(F32)[:, None] * inv_freq[None, :]
    cos, sin = jnp.cos(ang), jnp.sin(ang)
    S = positions.shape[0]
    pad = HEAD_SLOT // 2 - ROPE_HALF
    c_tab = jnp.concatenate([cos, jnp.ones((S, pad), F32), cos, jnp.ones((S, pad), F32)], axis=1)
    s_tab = jnp.concatenate([-sin, jnp.zeros((S, pad), F32), sin, jnp.zeros((S, pad), F32)], axis=1)

    def tables(gain, scale):
        g = _scatter_lanes(gain)[None, :] * scale
        return c_tab * g, s_tab * jnp.roll(g, HEAD_SLOT // 2, axis=1)

    cq_t, sq_t = tables(mla_qn_g[l], math.log2(math.e) / math.sqrt(MLA_QK))
    ck_t, sk_t = tables(mla_kn_g[l], 1.0)
    return w_in_p, w_uq_p, w_uk_p, w_vt, (cq_t, sq_t, ck_t, sk_t)


def kernel(x, mem, positions, mix_norm_g, w_in, mla_q_norm_g, mla_kv_norm_g, w_uq, w_ukv,
           mla_qn_g, mla_kn_g, hgrn_lb_logits, hgrn_o_norm_g, conv_w, w_out,
           xattn_norm_g, mem_norm_g, w_xq, w_xkv, xq_norm_g, xk_norm_g, w_xo,
           mlp_norm_g, w_up, w_down):
    B, S, _ = x.shape
    L = w_in.shape[0]
    nq, nk = S // ATT_TQ, S // ATT_TK
    pq = positions.reshape(nq, ATT_TQ)
    pk = positions.reshape(nk, ATT_TK)
    vis = jnp.min(pk, axis=1)[None, :] <= jnp.max(pq, axis=1)[:, None]
    n_kv = jnp.max(jnp.where(vis, jnp.arange(1, nk + 1, dtype=jnp.int32)[None, :], 0),
                   axis=1).astype(jnp.int32)
    full = jnp.max(pk, axis=1)[None, :] <= jnp.min(pq, axis=1)[:, None]
    n_full = jnp.sum(jnp.cumprod(full.astype(jnp.int32), axis=1), axis=1).astype(jnp.int32)
    posq = positions.reshape(nq, 1, ATT_TQ)
    posk = positions.reshape(S, 1)

    k_mem, v_mem = _mem_kv(mem, mem_norm_g.reshape(L, 1, D_MODEL), w_xkv.astype(BF16),
                           xk_norm_g.reshape(L, 1, X_HEAD_DIM))
    for l in range(L):
        w_in_p, w_uq_p, w_uk_p, w_vt, tabs = _prep_layer(l, positions, w_in, w_uq, w_ukv,
                                                         mla_qn_g, mla_kn_g)
        lat, hc = _in_proj(x, mix_norm_g[l][None], w_in_p)
        q, k, vt = _mla_qkv(lat, mla_q_norm_g[l][None], w_uq_p, mla_kv_norm_g[l][None],
                            w_uk_p, w_vt, tabs)
        score_bound = (math.log2(math.e) * math.sqrt(MLA_QK)
                       * jnp.max(jnp.abs(mla_qn_g[l])) * jnp.max(jnp.abs(mla_kn_g[l])))
        y_mla = lax.cond(score_bound <= ATT_PLAIN_MAX_SCORE,
                         functools.partial(_attention, online=False),
                         functools.partial(_attention, online=True),
                         q, k, vt, n_full, n_kv, posq, posk)
        y_hc = _hgrn_conv(hc, hgrn_lb_logits, hgrn_o_norm_g[l][None], conv_w[l], l)
        x = _out_xattn(x, y_mla, y_hc, w_out[l].astype(BF16), xattn_norm_g[l][None],
                       w_xq[l].astype(BF16), xq_norm_g[l][None], k_mem, v_mem,
                       w_xo[l].astype(BF16), l)
        x = _mlp(x, mlp_norm_g[l][None], w_up[l].astype(BF16), w_down[l].astype(BF16))
    return x
```

```python
import functools
import math

import numpy as np
import jax
import jax.numpy as jnp
from jax import lax
from jax.experimental import pallas as pl
from jax.experimental.pallas import tpu as pltpu

F32 = jnp.float32
BF16 = jnp.bfloat16

D_MODEL = 1024
DEPTH = 4
MEM_LEN = 256
EPS = 1e-6

MLA_HEADS = 8
MLA_NOPE = 64
MLA_ROPE = 32
MLA_V = 64
MLA_QK = MLA_NOPE + MLA_ROPE
MLA_Q_RANK = 384
MLA_KV_RANK = 256
ROPE_BASE = 10000.0

HG_HEADS = 4
HG_DK = 64
HG_CHUNK = 64
HG_SUB = 16
HG_GROUP = 8
HG_WIDTH = 256
CONV_WIDTH = 256
CONV_K = 3

X_HEADS = 4
X_HEAD_DIM = 128
D_FF = 4 * D_MODEL

LANES = 128
HEAD_SLOT = LANES
ROPE_HALF = MLA_ROPE // 2
N_IN_PAD = MLA_Q_RANK + MLA_KV_RANK + HEAD_SLOT + 4 * HG_WIDTH + 3 * CONV_WIDTH
HC_OFF = MLA_Q_RANK + MLA_KV_RANK + HEAD_SLOT
HC_WIDTH = N_IN_PAD - HC_OFF

ROW_TILE = 512
XATTN_SUB = 256
ATT_TQ = 256
ATT_TK = 256
ATT_HEADS = 8
ATT_CHAINS = 4
ATT_UNROLLS = (4, 2)
HG_ROWS = 512
MASK_VALUE = -1e30
ATT_PLAIN_MAX_SCORE = 64.0
EXP_CLAMP = 60.0
VMEM_LIMIT = 56 * 1024 * 1024


def _head_lane_map():
    m = -np.ones((HEAD_SLOT,), np.int64)
    m[0:ROPE_HALF] = MLA_NOPE + np.arange(ROPE_HALF)
    m[ROPE_HALF:64] = np.arange(64 - ROPE_HALF)
    m[64:64 + ROPE_HALF] = MLA_NOPE + ROPE_HALF + np.arange(ROPE_HALF)
    m[64 + ROPE_HALF:96] = (64 - ROPE_HALF) + np.arange(ROPE_HALF)
    return m


_LANE_MAP = _head_lane_map()


def _rms(x, n=None):
    n = x.shape[-1] if n is None else n
    return lax.rsqrt(jnp.sum(x * x, axis=-1, keepdims=True) * (1.0 / n) + EPS)


def _sigmoid(x):
    return 1.0 / (1.0 + jnp.exp(-x))


def _dot(a, b):
    return jnp.dot(a, b, preferred_element_type=F32)


def _dot_nt(a, b):
    return lax.dot_general(a, b, (((1,), (1,)), ((), ())), preferred_element_type=F32)


def _dot_tn(a, b):
    return lax.dot_general(a, b, (((0,), (0,)), ((), ())), preferred_element_type=F32)


def _in_proj_kernel(x_ref, g_ref, w_in_ref, lat_out, hc_out):
    x = x_ref[0]
    h = (x * _rms(x) * g_ref[...]).astype(BF16)
    lat_out[0] = _dot(h, w_in_ref[:, :HC_OFF])
    hc_out[0] = _dot(h, w_in_ref[:, HC_OFF:])


def _in_proj(x, g, w_in):
    B, S, _ = x.shape
    tm = ROW_TILE
    const = lambda b, s: (0, 0)
    return pl.pallas_call(
        _in_proj_kernel,
        grid=(B, S // tm),
        in_specs=[
            pl.BlockSpec((1, tm, D_MODEL), lambda b, s: (b, s, 0)),
            pl.BlockSpec((1, D_MODEL), const),
            pl.BlockSpec((D_MODEL, N_IN_PAD), const),
        ],
        out_specs=[
            pl.BlockSpec((1, tm, HC_OFF), lambda b, s: (b, s, 0)),
            pl.BlockSpec((1, tm, HC_WIDTH), lambda b, s: (b, s, 0)),
        ],
        out_shape=[
            jax.ShapeDtypeStruct((B, S, HC_OFF), F32),
            jax.ShapeDtypeStruct((B, S, HC_WIDTH), F32),
        ],
        compiler_params=pltpu.CompilerParams(
            dimension_semantics=("parallel", "parallel"), vmem_limit_bytes=VMEM_LIMIT),
        name="in_proj",
    )(x, g, w_in)


def _mla_qkv_kernel(lat_ref, gq_ref, w_uq_ref, gkv_ref, w_uk_ref, w_vt_ref,
                    cq_tab, sq_tab, ck_tab, sk_tab, q_out, k_out, vt_out):
    cq = lat_ref[0, :, :MLA_Q_RANK]
    ckv = lat_ref[0, :, MLA_Q_RANK:MLA_Q_RANK + MLA_KV_RANK]
    kr = lat_ref[0, :, MLA_Q_RANK + MLA_KV_RANK:]
    cqn = (cq * _rms(cq) * gq_ref[...]).astype(BF16)
    ckvn = (ckv * _rms(ckv) * gkv_ref[...]).astype(BF16)
    hw = MLA_HEADS * HEAD_SLOT
    qf = _dot(cqn, w_uq_ref[...])
    kf = _dot(ckvn, w_uk_ref[...])
    vt = _dot_nt(w_vt_ref[...], ckvn).astype(BF16)
    for j in range(vt_out.shape[1]):
        vt_out[0, j] = vt[:, j * ATT_TK:(j + 1) * ATT_TK]
    kr_sw = pltpu.roll(kr, HEAD_SLOT // 2, 1)
    cq_t, sq_t, ck_t, sk_t = cq_tab[...], sq_tab[...], ck_tab[...], sk_tab[...]
    for hd in range(MLA_HEADS):
        sl = slice(hd * HEAD_SLOT, (hd + 1) * HEAD_SLOT)
        sw = slice(hw + hd * HEAD_SLOT, hw + (hd + 1) * HEAD_SLOT)
        qh = qf[:, sl]
        qh = (qh * cq_t + qf[:, sw] * sq_t) * _rms(qh, MLA_QK)
        q_out[0, hd] = qh.T.astype(BF16)
        kh = kf[:, sl] + kr
        kh = (kh * ck_t + (kf[:, sw] + kr_sw) * sk_t) * _rms(kh, MLA_QK)
        k_out[0, hd] = kh.astype(BF16)


def _mla_qkv(lat, gq, w_uq, gkv, w_uk, w_vt, tabs):
    B, S, _ = lat.shape
    tm = ROW_TILE
    vw = MLA_HEADS * MLA_V
    const = lambda b, s: (0, 0)
    tab_spec = pl.BlockSpec((tm, HEAD_SLOT), lambda b, s: (s, 0))
    return pl.pallas_call(
        _mla_qkv_kernel,
        grid=(B, S // tm),
        in_specs=[
            pl.BlockSpec((1, tm, HC_OFF), lambda b, s: (b, s, 0)),
            pl.BlockSpec((1, MLA_Q_RANK), const),
            pl.BlockSpec((MLA_Q_RANK, 2 * MLA_HEADS * HEAD_SLOT), const),
            pl.BlockSpec((1, MLA_KV_RANK), const),
            pl.BlockSpec((MLA_KV_RANK, 2 * MLA_HEADS * HEAD_SLOT), const),
            pl.BlockSpec((vw, MLA_KV_RANK), const),
            tab_spec, tab_spec, tab_spec, tab_spec,
        ],
        out_specs=[
            pl.BlockSpec((1, MLA_HEADS, HEAD_SLOT, tm), lambda b, s: (b, 0, 0, s)),
            pl.BlockSpec((1, MLA_HEADS, tm, HEAD_SLOT), lambda b, s: (b, 0, s, 0)),
            pl.BlockSpec((1, tm // ATT_TK, vw, ATT_TK), lambda b, s: (b, s, 0, 0)),
        ],
        out_shape=[
            jax.ShapeDtypeStruct((B, MLA_HEADS, HEAD_SLOT, S), BF16),
            jax.ShapeDtypeStruct((B, MLA_HEADS, S, HEAD_SLOT), BF16),
            jax.ShapeDtypeStruct((B, S // ATT_TK, vw, ATT_TK), BF16),
        ],
        compiler_params=pltpu.CompilerParams(
            dimension_semantics=("parallel", "parallel"), vmem_limit_bytes=VMEM_LIMIT),
        name="mla_qkv",
    )(lat, gq, w_uq, gkv, w_uk, w_vt, *tabs)


def _attn_kernel(nfull_ref, nkv_ref, q_ref, k_ref, vt_ref, posq_ref, posk_ref, o_ref,
                 p_sc, acc_sc, *maybe_s_sc, online):
    qi = pl.program_id(2)
    n_full = nfull_ref[qi]
    n_kv = nkv_ref[qi]
    tq = q_ref.shape[3]
    nk = vt_ref.shape[1]
    heads = range(ATT_HEADS)
    qts = [q_ref[0, hd] for hd in heads]
    posq = posq_ref[0]

    groups = [list(heads)[i:i + ATT_CHAINS] for i in range(0, ATT_HEADS, ATT_CHAINS)]

    def score(ki, hd):
        off = pl.multiple_of(ki * ATT_TK, ATT_TK)
        return _dot(k_ref[0, hd, pl.ds(off, ATT_TK), :], qts[hd])

    def pv(ki, hd, p=None):
        p = p_sc[hd] if p is None else p
        return _dot(vt_ref[0, ki, hd * MLA_V:(hd + 1) * MLA_V, :], p)

    def issue(ki_scores, ki_pv):
        ss, pvs = {}, {}
        for g in groups:
            for hd in g:
                ss[hd] = score(ki_scores, hd)
            for hd in g:
                pvs[hd] = pv(ki_pv, hd)
        return ss, pvs

    def mask_of(ki):
        off = pl.multiple_of(ki * ATT_TK, ATT_TK)
        return posk_ref[pl.ds(off, ATT_TK), :] <= posq

    def step_plain(ki, carry, masked):
        ss, pvs = issue(ki, jnp.maximum(ki - 1, 0))
        if masked:
            mask = mask_of(ki)
        new = []
        for hd in heads:
            s = jnp.where(mask, ss[hd], MASK_VALUE) if masked else ss[hd]
            p = jnp.exp2(s)
            new.append(carry[hd] + jnp.sum(p, axis=0, keepdims=True))
            acc_sc[hd] = acc_sc[hd] + pvs[hd]
            p_sc[hd] = p.astype(BF16)
        return tuple(new)

    def multi_plain(j, carry, n_tiles, first):
        k0 = first + n_tiles * j
        ss, pvs = issue(k0, jnp.maximum(k0 - 1, 0))
        l_add = {hd: 0.0 for hd in heads}
        for hd in heads:
            acc_sc[hd] = acc_sc[hd] + pvs[hd]
        for u in range(1, n_tiles):
            ss_next = {}
            for g in groups:
                for hd in g:
                    ss_next[hd] = score(k0 + u, hd)
                for hd in g:
                    p = jnp.exp2(ss[hd])
                    l_add[hd] = l_add[hd] + jnp.sum(p, axis=0, keepdims=True)
                    acc_sc[hd] = acc_sc[hd] + pv(k0 + u - 1, hd, p.astype(BF16))
            ss = ss_next
        new = []
        for hd in heads:
            p = jnp.exp2(ss[hd])
            new.append(carry[hd] + l_add[hd] + jnp.sum(p, axis=0, keepdims=True))
            p_sc[hd] = p.astype(BF16)
        return tuple(new)

    def step_online(ki, carry, masked):
        (s_sc,) = maybe_s_sc
        s_next, pvs = issue(jnp.minimum(ki + 1, nk - 1), jnp.maximum(ki - 1, 0))
        if masked:
            mask = mask_of(ki)
        new = []
        for hd in heads:
            m_old, l_old = carry[hd]
            s = s_sc[hd]
            if masked:
                s = jnp.where(mask, s, MASK_VALUE)
            m_new = jnp.maximum(m_old, jnp.max(s, axis=0, keepdims=True))
            alpha = jnp.exp2(m_old - m_new)
            p = jnp.exp2(s - m_new)
            l_new = alpha * l_old + jnp.sum(p, axis=0, keepdims=True)
            acc_sc[hd] = alpha * (acc_sc[hd] + pvs[hd])
            p_sc[hd] = p.astype(BF16)
            new.append((m_new, l_new))
        for hd in heads:
            s_sc[hd] = s_next[hd]
        return tuple(new)

    p_sc[...] = jnp.zeros(p_sc.shape, BF16)
    acc_sc[...] = jnp.zeros(acc_sc.shape, F32)
    if online:
        for hd in heads:
            maybe_s_sc[0][hd] = score(0, hd)
        init = tuple((jnp.full((1, tq), MASK_VALUE, F32), jnp.zeros((1, tq), F32))
                     for _ in heads)
        step = step_online
    else:
        init = tuple(jnp.zeros((1, tq), F32) for _ in heads)
        step = step_plain
    n_done = 0
    if not online:
        for n_tiles in ATT_UNROLLS:
            n_steps = (n_full - n_done) // n_tiles
            init = lax.fori_loop(
                0, n_steps, functools.partial(multi_plain, n_tiles=n_tiles, first=n_done), init)
            n_done = n_done + n_tiles * n_steps
    carry = lax.fori_loop(n_done, n_full, functools.partial(step, masked=False), init)
    carry = lax.fori_loop(n_full, n_kv, functools.partial(step, masked=True), carry)
    last = jnp.maximum(n_kv - 1, 0)
    ls = [c[1] for c in carry] if online else carry
    o_t = jnp.concatenate([(acc_sc[hd] + pv(last, hd)) / ls[hd] for hd in heads], axis=0)
    o_ref[0] = o_t.T.astype(BF16)


def _attention(q, k, vt, n_full, n_kv, posq, posk, online):
    B, H, _, S = q.shape
    tq = ATT_TQ
    nh = ATT_HEADS
    nk = S // ATT_TK
    scratch = [pltpu.VMEM((nh, ATT_TK, tq), BF16), pltpu.VMEM((nh, MLA_V, tq), F32)]
    if online:
        scratch.append(pltpu.VMEM((nh, ATT_TK, tq), F32))
    grid_spec = pltpu.PrefetchScalarGridSpec(
        num_scalar_prefetch=2,
        grid=(B, H // nh, S // tq),
        in_specs=[
            pl.BlockSpec((1, nh, HEAD_SLOT, tq), lambda b, hp, qi, *_: (b, hp, 0, qi)),
            pl.BlockSpec((1, nh, S, HEAD_SLOT), lambda b, hp, qi, *_: (b, hp, 0, 0)),
            pl.BlockSpec((1, nk, nh * MLA_V, ATT_TK), lambda b, hp, qi, *_: (b, 0, hp, 0)),
            pl.BlockSpec((1, 1, tq), lambda b, hp, qi, *_: (qi, 0, 0)),
            pl.BlockSpec((S, 1), lambda b, hp, qi, *_: (0, 0)),
        ],
        out_specs=pl.BlockSpec((1, tq, nh * MLA_V), lambda b, hp, qi, *_: (b, qi, hp)),
        scratch_shapes=scratch,
    )
    return pl.pallas_call(
        functools.partial(_attn_kernel, online=online),
        grid_spec=grid_spec,
        out_shape=jax.ShapeDtypeStruct((B, S, H * MLA_V), BF16),
        compiler_params=pltpu.CompilerParams(
            dimension_semantics=("parallel", "parallel", "arbitrary"),
            vmem_limit_bytes=VMEM_LIMIT),
        name="mla_attention_online" if online else "mla_attention",
    )(n_full, n_kv, q, k, vt, posq, posk)


def _hgrn_conv_kernel(hc_ref, lbl_ref, onorm_ref, convw_ref, y_ref,
                      st_sc, ubuf_sc, o_sc, *, layer):
    sb = pl.program_id(1)
    R = HG_ROWS
    C = HG_CHUNK
    W = HG_WIDTH

    @pl.when(sb == 0)
    def _():
        st_sc[...] = jnp.zeros(st_sc.shape, F32)
        ubuf_sc[0:8, :] = jnp.zeros((8, CONV_WIDTH), F32)

    lg = lbl_ref[...]
    e = jnp.exp(lg - jnp.max(lg, axis=0, keepdims=True))
    soft = e / jnp.sum(e, axis=0, keepdims=True)
    lb = jnp.zeros((1, W), F32)
    for i in range(1, layer + 1):
        lb = lb + soft[i:i + 1, :]
    lb = jnp.maximum(lb, 0.0)
    log_lb = jnp.log(lb)
    log_1m = jnp.log1p(-lb)

    ri = lax.broadcasted_iota(jnp.int32, (C, C), 0)
    ci = lax.broadcasted_iota(jnp.int32, (C, C), 1)
    sub_shift = HG_SUB.bit_length() - 1
    dk_shift = HG_DK.bit_length() - 1
    tri = jnp.where(ci <= ri, 1.0, 0.0)
    tri_blk = jnp.where((ci >> sub_shift) == (ri >> sub_shift), tri, 0.0)
    cum_mat = jnp.concatenate([tri, tri_blk], axis=0).astype(BF16)
    tt = ri & (HG_SUB - 1)
    lane_w = lax.broadcasted_iota(jnp.int32, (1, W), 1)
    head_masks = [jnp.where((lane_w >> dk_shift) == hh, 1.0, 0.0) for hh in range(HG_HEADS)]
    rw = lax.broadcasted_iota(jnp.int32, (W, W), 0)
    cw = lax.broadcasted_iota(jnp.int32, (W, W), 1)
    bd_mask = (rw >> dk_shift) == (cw >> dk_shift)
    ones_bd = jnp.where(bd_mask, 1.0, 0.0).astype(BF16)
    n_sub = C // HG_SUB

    def group(gi, carry):
        base = gi * (HG_GROUP * C)
        starts = [pl.multiple_of(base + j * C, C) for j in range(HG_GROUP)]
        chunks = range(HG_GROUP)
        gates, splits = [], []
        for r0 in starts:
            zq = hc_ref[0, pl.ds(r0, C), 0:W]
            zf = hc_ref[0, pl.ds(r0, C), W:2 * W]
            vi = hc_ref[0, pl.ds(r0, C), 2 * W:3 * W]
            q = zq * _sigmoid(zq)
            t = jnp.exp(-jnp.abs(zf))
            d = 1.0 + t
            r = 1.0 / d
            log_sig = jnp.minimum(zf, 0.0) - jnp.log(d)
            c2 = log_1m + log_sig
            log_f = jnp.maximum(log_lb, c2) + jnp.log(1.0 + jnp.exp(-jnp.abs(log_lb - c2)))
            kk = (1.0 - lb) * jnp.where(zf >= 0.0, t * r, r)
            f_hi = log_f.astype(BF16)
            r1 = log_f - f_hi.astype(F32)
            f_mid = r1.astype(BF16)
            f_lo = (r1 - f_mid.astype(F32)).astype(BF16)
            gates.append((q, kk, vi.astype(BF16)))
            splits.append((f_hi, f_mid, f_lo))
        cums = [_dot(cum_mat, s[0]) + _dot(cum_mat, s[1]) + _dot(cum_mat, s[2]) for s in splits]

        def diag_direct(j, i):
            q, kk, vb = gates[j]
            rows = slice(i * HG_SUB, (i + 1) * HG_SUB)
            b_t, q_t, k_t = cums[j][:C][rows], q[rows], kk[rows]
            v_t = vb[rows].astype(F32)
            t_idx = lax.broadcasted_iota(jnp.int32, (HG_SUB, W), 0)
            slabs = []
            for s in range(HG_SUB):
                w = jnp.exp(jnp.minimum(b_t - b_t[s:s + 1, :], 0.0))
                slabs.append(jnp.where(t_idx >= s, q_t * k_t[s:s + 1, :] * w, 0.0))
            x = jnp.concatenate(slabs, axis=0).astype(BF16)
            a = _dot(x, ones_bd)
            o = a[0:HG_SUB] * v_t[0:1, :]
            for s in range(1, HG_SUB):
                o = o + a[s * HG_SUB:(s + 1) * HG_SUB] * v_t[s:s + 1, :]
            return o

        def finish(direct_diag):
            qbs, b_lasts, lhss, kds, kdls = [], [], [], [], []
            for j in chunks:
                q, kk, _ = gates[j]
                b = cums[j][:C]
                g = cums[j][C:]
                b_last = b[C - 1:C, :]
                rref = b - g
                qd = q * jnp.exp(g)
                qbs.append((q * jnp.exp(b)).astype(BF16))
                b_lasts.append(b_last)
                kdls.append((kk * jnp.exp(b_last - b)).astype(BF16))
                lhs_j, kd_j = [], []
                for i in range(n_sub):
                    rr = rref[i * HG_SUB:i * HG_SUB + 1, :]
                    kd_j.append((kk * jnp.exp(jnp.minimum(rr - b, EXP_CLAMP))).astype(BF16))
                    qi = qd[i * HG_SUB:(i + 1) * HG_SUB]
                    lhs_j.append(jnp.concatenate([qi * hm for hm in head_masks],
                                                 axis=0).astype(BF16))
                lhss.append(lhs_j)
                kds.append(kd_j)
            attn = [[_dot_nt(lhss[j][i], kds[j][i]) for i in range(n_sub)] for j in chunks]
            upds = [_dot_tn(gates[j][2], kdls[j]) for j in chunks]
            if direct_diag:
                attn = [[jnp.where(ci < i * HG_SUB, attn[j][i], 0.0).astype(BF16)
                         for i in range(n_sub)] for j in chunks]
            else:
                attn = [[jnp.where(ci <= i * HG_SUB + tt, attn[j][i], 0.0).astype(BF16)
                         for i in range(n_sub)] for j in chunks]
            pvs = [[_dot(attn[j][i], gates[j][2]) for i in range(n_sub)] for j in chunks]
            st = st_sc[...]
            for j in chunks:
                o_parts = []
                for i in range(n_sub):
                    pv = pvs[j][i]
                    oi = pv[0:HG_SUB] * head_masks[0]
                    for hh in range(1, HG_HEADS):
                        oi = oi + pv[hh * HG_SUB:(hh + 1) * HG_SUB] * head_masks[hh]
                    if direct_diag:
                        oi = oi + diag_direct(j, i)
                    o_parts.append(oi)
                o_inter = _dot_nt(qbs[j], st.astype(BF16))
                o_sc[pl.ds(starts[j], C), :] = o_inter + jnp.concatenate(o_parts, axis=0)
                st = st * jnp.exp(b_lasts[j]) + jnp.where(bd_mask, upds[j], 0.0)
            st_sc[...] = st
            return jnp.int32(0)

        g_min = cums[0][C:]
        for j in range(1, HG_GROUP):
            g_min = jnp.minimum(g_min, cums[j][C:])
        lax.cond(jnp.min(g_min) < -EXP_CLAMP,
                 functools.partial(finish, True), functools.partial(finish, False))
        return carry

    lax.fori_loop(0, R // (C * HG_GROUP), group, 0)

    o = o_sc[...]
    ms = _dot((o * o).astype(BF16), ones_bd) * (1.0 / HG_DK)
    zg = hc_ref[0, :, 3 * W:4 * W]
    y_hg = o * lax.rsqrt(ms + EPS) * onorm_ref[...] * (zg * _sigmoid(zg))
    y_ref[0, :, 0:W] = y_hg.astype(BF16)

    cb = hc_ref[0, :, 4 * W:4 * W + CONV_WIDTH]
    cc = hc_ref[0, :, 4 * W + CONV_WIDTH:4 * W + 2 * CONV_WIDTH]
    cx = hc_ref[0, :, 4 * W + 2 * CONV_WIDTH:4 * W + 3 * CONV_WIDTH]
    u = cc * cx
    ubuf_sc[8:8 + R, :] = u
    u1 = ubuf_sc[7:7 + R, :]
    u2 = ubuf_sc[6:6 + R, :]
    wc = convw_ref[...]
    y_cv = cb * (u2 * wc[0:1, :] + u1 * wc[1:2, :] + u * wc[2:3, :])
    y_ref[0, :, W:W + CONV_WIDTH] = y_cv.astype(BF16)
    ubuf_sc[0:8, :] = ubuf_sc[R:R + 8, :]


def _hgrn_conv(hc, lb_logits, onorm_g, conv_w, layer):
    B, S, _ = hc.shape
    R = HG_ROWS
    const = lambda b, s: (0, 0)
    return pl.pallas_call(
        functools.partial(_hgrn_conv_kernel, layer=layer),
        grid=(B, S // R),
        in_specs=[
            pl.BlockSpec((1, R, HC_WIDTH), lambda b, s: (b, s, 0)),
            pl.BlockSpec((DEPTH, HG_WIDTH), const),
            pl.BlockSpec((1, HG_WIDTH), const),
            pl.BlockSpec((CONV_K, CONV_WIDTH), const),
        ],
        out_specs=pl.BlockSpec((1, R, HG_WIDTH + CONV_WIDTH), lambda b, s: (b, s, 0)),
        out_shape=jax.ShapeDtypeStruct((B, S, HG_WIDTH + CONV_WIDTH), BF16),
        scratch_shapes=[
            pltpu.VMEM((HG_WIDTH, HG_WIDTH), F32),
            pltpu.VMEM((R + 8, CONV_WIDTH), F32),
            pltpu.VMEM((R, HG_WIDTH), F32),
        ],
        compiler_params=pltpu.CompilerParams(
            dimension_semantics=("parallel", "arbitrary"), vmem_limit_bytes=VMEM_LIMIT),
        name="hgrn_conv",
    )(hc, lb_logits, onorm_g, conv_w)


def _mem_kv_kernel(mem_ref, g_ref, w_ref, gk_ref, k_out, v_out):
    m = mem_ref[0]
    mn = m * _rms(m) * g_ref[0]
    kv = _dot(mn.astype(BF16), w_ref[0])
    hw = X_HEADS * X_HEAD_DIM
    gk = gk_ref[0]
    for hd in range(X_HEADS):
        sl = slice(hd * X_HEAD_DIM, (hd + 1) * X_HEAD_DIM)
        kh = kv[:, sl]
        k_out[0, 0, :, sl] = (kh * _rms(kh) * gk).astype(BF16)
    v_out[0, 0] = kv[:, hw:].astype(BF16)


def _mem_kv(mem, mem_norm_g, w_xkv, xk_norm_g):
    B, M, _ = mem.shape
    L = w_xkv.shape[0]
    hw = X_HEADS * X_HEAD_DIM
    out_spec = pl.BlockSpec((1, 1, M, hw), lambda l, b: (l, b, 0, 0))
    return pl.pallas_call(
        _mem_kv_kernel,
        grid=(L, B),
        in_specs=[
            pl.BlockSpec((1, M, D_MODEL), lambda l, b: (b, 0, 0)),
            pl.BlockSpec((1, 1, D_MODEL), lambda l, b: (l, 0, 0)),
            pl.BlockSpec((1, D_MODEL, 2 * hw), lambda l, b: (l, 0, 0)),
            pl.BlockSpec((1, 1, X_HEAD_DIM), lambda l, b: (l, 0, 0)),
        ],
        out_specs=[out_spec, out_spec],
        out_shape=[jax.ShapeDtypeStruct((L, B, M, hw), BF16)] * 2,
        compiler_params=pltpu.CompilerParams(
            dimension_semantics=("parallel", "parallel"), vmem_limit_bytes=VMEM_LIMIT),
        name="mem_kv",
    )(mem, mem_norm_g, w_xkv, xk_norm_g)


def _out_xattn_kernel(x_ref, ya_ref, yb_ref, wo_ref, g_ref, wq_ref, gq_ref,
                      k_ref, v_ref, wxo_ref, o_ref):
    half = wo_ref.shape[0] // 2
    tm = x_ref.shape[1]
    subs = [slice(j * XATTN_SUB, (j + 1) * XATTN_SUB) for j in range(tm // XATTN_SUB)]
    slots = [slice(hd * X_HEAD_DIM, (hd + 1) * X_HEAD_DIM) for hd in range(X_HEADS)]
    gq = gq_ref[...] * (1.0 / math.sqrt(X_HEAD_DIM))
    x1s = [x_ref[0, r, :] + _dot(ya_ref[0, r, :], wo_ref[:half, :])
           + _dot(yb_ref[0, r, :], wo_ref[half:, :]) for r in subs]
    hs = [(x1 * _rms(x1) * g_ref[...]).astype(BF16) for x1 in x1s]
    qs = [_dot(h, wq_ref[...]) for h in hs]
    qhs = [[(q[:, sl] * _rms(q[:, sl]) * gq).astype(BF16) for sl in slots] for q in qs]
    ss = [[_dot_nt(qh, k_ref[0, 0, :, sl]) for qh, sl in zip(row, slots)] for row in qhs]
    ps, ls = [], []
    for row in ss:
        p_row, l_row = [], []
        for s in row:
            p = jnp.exp(s - jnp.max(s, axis=1, keepdims=True))
            l_row.append(jnp.sum(p, axis=1, keepdims=True))
            p_row.append(p.astype(BF16))
        ps.append(p_row)
        ls.append(l_row)
    os_ = [jnp.concatenate([_dot(p, v_ref[0, 0, :, sl]) / l
                            for p, l, sl in zip(p_row, l_row, slots)], axis=1).astype(BF16)
           for p_row, l_row in zip(ps, ls)]
    for r, x1, o in zip(subs, x1s, os_):
        o_ref[0, r, :] = x1 + _dot(o, wxo_ref[...])


def _out_xattn(x, y_mla, y_hc, w_out, g, w_xq, gq, k_mem, v_mem, w_xo, layer):
    B, S, _ = x.shape
    tm = ROW_TILE
    hw = X_HEADS * X_HEAD_DIM
    const = lambda b, s: (0, 0)
    row = lambda w: pl.BlockSpec((1, tm, w), lambda b, s: (b, s, 0))
    mem_spec = pl.BlockSpec((1, 1, MEM_LEN, hw), lambda b, s: (layer, b, 0, 0))
    return pl.pallas_call(
        _out_xattn_kernel,
        grid=(B, S // tm),
        in_specs=[
            row(D_MODEL), row(y_mla.shape[-1]), row(y_hc.shape[-1]),
            pl.BlockSpec((D_MODEL, D_MODEL), const),
            pl.BlockSpec((1, D_MODEL), const),
            pl.BlockSpec((D_MODEL, hw), const),
            pl.BlockSpec((1, X_HEAD_DIM), const),
            mem_spec, mem_spec,
            pl.BlockSpec((hw, D_MODEL), const),
        ],
        out_specs=row(D_MODEL),
        out_shape=jax.ShapeDtypeStruct((B, S, D_MODEL), F32),
        compiler_params=pltpu.CompilerParams(
            dimension_semantics=("parallel", "parallel"), vmem_limit_bytes=VMEM_LIMIT),
        name="out_xattn",
    )(x, y_mla, y_hc, w_out, g, w_xq, gq, k_mem, v_mem, w_xo)


def _mlp_kernel(x_ref, g_ref, wu_ref, wd_ref, o_ref):
    x = x_ref[0]
    h = (x * _rms(x) * g_ref[...]).astype(BF16)
    acc = x
    step = D_MODEL
    for c in range(D_FF // step):
        u = _dot(h, wu_ref[:, c * step:(c + 1) * step])
        a = jnp.square(jnp.maximum(u, 0.0)).astype(BF16)
        acc = acc + _dot(a, wd_ref[c * step:(c + 1) * step, :])
    o_ref[0] = acc


def _mlp(x, g, w_up, w_down):
    B, S, _ = x.shape
    tm = ROW_TILE
    const = lambda b, s: (0, 0)
    row = pl.BlockSpec((1, tm, D_MODEL), lambda b, s: (b, s, 0))
    return pl.pallas_call(
        _mlp_kernel,
        grid=(B, S // tm),
        in_specs=[row, pl.BlockSpec((1, D_MODEL), const),
                  pl.BlockSpec((D_MODEL, D_FF), const),
                  pl.BlockSpec((D_FF, D_MODEL), const)],
        out_specs=row,
        out_shape=jax.ShapeDtypeStruct((B, S, D_MODEL), F32),
        compiler_params=pltpu.CompilerParams(
            dimension_semantics=("parallel", "parallel"), vmem_limit_bytes=VMEM_LIMIT),
        name="mlp",
    )(x, g, w_up, w_down)


def _scatter_lanes(w_cols):
    idx = jnp.asarray(np.where(_LANE_MAP >= 0, _LANE_MAP, 0), jnp.int32)
    valid = jnp.asarray(_LANE_MAP >= 0)
    return jnp.where(valid, jnp.take(w_cols, idx, axis=-1), 0.0)


def _prep_layer(l, positions, w_in, w_uq, w_ukv, mla_qn_g, mla_kn_g):
    o_kr = MLA_Q_RANK + MLA_KV_RANK
    w = w_in[l]
    kr_src = jnp.concatenate(
        [jnp.zeros((D_MODEL, MLA_NOPE), F32), w[:, o_kr:o_kr + MLA_ROPE]], axis=1)
    w_in_p = jnp.concatenate(
        [w[:, :o_kr], _scatter_lanes(kr_src), w[:, o_kr + MLA_ROPE:]], axis=1).astype(BF16)

    def with_swapped_halves(w_slots):
        rank = w_slots.shape[0]
        both = jnp.concatenate([w_slots, jnp.roll(w_slots, HEAD_SLOT // 2, axis=-1)], axis=1)
        return both.reshape(rank, 2 * MLA_HEADS * HEAD_SLOT).astype(BF16)

    wq = w_uq[l].reshape(MLA_Q_RANK, MLA_HEADS, MLA_QK)
    w_uq_p = with_swapped_halves(_scatter_lanes(wq))

    wkv = w_ukv[l].reshape(MLA_KV_RANK, MLA_HEADS, MLA_NOPE + MLA_V)
    k_src = jnp.concatenate(
        [wkv[..., :MLA_NOPE], jnp.zeros((MLA_KV_RANK, MLA_HEADS, MLA_ROPE), F32)], axis=-1)
    w_uk_p = with_swapped_halves(_scatter_lanes(k_src))
    w_vt = wkv[..., MLA_NOPE:].reshape(MLA_KV_RANK, MLA_HEADS * MLA_V).T.astype(BF16)

    inv_freq = ROPE_BASE ** (-jnp.arange(0, MLA_ROPE, 2, dtype=F32) / MLA_ROPE)
    ang = positions.astype(F32)[:, None] * inv_freq[None, :]
    cos, sin = jnp.cos(ang), jnp.sin(ang)
    S = positions.shape[0]
    pad = HEAD_SLOT // 2 - ROPE_HALF
    c_tab = jnp.concatenate([cos, jnp.ones((S, pad), F32), cos, jnp.ones((S, pad), F32)], axis=1)
    s_tab = jnp.concatenate([-sin, jnp.zeros((S, pad), F32), sin, jnp.zeros((S, pad), F32)], axis=1)

    def tables(gain, scale):
        g = _scatter_lanes(gain)[None, :] * scale
        return c_tab * g, s_tab * jnp.roll(g, HEAD_SLOT // 2, axis=1)

    cq_t, sq_t = tables(mla_qn_g[l], math.log2(math.e) / math.sqrt(MLA_QK))
    ck_t, sk_t = tables(mla_kn_g[l], 1.0)
    return w_in_p, w_uq_p, w_uk_p, w_vt, (cq_t, sq_t, ck_t, sk_t)


def kernel(x, mem, positions, mix_norm_g, w_in, mla_q_norm_g, mla_kv_norm_g, w_uq, w_ukv,
           mla_qn_g, mla_kn_g, hgrn_lb_logits, hgrn_o_norm_g, conv_w, w_out,
           xattn_norm_g, mem_norm_g, w_xq, w_xkv, xq_norm_g, xk_norm_g, w_xo,
           mlp_norm_g, w_up, w_down):
    B, S, _ = x.shape
    L = w_in.shape[0]
    nq, nk = S // ATT_TQ, S // ATT_TK
    pq = positions.reshape(nq, ATT_TQ)
    pk = positions.reshape(nk, ATT_TK)
    vis = jnp.min(pk, axis=1)[None, :] <= jnp.max(pq, axis=1)[:, None]
    n_kv = jnp.max(jnp.where(vis, jnp.arange(1, nk + 1, dtype=jnp.int32)[None, :], 0),
                   axis=1).astype(jnp.int32)
    full = jnp.max(pk, axis=1)[None, :] <= jnp.min(pq, axis=1)[:, None]
    n_full = jnp.sum(jnp.cumprod(full.astype(jnp.int32), axis=1), axis=1).astype(jnp.int32)
    posq = positions.reshape(nq, 1, ATT_TQ)
    posk = positions.reshape(S, 1)

    k_mem, v_mem = _mem_kv(mem, mem_norm_g.reshape(L, 1, D_MODEL), w_xkv.astype(BF16),
                           xk_norm_g.reshape(L, 1, X_HEAD_DIM))
    for l in range(L):
        w_in_p, w_uq_p, w_uk_p, w_vt, tabs = _prep_layer(l, positions, w_in, w_uq, w_ukv,
                                                         mla_qn_g, mla_kn_g)
        lat, hc = _in_proj(x, mix_norm_g[l][None], w_in_p)
        q, k, vt = _mla_qkv(lat, mla_q_norm_g[l][None], w_uq_p, mla_kv_norm_g[l][None],
                            w_uk_p, w_vt, tabs)
        score_bound = (math.log2(math.e) * math.sqrt(MLA_QK)
                       * jnp.max(jnp.abs(mla_qn_g[l])) * jnp.max(jnp.abs(mla_kn_g[l])))
        y_mla = lax.cond(score_bound <= ATT_PLAIN_MAX_SCORE,
                         functools.partial(_attention, online=False),
                         functools.partial(_attention, online=True),
                         q, k, vt, n_full, n_kv, posq, posk)
        y_hc = _hgrn_conv(hc, hgrn_lb_logits, hgrn_o_norm_g[l][None], conv_w[l], l)
        x = _out_xattn(x, y_mla, y_hc, w_out[l].astype(BF16), xattn_norm_g[l][None],
                       w_xq[l].astype(BF16), xq_norm_g[l][None], k_mem, v_mem,
                       w_xo[l].astype(BF16), l)
        x = _mlp(x, mlp_norm_g[l][None], w_up[l].astype(BF16), w_down[l].astype(BF16))
    return x
```

```python
import functools
import math

import numpy as np
import jax
import jax.numpy as jnp
from jax import lax
from jax.experimental import pallas as pl
from jax.experimental.pallas import tpu as pltpu

F32 = jnp.float32
BF16 = jnp.bfloat16

D_MODEL = 1024
DEPTH = 4
MEM_LEN = 256
EPS = 1e-6

MLA_HEADS = 8
MLA_NOPE = 64
MLA_ROPE = 32
MLA_V = 64
MLA_QK = MLA_NOPE + MLA_ROPE
MLA_Q_RANK = 384
MLA_KV_RANK = 256
ROPE_BASE = 10000.0

HG_HEADS = 4
HG_DK = 64
HG_CHUNK = 64
HG_SUB = 16
HG_GROUP = 8
HG_WIDTH = 256
CONV_WIDTH = 256
CONV_K = 3

X_HEADS = 4
X_HEAD_DIM = 128
D_FF = 4 * D_MODEL

LANES = 128
HEAD_SLOT = LANES
ROPE_HALF = MLA_ROPE // 2
N_IN_PAD = MLA_Q_RANK + MLA_KV_RANK + HEAD_SLOT + 4 * HG_WIDTH + 3 * CONV_WIDTH
HC_OFF = MLA_Q_RANK + MLA_KV_RANK + HEAD_SLOT
HC_WIDTH = N_IN_PAD - HC_OFF

ROW_TILE = 512
XATTN_SUB = 256
ATT_TQ = 256
ATT_TK = 256
ATT_HEADS = 8
ATT_CHAINS = 4
ATT_UNROLLS = (4, 2)
HG_ROWS = 512
MASK_VALUE = -1e30
ATT_PLAIN_MAX_SCORE = 64.0
EXP_CLAMP = 60.0
VMEM_LIMIT = 56 * 1024 * 1024


def _head_lane_map():
    m = -np.ones((HEAD_SLOT,), np.int64)
    m[0:ROPE_HALF] = MLA_NOPE + np.arange(ROPE_HALF)
    m[ROPE_HALF:64] = np.arange(64 - ROPE_HALF)
    m[64:64 + ROPE_HALF] = MLA_NOPE + ROPE_HALF + np.arange(ROPE_HALF)
    m[64 + ROPE_HALF:96] = (64 - ROPE_HALF) + np.arange(ROPE_HALF)
    return m


_LANE_MAP = _head_lane_map()


def _rms(x, n=None):
    n = x.shape[-1] if n is None else n
    return lax.rsqrt(jnp.sum(x * x, axis=-1, keepdims=True) * (1.0 / n) + EPS)


def _sigmoid(x):
    return 1.0 / (1.0 + jnp.exp(-x))


def _dot(a, b):
    return jnp.dot(a, b, preferred_element_type=F32)


def _dot_nt(a, b):
    return lax.dot_general(a, b, (((1,), (1,)), ((), ())), preferred_element_type=F32)


def _dot_tn(a, b):
    return lax.dot_general(a, b, (((0,), (0,)), ((), ())), preferred_element_type=F32)


def _in_proj_kernel(x_ref, g_ref, w_in_ref, lbl_ref, lat_out, rec_out, post_out, *, layer):
    x = x_ref[0]
    h = (x * _rms(x) * g_ref[...]).astype(BF16)
    lat_out[0] = _dot(h, w_in_ref[:, :HC_OFF])
    hc = _dot(h, w_in_ref[:, HC_OFF:])
    W = HG_WIDTH
    zq, zf, vi, zg = (hc[:, i * W:(i + 1) * W] for i in range(4))
    cb, cc, cx = (hc[:, 4 * W + i * CONV_WIDTH:4 * W + (i + 1) * CONV_WIDTH] for i in range(3))

    lg = lbl_ref[...]
    e = jnp.exp(lg - jnp.max(lg, axis=0, keepdims=True))
    soft = e / jnp.sum(e, axis=0, keepdims=True)
    lb = jnp.zeros((1, W), F32)
    for i in range(1, layer + 1):
        lb = lb + soft[i:i + 1, :]
    lb = jnp.maximum(lb, 0.0)
    log_lb = jnp.log(lb)
    log_1m = jnp.log1p(-lb)

    t = jnp.exp(-jnp.abs(zf))
    d = 1.0 + t
    r = 1.0 / d
    log_sig = jnp.minimum(zf, 0.0) - jnp.log(d)
    c2 = log_1m + log_sig
    log_f = jnp.maximum(log_lb, c2) + jnp.log(1.0 + jnp.exp(-jnp.abs(log_lb - c2)))
    rec_out[0, :, 0:W] = zq * _sigmoid(zq)
    rec_out[0, :, W:2 * W] = log_f
    rec_out[0, :, 2 * W:3 * W] = (1.0 - lb) * jnp.where(zf >= 0.0, t * r, r)
    rec_out[0, :, 3 * W:4 * W] = vi
    post_out[0, :, 0:W] = zg * _sigmoid(zg)
    post_out[0, :, W:W + CONV_WIDTH] = cb
    post_out[0, :, W + CONV_WIDTH:W + 2 * CONV_WIDTH] = cc * cx


def _in_proj(x, g, w_in, lb_logits, layer):
    B, S, _ = x.shape
    tm = ROW_TILE
    const = lambda b, s: (0, 0)
    widths = (HC_OFF, 4 * HG_WIDTH, HG_WIDTH + 2 * CONV_WIDTH)
    return pl.pallas_call(
        functools.partial(_in_proj_kernel, layer=layer),
        grid=(B, S // tm),
        in_specs=[
            pl.BlockSpec((1, tm, D_MODEL), lambda b, s: (b, s, 0)),
            pl.BlockSpec((1, D_MODEL), const),
            pl.BlockSpec((D_MODEL, N_IN_PAD), const),
            pl.BlockSpec((DEPTH, HG_WIDTH), const),
        ],
        out_specs=[pl.BlockSpec((1, tm, w), lambda b, s: (b, s, 0)) for w in widths],
        out_shape=[jax.ShapeDtypeStruct((B, S, w), F32) for w in widths],
        compiler_params=pltpu.CompilerParams(
            dimension_semantics=("parallel", "parallel"), vmem_limit_bytes=VMEM_LIMIT),
        name="in_proj",
    )(x, g, w_in, lb_logits)


def _mla_qkv_kernel(lat_ref, gq_ref, w_uq_ref, gkv_ref, w_uk_ref, w_vt_ref,
                    cq_tab, sq_tab, ck_tab, sk_tab, q_out, k_out, vt_out):
    cq = lat_ref[0, :, :MLA_Q_RANK]
    ckv = lat_ref[0, :, MLA_Q_RANK:MLA_Q_RANK + MLA_KV_RANK]
    kr = lat_ref[0, :, MLA_Q_RANK + MLA_KV_RANK:]
    cqn = (cq * _rms(cq) * gq_ref[...]).astype(BF16)
    ckvn = (ckv * _rms(ckv) * gkv_ref[...]).astype(BF16)
    hw = MLA_HEADS * HEAD_SLOT
    qf = _dot(cqn, w_uq_ref[...])
    kf = _dot(ckvn, w_uk_ref[...])
    vt = _dot_nt(w_vt_ref[...], ckvn).astype(BF16)
    for j in range(vt_out.shape[1]):
        vt_out[0, j] = vt[:, j * ATT_TK:(j + 1) * ATT_TK]
    kr_sw = pltpu.roll(kr, HEAD_SLOT // 2, 1)
    cq_t, sq_t, ck_t, sk_t = cq_tab[...], sq_tab[...], ck_tab[...], sk_tab[...]
    for hd in range(MLA_HEADS):
        sl = slice(hd * HEAD_SLOT, (hd + 1) * HEAD_SLOT)
        sw = slice(hw + hd * HEAD_SLOT, hw + (hd + 1) * HEAD_SLOT)
        qh = qf[:, sl]
        qh = (qh * cq_t + qf[:, sw] * sq_t) * _rms(qh, MLA_QK)
        q_out[0, hd] = qh.T.astype(BF16)
        kh = kf[:, sl] + kr
        kh = (kh * ck_t + (kf[:, sw] + kr_sw) * sk_t) * _rms(kh, MLA_QK)
        k_out[0, hd] = kh.astype(BF16)


def _mla_qkv(lat, gq, w_uq, gkv, w_uk, w_vt, tabs):
    B, S, _ = lat.shape
    tm = ROW_TILE
    vw = MLA_HEADS * MLA_V
    const = lambda b, s: (0, 0)
    tab_spec = pl.BlockSpec((tm, HEAD_SLOT), lambda b, s: (s, 0))
    return pl.pallas_call(
        _mla_qkv_kernel,
        grid=(B, S // tm),
        in_specs=[
            pl.BlockSpec((1, tm, HC_OFF), lambda b, s: (b, s, 0)),
            pl.BlockSpec((1, MLA_Q_RANK), const),
            pl.BlockSpec((MLA_Q_RANK, 2 * MLA_HEADS * HEAD_SLOT), const),
            pl.BlockSpec((1, MLA_KV_RANK), const),
            pl.BlockSpec((MLA_KV_RANK, 2 * MLA_HEADS * HEAD_SLOT), const),
            pl.BlockSpec((vw, MLA_KV_RANK), const),
            tab_spec, tab_spec, tab_spec, tab_spec,
        ],
        out_specs=[
            pl.BlockSpec((1, MLA_HEADS, HEAD_SLOT, tm), lambda b, s: (b, 0, 0, s)),
            pl.BlockSpec((1, MLA_HEADS, tm, HEAD_SLOT), lambda b, s: (b, 0, s, 0)),
            pl.BlockSpec((1, tm // ATT_TK, vw, ATT_TK), lambda b, s: (b, s, 0, 0)),
        ],
        out_shape=[
            jax.ShapeDtypeStruct((B, MLA_HEADS, HEAD_SLOT, S), BF16),
            jax.ShapeDtypeStruct((B, MLA_HEADS, S, HEAD_SLOT), BF16),
            jax.ShapeDtypeStruct((B, S // ATT_TK, vw, ATT_TK), BF16),
        ],
        compiler_params=pltpu.CompilerParams(
            dimension_semantics=("parallel", "parallel"), vmem_limit_bytes=VMEM_LIMIT),
        name="mla_qkv",
    )(lat, gq, w_uq, gkv, w_uk, w_vt, *tabs)


def _attn_kernel(nfull_ref, nkv_ref, q_ref, k_ref, vt_ref, posq_ref, posk_ref, o_ref,
                 p_sc, acc_sc, *maybe_s_sc, online):
    qi = pl.program_id(2)
    n_full = nfull_ref[qi]
    n_kv = nkv_ref[qi]
    tq = q_ref.shape[3]
    nk = vt_ref.shape[1]
    heads = range(ATT_HEADS)
    qts = [q_ref[0, hd] for hd in heads]
    posq = posq_ref[0]

    groups = [list(heads)[i:i + ATT_CHAINS] for i in range(0, ATT_HEADS, ATT_CHAINS)]

    def score(ki, hd):
        off = pl.multiple_of(ki * ATT_TK, ATT_TK)
        return _dot(k_ref[0, hd, pl.ds(off, ATT_TK), :], qts[hd])

    def pv(ki, hd, p=None):
        p = p_sc[hd] if p is None else p
        return _dot(vt_ref[0, ki, hd * MLA_V:(hd + 1) * MLA_V, :], p)

    def issue(ki_scores, ki_pv):
        ss, pvs = {}, {}
        for g in groups:
            for hd in g:
                ss[hd] = score(ki_scores, hd)
            for hd in g:
                pvs[hd] = pv(ki_pv, hd)
        return ss, pvs

    def mask_of(ki):
        off = pl.multiple_of(ki * ATT_TK, ATT_TK)
        return posk_ref[pl.ds(off, ATT_TK), :] <= posq

    def step_plain(ki, carry, masked):
        ss, pvs = issue(ki, jnp.maximum(ki - 1, 0))
        if masked:
            mask = mask_of(ki)
        new = []
        for hd in heads:
            s = jnp.where(mask, ss[hd], MASK_VALUE) if masked else ss[hd]
            p = jnp.exp2(s)
            new.append(carry[hd] + jnp.sum(p, axis=0, keepdims=True))
            acc_sc[hd] = acc_sc[hd] + pvs[hd]
            p_sc[hd] = p.astype(BF16)
        return tuple(new)

    def multi_plain(j, carry, n_tiles, first):
        k0 = first + n_tiles * j
        ss, pvs = issue(k0, jnp.maximum(k0 - 1, 0))
        l_add = {hd: 0.0 for hd in heads}
        for hd in heads:
            acc_sc[hd] = acc_sc[hd] + pvs[hd]
        for u in range(1, n_tiles):
            ss_next = {}
            for g in groups:
                for hd in g:
                    ss_next[hd] = score(k0 + u, hd)
                for hd in g:
                    p = jnp.exp2(ss[hd])
                    l_add[hd] = l_add[hd] + jnp.sum(p, axis=0, keepdims=True)
                    acc_sc[hd] = acc_sc[hd] + pv(k0 + u - 1, hd, p.astype(BF16))
            ss = ss_next
        new = []
        for hd in heads:
            p = jnp.exp2(ss[hd])
            new.append(carry[hd] + l_add[hd] + jnp.sum(p, axis=0, keepdims=True))
            p_sc[hd] = p.astype(BF16)
        return tuple(new)

    def step_online(ki, carry, masked):
        (s_sc,) = maybe_s_sc
        s_next, pvs = issue(jnp.minimum(ki + 1, nk - 1), jnp.maximum(ki - 1, 0))
        if masked:
            mask = mask_of(ki)
        new = []
        for hd in heads:
            m_old, l_old = carry[hd]
            s = s_sc[hd]
            if masked:
                s = jnp.where(mask, s, MASK_VALUE)
            m_new = jnp.maximum(m_old, jnp.max(s, axis=0, keepdims=True))
            alpha = jnp.exp2(m_old - m_new)
            p = jnp.exp2(s - m_new)
            l_new = alpha * l_old + jnp.sum(p, axis=0, keepdims=True)
            acc_sc[hd] = alpha * (acc_sc[hd] + pvs[hd])
            p_sc[hd] = p.astype(BF16)
            new.append((m_new, l_new))
        for hd in heads:
            s_sc[hd] = s_next[hd]
        return tuple(new)

    p_sc[...] = jnp.zeros(p_sc.shape, BF16)
    acc_sc[...] = jnp.zeros(acc_sc.shape, F32)
    if online:
        for hd in heads:
            maybe_s_sc[0][hd] = score(0, hd)
        init = tuple((jnp.full((1, tq), MASK_VALUE, F32), jnp.zeros((1, tq), F32))
                     for _ in heads)
        step = step_online
    else:
        init = tuple(jnp.zeros((1, tq), F32) for _ in heads)
        step = step_plain
    n_done = 0
    if not online:
        for n_tiles in ATT_UNROLLS:
            n_steps = (n_full - n_done) // n_tiles
            init = lax.fori_loop(
                0, n_steps, functools.partial(multi_plain, n_tiles=n_tiles, first=n_done), init)
            n_done = n_done + n_tiles * n_steps
    carry = lax.fori_loop(n_done, n_full, functools.partial(step, masked=False), init)
    carry = lax.fori_loop(n_full, n_kv, functools.partial(step, masked=True), carry)
    last = jnp.maximum(n_kv - 1, 0)
    ls = [c[1] for c in carry] if online else carry
    o_t = jnp.concatenate([(acc_sc[hd] + pv(last, hd)) / ls[hd] for hd in heads], axis=0)
    o_ref[0] = o_t.T.astype(BF16)


def _attention(q, k, vt, n_full, n_kv, posq, posk, online):
    B, H, _, S = q.shape
    tq = ATT_TQ
    nh = ATT_HEADS
    nk = S // ATT_TK
    scratch = [pltpu.VMEM((nh, ATT_TK, tq), BF16), pltpu.VMEM((nh, MLA_V, tq), F32)]
    if online:
        scratch.append(pltpu.VMEM((nh, ATT_TK, tq), F32))
    grid_spec = pltpu.PrefetchScalarGridSpec(
        num_scalar_prefetch=2,
        grid=(B, H // nh, S // tq),
        in_specs=[
            pl.BlockSpec((1, nh, HEAD_SLOT, tq), lambda b, hp, qi, *_: (b, hp, 0, qi)),
            pl.BlockSpec((1, nh, S, HEAD_SLOT), lambda b, hp, qi, *_: (b, hp, 0, 0)),
            pl.BlockSpec((1, nk, nh * MLA_V, ATT_TK), lambda b, hp, qi, *_: (b, 0, hp, 0)),
            pl.BlockSpec((1, 1, tq), lambda b, hp, qi, *_: (qi, 0, 0)),
            pl.BlockSpec((S, 1), lambda b, hp, qi, *_: (0, 0)),
        ],
        out_specs=pl.BlockSpec((1, tq, nh * MLA_V), lambda b, hp, qi, *_: (b, qi, hp)),
        scratch_shapes=scratch,
    )
    return pl.pallas_call(
        functools.partial(_attn_kernel, online=online),
        grid_spec=grid_spec,
        out_shape=jax.ShapeDtypeStruct((B, S, H * MLA_V), BF16),
        compiler_params=pltpu.CompilerParams(
            dimension_semantics=("parallel", "parallel", "arbitrary"),
            vmem_limit_bytes=VMEM_LIMIT),
        name="mla_attention_online" if online else "mla_attention",
    )(n_full, n_kv, q, k, vt, posq, posk)


def _hgrn_kernel(rec_ref, o_ref, st_sc):
    sb = pl.program_id(1)
    R = HG_ROWS
    C = HG_CHUNK
    W = HG_WIDTH

    @pl.when(sb == 0)
    def _():
        st_sc[...] = jnp.zeros(st_sc.shape, F32)

    ri = lax.broadcasted_iota(jnp.int32, (C, C), 0)
    ci = lax.broadcasted_iota(jnp.int32, (C, C), 1)
    sub_shift = HG_SUB.bit_length() - 1
    dk_shift = HG_DK.bit_length() - 1
    tri = jnp.where(ci <= ri, 1.0, 0.0)
    tri_blk = jnp.where((ci >> sub_shift) == (ri >> sub_shift), tri, 0.0)
    cum_mat = jnp.concatenate([tri, tri_blk], axis=0).astype(BF16)
    tt = ri & (HG_SUB - 1)
    lane_w = lax.broadcasted_iota(jnp.int32, (1, W), 1)
    head_masks = [jnp.where((lane_w >> dk_shift) == hh, 1.0, 0.0) for hh in range(HG_HEADS)]
    rw = lax.broadcasted_iota(jnp.int32, (W, W), 0)
    cw = lax.broadcasted_iota(jnp.int32, (W, W), 1)
    bd_mask = (rw >> dk_shift) == (cw >> dk_shift)
    ones_bd = jnp.where(bd_mask, 1.0, 0.0).astype(BF16)
    n_sub = C // HG_SUB

    def group(gi, carry):
        base = gi * (HG_GROUP * C)
        starts = [pl.multiple_of(base + j * C, C) for j in range(HG_GROUP)]
        chunks = range(HG_GROUP)
        gates, splits = [], []
        for r0 in starts:
            q = rec_ref[0, pl.ds(r0, C), 0:W]
            log_f = rec_ref[0, pl.ds(r0, C), W:2 * W]
            kk = rec_ref[0, pl.ds(r0, C), 2 * W:3 * W]
            vi = rec_ref[0, pl.ds(r0, C), 3 * W:4 * W]
            f_hi = log_f.astype(BF16)
            r1 = log_f - f_hi.astype(F32)
            f_mid = r1.astype(BF16)
            f_lo = (r1 - f_mid.astype(F32)).astype(BF16)
            gates.append((q, kk, vi.astype(BF16)))
            splits.append((f_hi, f_mid, f_lo))
        cums = [_dot(cum_mat, s[0]) + _dot(cum_mat, s[1]) + _dot(cum_mat, s[2]) for s in splits]

        def diag_direct(j, i):
            q, kk, vb = gates[j]
            rows = slice(i * HG_SUB, (i + 1) * HG_SUB)
            b_t, q_t, k_t = cums[j][:C][rows], q[rows], kk[rows]
            v_t = vb[rows].astype(F32)
            t_idx = lax.broadcasted_iota(jnp.int32, (HG_SUB, W), 0)
            slabs = []
            for s in range(HG_SUB):
                w = jnp.exp(jnp.minimum(b_t - b_t[s:s + 1, :], 0.0))
                slabs.append(jnp.where(t_idx >= s, q_t * k_t[s:s + 1, :] * w, 0.0))
            x = jnp.concatenate(slabs, axis=0).astype(BF16)
            a = _dot(x, ones_bd)
            o = a[0:HG_SUB] * v_t[0:1, :]
            for s in range(1, HG_SUB):
                o = o + a[s * HG_SUB:(s + 1) * HG_SUB] * v_t[s:s + 1, :]
            return o

        def finish(direct_diag):
            qbs, b_lasts, lhss, kds, kdls = [], [], [], [], []
            for j in chunks:
                q, kk, _ = gates[j]
                b = cums[j][:C]
                g = cums[j][C:]
                b_last = b[C - 1:C, :]
                rref = b - g
                qd = q * jnp.exp(g)
                qbs.append((q * jnp.exp(b)).astype(BF16))
                b_lasts.append(b_last)
                kdls.append((kk * jnp.exp(b_last - b)).astype(BF16))
                lhs_j, kd_j = [], []
                for i in range(n_sub):
                    rr = rref[i * HG_SUB:i * HG_SUB + 1, :]
                    kd_j.append((kk * jnp.exp(jnp.minimum(rr - b, EXP_CLAMP))).astype(BF16))
                    qi = qd[i * HG_SUB:(i + 1) * HG_SUB]
                    lhs_j.append(jnp.concatenate([qi * hm for hm in head_masks],
                                                 axis=0).astype(BF16))
                lhss.append(lhs_j)
                kds.append(kd_j)
            attn = [[_dot_nt(lhss[j][i], kds[j][i]) for i in range(n_sub)] for j in chunks]
            upds = [_dot_tn(gates[j][2], kdls[j]) for j in chunks]
            if direct_diag:
                attn = [[jnp.where(ci < i * HG_SUB, attn[j][i], 0.0).astype(BF16)
                         for i in range(n_sub)] for j in chunks]
            else:
                attn = [[jnp.where(ci <= i * HG_SUB + tt, attn[j][i], 0.0).astype(BF16)
                         for i in range(n_sub)] for j in chunks]
            pvs = [[_dot(attn[j][i], gates[j][2]) for i in range(n_sub)] for j in chunks]
            st = st_sc[...]
            for j in chunks:
                o_parts = []
                for i in range(n_sub):
                    pv = pvs[j][i]
                    oi = pv[0:HG_SUB] * head_masks[0]
                    for hh in range(1, HG_HEADS):
                        oi = oi + pv[hh * HG_SUB:(hh + 1) * HG_SUB] * head_masks[hh]
                    if direct_diag:
                        oi = oi + diag_direct(j, i)
                    o_parts.append(oi)
                o_inter = _dot_nt(qbs[j], st.astype(BF16))
                o_ref[0, pl.ds(starts[j], C), :] = o_inter + jnp.concatenate(o_parts, axis=0)
                st = st * jnp.exp(b_lasts[j]) + jnp.where(bd_mask, upds[j], 0.0)
            st_sc[...] = st
            return jnp.int32(0)

        g_min = cums[0][C:]
        for j in range(1, HG_GROUP):
            g_min = jnp.minimum(g_min, cums[j][C:])
        lax.cond(jnp.min(g_min) < -EXP_CLAMP,
                 functools.partial(finish, True), functools.partial(finish, False))
        return carry

    lax.fori_loop(0, R // (C * HG_GROUP), group, 0)


def _hgrn(rec):
    B, S, _ = rec.shape
    R = HG_ROWS
    return pl.pallas_call(
        _hgrn_kernel,
        grid=(B, S // R),
        in_specs=[pl.BlockSpec((1, R, 4 * HG_WIDTH), lambda b, s: (b, s, 0))],
        out_specs=pl.BlockSpec((1, R, HG_WIDTH), lambda b, s: (b, s, 0)),
        out_shape=jax.ShapeDtypeStruct((B, S, HG_WIDTH), F32),
        scratch_shapes=[pltpu.VMEM((HG_WIDTH, HG_WIDTH), F32)],
        compiler_params=pltpu.CompilerParams(
            dimension_semantics=("parallel", "arbitrary"), vmem_limit_bytes=VMEM_LIMIT),
        name="hgrn",
    )(rec)


def _mem_kv_kernel(mem_ref, g_ref, w_ref, gk_ref, k_out, v_out):
    m = mem_ref[0]
    mn = m * _rms(m) * g_ref[0]
    kv = _dot(mn.astype(BF16), w_ref[0])
    hw = X_HEADS * X_HEAD_DIM
    gk = gk_ref[0]
    for hd in range(X_HEADS):
        sl = slice(hd * X_HEAD_DIM, (hd + 1) * X_HEAD_DIM)
        kh = kv[:, sl]
        k_out[0, 0, :, sl] = (kh * _rms(kh) * gk).astype(BF16)
    v_out[0, 0] = kv[:, hw:].astype(BF16)


def _mem_kv(mem, mem_norm_g, w_xkv, xk_norm_g):
    B, M, _ = mem.shape
    L = w_xkv.shape[0]
    hw = X_HEADS * X_HEAD_DIM
    out_spec = pl.BlockSpec((1, 1, M, hw), lambda l, b: (l, b, 0, 0))
    return pl.pallas_call(
        _mem_kv_kernel,
        grid=(L, B),
        in_specs=[
            pl.BlockSpec((1, M, D_MODEL), lambda l, b: (b, 0, 0)),
            pl.BlockSpec((1, 1, D_MODEL), lambda l, b: (l, 0, 0)),
            pl.BlockSpec((1, D_MODEL, 2 * hw), lambda l, b: (l, 0, 0)),
            pl.BlockSpec((1, 1, X_HEAD_DIM), lambda l, b: (l, 0, 0)),
        ],
        out_specs=[out_spec, out_spec],
        out_shape=[jax.ShapeDtypeStruct((L, B, M, hw), BF16)] * 2,
        compiler_params=pltpu.CompilerParams(
            dimension_semantics=("parallel", "parallel"), vmem_limit_bytes=VMEM_LIMIT),
        name="mem_kv",
    )(mem, mem_norm_g, w_xkv, xk_norm_g)


def _out_xattn_kernel(x_ref, ya_ref, rec_ref, post_ref, onorm_ref, convw_ref, wo_ref, g_ref,
                      wq_ref, gq_ref, k_ref, v_ref, wxo_ref, o_ref, ubuf_sc):
    half = wo_ref.shape[0] // 2
    tm = x_ref.shape[1]
    W = HG_WIDTH
    subs = [slice(j * XATTN_SUB, (j + 1) * XATTN_SUB) for j in range(tm // XATTN_SUB)]

    rw = lax.broadcasted_iota(jnp.int32, (W, W), 0)
    cw = lax.broadcasted_iota(jnp.int32, (W, W), 1)
    dk_shift = HG_DK.bit_length() - 1
    ones_bd = jnp.where((rw >> dk_shift) == (cw >> dk_shift), 1.0, 0.0).astype(BF16)
    @pl.when(pl.program_id(1) == 0)
    def _():
        ubuf_sc[0:8, :] = jnp.zeros((8, CONV_WIDTH), F32)

    ubuf_sc[8:8 + tm, :] = post_ref[0, :, W + CONV_WIDTH:W + 2 * CONV_WIDTH]
    wc = convw_ref[...]
    ybs = []
    for r in subs:
        o = rec_ref[0, r, :]
        ms = _dot((o * o).astype(BF16), ones_bd) * (1.0 / HG_DK)
        y_hg = o * lax.rsqrt(ms + EPS) * onorm_ref[...] * post_ref[0, r, 0:W]
        u0, u1, u2 = (ubuf_sc[8 - d + r.start:8 - d + r.stop, :] for d in range(CONV_K))
        y_cv = post_ref[0, r, W:W + CONV_WIDTH] * (
            u2 * wc[0:1, :] + u1 * wc[1:2, :] + u0 * wc[2:3, :])
        ybs.append(jnp.concatenate([y_hg, y_cv], axis=1).astype(BF16))
    ubuf_sc[0:8, :] = ubuf_sc[tm:tm + 8, :]

    slots = [slice(hd * X_HEAD_DIM, (hd + 1) * X_HEAD_DIM) for hd in range(X_HEADS)]
    gq = gq_ref[...] * (1.0 / math.sqrt(X_HEAD_DIM))
    x1s = [x_ref[0, r, :] + _dot(ya_ref[0, r, :], wo_ref[:half, :])
           + _dot(yb, wo_ref[half:, :]) for r, yb in zip(subs, ybs)]
    hs = [(x1 * _rms(x1) * g_ref[...]).astype(BF16) for x1 in x1s]
    qs = [_dot(h, wq_ref[...]) for h in hs]
    qhs = [[(q[:, sl] * _rms(q[:, sl]) * gq).astype(BF16) for sl in slots] for q in qs]
    ss = [[_dot_nt(qh, k_ref[0, 0, :, sl]) for qh, sl in zip(row, slots)] for row in qhs]
    ps, ls = [], []
    for row in ss:
        p_row, l_row = [], []
        for s in row:
            p = jnp.exp(s - jnp.max(s, axis=1, keepdims=True))
            l_row.append(jnp.sum(p, axis=1, keepdims=True))
            p_row.append(p.astype(BF16))
        ps.append(p_row)
        ls.append(l_row)
    os_ = [jnp.concatenate([_dot(p, v_ref[0, 0, :, sl]) / l
                            for p, l, sl in zip(p_row, l_row, slots)], axis=1).astype(BF16)
           for p_row, l_row in zip(ps, ls)]
    for r, x1, o in zip(subs, x1s, os_):
        o_ref[0, r, :] = x1 + _dot(o, wxo_ref[...])


def _out_xattn(x, y_mla, o_rec, post, onorm_g, conv_w, w_out, g, w_xq, gq, k_mem, v_mem, w_xo,
               layer):
    B, S, _ = x.shape
    tm = ROW_TILE
    hw = X_HEADS * X_HEAD_DIM
    const = lambda b, s: (0, 0)
    row = lambda w: pl.BlockSpec((1, tm, w), lambda b, s: (b, s, 0))
    mem_spec = pl.BlockSpec((1, 1, MEM_LEN, hw), lambda b, s: (layer, b, 0, 0))
    return pl.pallas_call(
        _out_xattn_kernel,
        grid=(B, S // tm),
        in_specs=[
            row(D_MODEL), row(y_mla.shape[-1]), row(o_rec.shape[-1]), row(post.shape[-1]),
            pl.BlockSpec((1, HG_WIDTH), const),
            pl.BlockSpec((CONV_K, CONV_WIDTH), const),
            pl.BlockSpec((D_MODEL, D_MODEL), const),
            pl.BlockSpec((1, D_MODEL), const),
            pl.BlockSpec((D_MODEL, hw), const),
            pl.BlockSpec((1, X_HEAD_DIM), const),
            mem_spec, mem_spec,
            pl.BlockSpec((hw, D_MODEL), const),
        ],
        out_specs=row(D_MODEL),
        out_shape=jax.ShapeDtypeStruct((B, S, D_MODEL), F32),
        scratch_shapes=[pltpu.VMEM((tm + 8, CONV_WIDTH), F32)],
        compiler_params=pltpu.CompilerParams(
            dimension_semantics=("parallel", "arbitrary"), vmem_limit_bytes=VMEM_LIMIT),
        name="out_xattn",
    )(x, y_mla, o_rec, post, onorm_g, conv_w, w_out, g, w_xq, gq, k_mem, v_mem, w_xo)


def _mlp_kernel(x_ref, g_ref, wu_ref, wd_ref, o_ref):
    x = x_ref[0]
    h = (x * _rms(x) * g_ref[...]).astype(BF16)
    acc = x
    step = D_MODEL
    for c in range(D_FF // step):
        u = _dot(h, wu_ref[:, c * step:(c + 1) * step])
        a = jnp.square(jnp.maximum(u, 0.0)).astype(BF16)
        acc = acc + _dot(a, wd_ref[c * step:(c + 1) * step, :])
    o_ref[0] = acc


def _mlp(x, g, w_up, w_down):
    B, S, _ = x.shape
    tm = ROW_TILE
    const = lambda b, s: (0, 0)
    row = pl.BlockSpec((1, tm, D_MODEL), lambda b, s: (b, s, 0))
    return pl.pallas_call(
        _mlp_kernel,
        grid=(B, S // tm),
        in_specs=[row, pl.BlockSpec((1, D_MODEL), const),
                  pl.BlockSpec((D_MODEL, D_FF), const),
                  pl.BlockSpec((D_FF, D_MODEL), const)],
        out_specs=row,
        out_shape=jax.ShapeDtypeStruct((B, S, D_MODEL), F32),
        compiler_params=pltpu.CompilerParams(
            dimension_semantics=("parallel", "parallel"), vmem_limit_bytes=VMEM_LIMIT),
        name="mlp",
    )(x, g, w_up, w_down)


def _scatter_lanes(w_cols):
    idx = jnp.asarray(np.where(_LANE_MAP >= 0, _LANE_MAP, 0), jnp.int32)
    valid = jnp.asarray(_LANE_MAP >= 0)
    return jnp.where(valid, jnp.take(w_cols, idx, axis=-1), 0.0)


def _prep_layer(l, positions, w_in, w_uq, w_ukv, mla_qn_g, mla_kn_g):
    o_kr = MLA_Q_RANK + MLA_KV_RANK
    w = w_in[l]
    kr_src = jnp.concatenate(
        [jnp.zeros((D_MODEL, MLA_NOPE), F32), w[:, o_kr:o_kr + MLA_ROPE]], axis=1)
    w_in_p = jnp.concatenate(
        [w[:, :o_kr], _scatter_lanes(kr_src), w[:, o_kr + MLA_ROPE:]], axis=1).astype(BF16)

    def with_swapped_halves(w_slots):
        rank = w_slots.shape[0]
        both = jnp.concatenate([w_slots, jnp.roll(w_slots, HEAD_SLOT // 2, axis=-1)], axis=1)
        return both.reshape(rank, 2 * MLA_HEADS * HEAD_SLOT).astype(BF16)

    wq = w_uq[l].reshape(MLA_Q_RANK, MLA_HEADS, MLA_QK)
    w_uq_p = with_swapped_halves(_scatter_lanes(wq))

    wkv = w_ukv[l].reshape(MLA_KV_RANK, MLA_HEADS, MLA_NOPE + MLA_V)
    k_src = jnp.concatenate(
        [wkv[..., :MLA_NOPE], jnp.zeros((MLA_KV_RANK, MLA_HEADS, MLA_ROPE), F32)], axis=-1)
    w_uk_p = with_swapped_halves(_scatter_lanes(k_src))
    w_vt = wkv[..., MLA_NOPE:].reshape(MLA_KV_RANK, MLA_HEADS * MLA_V).T.astype(BF16)

    inv_freq = ROPE_BASE ** (-jnp.arange(0, MLA_ROPE, 2, dtype=F32) / MLA_ROPE)
    ang = positions.astype(F32)[:, None] * inv_freq[None, :]
    cos, sin = jnp.cos(ang), jnp.sin(ang)
    S = positions.shape[0]
    pad = HEAD_SLOT // 2 - ROPE_HALF
    c_tab = jnp.concatenate([cos, jnp.ones((S, pad), F32), cos, jnp.ones((S, pad), F32)], axis=1)
    s_tab = jnp.concatenate([-sin, jnp.zeros((S, pad), F32), sin, jnp.zeros((S, pad), F32)], axis=1)

    def tables(gain, scale):
        g = _scatter_lanes(gain)[None, :] * scale
        return c_tab * g, s_tab * jnp.roll(g, HEAD_SLOT // 2, axis=1)

    cq_t, sq_t = tables(mla_qn_g[l], math.log2(math.e) / math.sqrt(MLA_QK))
    ck_t, sk_t = tables(mla_kn_g[l], 1.0)
    return w_in_p, w_uq_p, w_uk_p, w_vt, (cq_t, sq_t, ck_t, sk_t)


def kernel(x, mem, positions, mix_norm_g, w_in, mla_q_norm_g, mla_kv_norm_g, w_uq, w_ukv,
           mla_qn_g, mla_kn_g, hgrn_lb_logits, hgrn_o_norm_g, conv_w, w_out,
           xattn_norm_g, mem_norm_g, w_xq, w_xkv, xq_norm_g, xk_norm_g, w_xo,
           mlp_norm_g, w_up, w_down):
    B, S, _ = x.shape
    L = w_in.shape[0]
    nq, nk = S // ATT_TQ, S // ATT_TK
    pq = positions.reshape(nq, ATT_TQ)
    pk = positions.reshape(nk, ATT_TK)
    vis = jnp.min(pk, axis=1)[None, :] <= jnp.max(pq, axis=1)[:, None]
    n_kv = jnp.max(jnp.where(vis, jnp.arange(1, nk + 1, dtype=jnp.int32)[None, :], 0),
                   axis=1).astype(jnp.int32)
    full = jnp.max(pk, axis=1)[None, :] <= jnp.min(pq, axis=1)[:, None]
    n_full = jnp.sum(jnp.cumprod(full.astype(jnp.int32), axis=1), axis=1).astype(jnp.int32)
    posq = positions.reshape(nq, 1, ATT_TQ)
    posk = positions.reshape(S, 1)

    k_mem, v_mem = _mem_kv(mem, mem_norm_g.reshape(L, 1, D_MODEL), w_xkv.astype(BF16),
                           xk_norm_g.reshape(L, 1, X_HEAD_DIM))
    for l in range(L):
        w_in_p, w_uq_p, w_uk_p, w_vt, tabs = _prep_layer(l, positions, w_in, w_uq, w_ukv,
                                                         mla_qn_g, mla_kn_g)
        lat, rec, post = _in_proj(x, mix_norm_g[l][None], w_in_p, hgrn_lb_logits, l)
        q, k, vt = _mla_qkv(lat, mla_q_norm_g[l][None], w_uq_p, mla_kv_norm_g[l][None],
                            w_uk_p, w_vt, tabs)
        score_bound = (math.log2(math.e) * math.sqrt(MLA_QK)
                       * jnp.max(jnp.abs(mla_qn_g[l])) * jnp.max(jnp.abs(mla_kn_g[l])))
        y_mla = lax.cond(score_bound <= ATT_PLAIN_MAX_SCORE,
                         functools.partial(_attention, online=False),
                         functools.partial(_attention, online=True),
                         q, k, vt, n_full, n_kv, posq, posk)
        o_rec = _hgrn(rec)
        x = _out_xattn(x, y_mla, o_rec, post, hgrn_o_norm_g[l][None], conv_w[l],
                       w_out[l].astype(BF16), xattn_norm_g[l][None], w_xq[l].astype(BF16),
                       xq_norm_g[l][None], k_mem, v_mem, w_xo[l].astype(BF16), l)
        x = _mlp(x, mlp_norm_g[l][None], w_up[l].astype(BF16), w_down[l].astype(BF16))
    return x
```

```python
import functools
import math

import numpy as np
import jax
import jax.numpy as jnp
from jax import lax
from jax.experimental import pallas as pl
from jax.experimental.pallas import tpu as pltpu

F32 = jnp.float32
BF16 = jnp.bfloat16

D_MODEL = 1024
DEPTH = 4
MEM_LEN = 256
EPS = 1e-6

MLA_HEADS = 8
MLA_NOPE = 64
MLA_ROPE = 32
MLA_V = 64
MLA_QK = MLA_NOPE + MLA_ROPE
MLA_Q_RANK = 384
MLA_KV_RANK = 256
ROPE_BASE = 10000.0

HG_HEADS = 4
HG_DK = 64
HG_CHUNK = 64
HG_SUB = 16
HG_GROUP = 8
HG_WIDTH = 256
CONV_WIDTH = 256
CONV_K = 3

X_HEADS = 4
X_HEAD_DIM = 128
D_FF = 4 * D_MODEL

LANES = 128
HEAD_SLOT = LANES
ROPE_HALF = MLA_ROPE // 2
N_IN_PAD = MLA_Q_RANK + MLA_KV_RANK + HEAD_SLOT + 4 * HG_WIDTH + 3 * CONV_WIDTH
HC_OFF = MLA_Q_RANK + MLA_KV_RANK + HEAD_SLOT
HC_WIDTH = N_IN_PAD - HC_OFF

ROW_TILE = 512
IN_PROJ_SUB = 256
XATTN_SUB = 256
ATT_TQ = 256
ATT_TK = 256
ATT_HEADS = 8
ATT_CHAINS = 4
ATT_UNROLLS = (4, 2)
HG_ROWS = 512
MASK_VALUE = -1e30
ATT_PLAIN_MAX_SCORE = 64.0
EXP_CLAMP = 60.0
VMEM_LIMIT = 56 * 1024 * 1024


def _head_lane_map():
    m = -np.ones((HEAD_SLOT,), np.int64)
    m[0:ROPE_HALF] = MLA_NOPE + np.arange(ROPE_HALF)
    m[ROPE_HALF:64] = np.arange(64 - ROPE_HALF)
    m[64:64 + ROPE_HALF] = MLA_NOPE + ROPE_HALF + np.arange(ROPE_HALF)
    m[64 + ROPE_HALF:96] = (64 - ROPE_HALF) + np.arange(ROPE_HALF)
    return m


_LANE_MAP = _head_lane_map()


def _rms(x, n=None):
    n = x.shape[-1] if n is None else n
    return lax.rsqrt(jnp.sum(x * x, axis=-1, keepdims=True) * (1.0 / n) + EPS)


def _sigmoid(x):
    return 1.0 / (1.0 + jnp.exp(-x))


def _dot(a, b):
    return jnp.dot(a, b, preferred_element_type=F32)


def _dot_nt(a, b):
    return lax.dot_general(a, b, (((1,), (1,)), ((), ())), preferred_element_type=F32)


def _dot_tn(a, b):
    return lax.dot_general(a, b, (((0,), (0,)), ((), ())), preferred_element_type=F32)


def _in_proj_kernel(x_ref, g_ref, w_in_ref, lbl_ref, lat_out, rec_out, post_out, *, layer):
    W = HG_WIDTH
    subs = [slice(j * IN_PROJ_SUB, (j + 1) * IN_PROJ_SUB)
            for j in range(x_ref.shape[1] // IN_PROJ_SUB)]
    hs = []
    for r in subs:
        x = x_ref[0, r, :]
        hs.append((x * _rms(x) * g_ref[...]).astype(BF16))
    hcs = []
    for r, h in zip(subs, hs):
        lat_out[0, r, :] = _dot(h, w_in_ref[:, :HC_OFF])
        hcs.append(_dot(h, w_in_ref[:, HC_OFF:]))

    lg = lbl_ref[...]
    e = jnp.exp(lg - jnp.max(lg, axis=0, keepdims=True))
    soft = e / jnp.sum(e, axis=0, keepdims=True)
    lb = jnp.zeros((1, W), F32)
    for i in range(1, layer + 1):
        lb = lb + soft[i:i + 1, :]
    lb = jnp.maximum(lb, 0.0)
    log_lb = jnp.log(lb)
    log_1m = jnp.log1p(-lb)

    for rows, hc in zip(subs, hcs):
        zq, zf, vi, zg = (hc[:, i * W:(i + 1) * W] for i in range(4))
        cb, cc, cx = (hc[:, 4 * W + i * CONV_WIDTH:4 * W + (i + 1) * CONV_WIDTH]
                      for i in range(3))
        t = jnp.exp(-jnp.abs(zf))
        d = 1.0 + t
        r = 1.0 / d
        log_sig = jnp.minimum(zf, 0.0) - jnp.log(d)
        c2 = log_1m + log_sig
        log_f = jnp.maximum(log_lb, c2) + jnp.log(1.0 + jnp.exp(-jnp.abs(log_lb - c2)))
        rec_out[0, rows, 0:W] = zq * _sigmoid(zq)
        rec_out[0, rows, W:2 * W] = log_f
        rec_out[0, rows, 2 * W:3 * W] = (1.0 - lb) * jnp.where(zf >= 0.0, t * r, r)
        rec_out[0, rows, 3 * W:4 * W] = vi
        post_out[0, rows, 0:W] = zg * _sigmoid(zg)
        post_out[0, rows, W:W + CONV_WIDTH] = cb
        post_out[0, rows, W + CONV_WIDTH:W + 2 * CONV_WIDTH] = cc * cx


def _in_proj(x, g, w_in, lb_logits, layer):
    B, S, _ = x.shape
    tm = ROW_TILE
    const = lambda b, s: (0, 0)
    widths = (HC_OFF, 4 * HG_WIDTH, HG_WIDTH + 2 * CONV_WIDTH)
    return pl.pallas_call(
        functools.partial(_in_proj_kernel, layer=layer),
        grid=(B, S // tm),
        in_specs=[
            pl.BlockSpec((1, tm, D_MODEL), lambda b, s: (b, s, 0)),
            pl.BlockSpec((1, D_MODEL), const),
            pl.BlockSpec((D_MODEL, N_IN_PAD), const),
            pl.BlockSpec((DEPTH, HG_WIDTH), const),
        ],
        out_specs=[pl.BlockSpec((1, tm, w), lambda b, s: (b, s, 0)) for w in widths],
        out_shape=[jax.ShapeDtypeStruct((B, S, w), F32) for w in widths],
        compiler_params=pltpu.CompilerParams(
            dimension_semantics=("parallel", "parallel"), vmem_limit_bytes=VMEM_LIMIT),
        name="in_proj",
    )(x, g, w_in, lb_logits)


def _mla_qkv_kernel(lat_ref, gq_ref, w_uq_ref, gkv_ref, w_uk_ref, w_vt_ref,
                    cq_tab, sq_tab, ck_tab, sk_tab, q_out, k_out, vt_out):
    hw = MLA_HEADS * HEAD_SLOT
    subs = [slice(j * ATT_TK, (j + 1) * ATT_TK) for j in range(vt_out.shape[1])]
    lats = []
    for r in subs:
        cq = lat_ref[0, r, :MLA_Q_RANK]
        ckv = lat_ref[0, r, MLA_Q_RANK:MLA_Q_RANK + MLA_KV_RANK]
        lats.append(((cq * _rms(cq) * gq_ref[...]).astype(BF16),
                     (ckv * _rms(ckv) * gkv_ref[...]).astype(BF16)))
    qfs = [_dot(cqn, w_uq_ref[...]) for cqn, _ in lats]
    kfs = [_dot(ckvn, w_uk_ref[...]) for _, ckvn in lats]
    for j, (_, ckvn) in enumerate(lats):
        vt_out[0, j] = _dot_nt(w_vt_ref[...], ckvn).astype(BF16)
    for r, qf, kf in zip(subs, qfs, kfs):
        kr = lat_ref[0, r, MLA_Q_RANK + MLA_KV_RANK:]
        kr_sw = pltpu.roll(kr, HEAD_SLOT // 2, 1)
        cq_t, sq_t, ck_t, sk_t = cq_tab[r, :], sq_tab[r, :], ck_tab[r, :], sk_tab[r, :]
        for hd in range(MLA_HEADS):
            sl = slice(hd * HEAD_SLOT, (hd + 1) * HEAD_SLOT)
            sw = slice(hw + hd * HEAD_SLOT, hw + (hd + 1) * HEAD_SLOT)
            qh = qf[:, sl]
            qh = (qh * cq_t + qf[:, sw] * sq_t) * _rms(qh, MLA_QK)
            q_out[0, hd, :, r] = qh.T.astype(BF16)
            kh = kf[:, sl] + kr
            kh = (kh * ck_t + (kf[:, sw] + kr_sw) * sk_t) * _rms(kh, MLA_QK)
            k_out[0, hd, r, :] = kh.astype(BF16)


def _mla_qkv(lat, gq, w_uq, gkv, w_uk, w_vt, tabs):
    B, S, _ = lat.shape
    tm = ROW_TILE
    vw = MLA_HEADS * MLA_V
    const = lambda b, s: (0, 0)
    tab_spec = pl.BlockSpec((tm, HEAD_SLOT), lambda b, s: (s, 0))
    return pl.pallas_call(
        _mla_qkv_kernel,
        grid=(B, S // tm),
        in_specs=[
            pl.BlockSpec((1, tm, HC_OFF), lambda b, s: (b, s, 0)),
            pl.BlockSpec((1, MLA_Q_RANK), const),
            pl.BlockSpec((MLA_Q_RANK, 2 * MLA_HEADS * HEAD_SLOT), const),
            pl.BlockSpec((1, MLA_KV_RANK), const),
            pl.BlockSpec((MLA_KV_RANK, 2 * MLA_HEADS * HEAD_SLOT), const),
            pl.BlockSpec((vw, MLA_KV_RANK), const),
            tab_spec, tab_spec, tab_spec, tab_spec,
        ],
        out_specs=[
            pl.BlockSpec((1, MLA_HEADS, HEAD_SLOT, tm), lambda b, s: (b, 0, 0, s)),
            pl.BlockSpec((1, MLA_HEADS, tm, HEAD_SLOT), lambda b, s: (b, 0, s, 0)),
            pl.BlockSpec((1, tm // ATT_TK, vw, ATT_TK), lambda b, s: (b, s, 0, 0)),
        ],
        out_shape=[
            jax.ShapeDtypeStruct((B, MLA_HEADS, HEAD_SLOT, S), BF16),
            jax.ShapeDtypeStruct((B, MLA_HEADS, S, HEAD_SLOT), BF16),
            jax.ShapeDtypeStruct((B, S // ATT_TK, vw, ATT_TK), BF16),
        ],
        compiler_params=pltpu.CompilerParams(
            dimension_semantics=("parallel", "parallel"), vmem_limit_bytes=VMEM_LIMIT),
        name="mla_qkv",
    )(lat, gq, w_uq, gkv, w_uk, w_vt, *tabs)


def _attn_kernel(nfull_ref, nkv_ref, q_ref, k_ref, vt_ref, posq_ref, posk_ref, o_ref,
                 p_sc, acc_sc, *maybe_s_sc, online):
    qi = pl.program_id(2)
    n_full = nfull_ref[qi]
    n_kv = nkv_ref[qi]
    tq = q_ref.shape[3]
    nk = vt_ref.shape[1]
    heads = range(ATT_HEADS)
    qts = [q_ref[0, hd] for hd in heads]
    posq = posq_ref[0]

    groups = [list(heads)[i:i + ATT_CHAINS] for i in range(0, ATT_HEADS, ATT_CHAINS)]

    def score(ki, hd):
        off = pl.multiple_of(ki * ATT_TK, ATT_TK)
        return _dot(k_ref[0, hd, pl.ds(off, ATT_TK), :], qts[hd])

    def pv(ki, hd, p=None):
        p = p_sc[hd] if p is None else p
        return _dot(vt_ref[0, ki, hd * MLA_V:(hd + 1) * MLA_V, :], p)

    def issue(ki_scores, ki_pv):
        ss, pvs = {}, {}
        for g in groups:
            for hd in g:
                ss[hd] = score(ki_scores, hd)
            for hd in g:
                pvs[hd] = pv(ki_pv, hd)
        return ss, pvs

    def mask_of(ki):
        off = pl.multiple_of(ki * ATT_TK, ATT_TK)
        return posk_ref[pl.ds(off, ATT_TK), :] <= posq

    def step_plain(ki, carry, masked):
        ss, pvs = issue(ki, jnp.maximum(ki - 1, 0))
        if masked:
            mask = mask_of(ki)
        new = []
        for hd in heads:
            s = jnp.where(mask, ss[hd], MASK_VALUE) if masked else ss[hd]
            p = jnp.exp2(s)
            new.append(carry[hd] + jnp.sum(p, axis=0, keepdims=True))
            acc_sc[hd] = acc_sc[hd] + pvs[hd]
            p_sc[hd] = p.astype(BF16)
        return tuple(new)

    def multi_plain(j, carry, n_tiles, first):
        k0 = first + n_tiles * j
        ss, pvs = issue(k0, jnp.maximum(k0 - 1, 0))
        l_add = {hd: 0.0 for hd in heads}
        for hd in heads:
            acc_sc[hd] = acc_sc[hd] + pvs[hd]
        for u in range(1, n_tiles):
            ss_next = {}
            for g in groups:
                for hd in g:
                    ss_next[hd] = score(k0 + u, hd)
                for hd in g:
                    p = jnp.exp2(ss[hd])
                    l_add[hd] = l_add[hd] + jnp.sum(p, axis=0, keepdims=True)
                    acc_sc[hd] = acc_sc[hd] + pv(k0 + u - 1, hd, p.astype(BF16))
            ss = ss_next
        new = []
        for hd in heads:
            p = jnp.exp2(ss[hd])
            new.append(carry[hd] + l_add[hd] + jnp.sum(p, axis=0, keepdims=True))
            p_sc[hd] = p.astype(BF16)
        return tuple(new)

    def step_online(ki, carry, masked):
        (s_sc,) = maybe_s_sc
        s_next, pvs = issue(jnp.minimum(ki + 1, nk - 1), jnp.maximum(ki - 1, 0))
        if masked:
            mask = mask_of(ki)
        new = []
        for hd in heads:
            m_old, l_old = carry[hd]
            s = s_sc[hd]
            if masked:
                s = jnp.where(mask, s, MASK_VALUE)
            m_new = jnp.maximum(m_old, jnp.max(s, axis=0, keepdims=True))
            alpha = jnp.exp2(m_old - m_new)
            p = jnp.exp2(s - m_new)
            l_new = alpha * l_old + jnp.sum(p, axis=0, keepdims=True)
            acc_sc[hd] = alpha * (acc_sc[hd] + pvs[hd])
            p_sc[hd] = p.astype(BF16)
            new.append((m_new, l_new))
        for hd in heads:
            s_sc[hd] = s_next[hd]
        return tuple(new)

    p_sc[...] = jnp.zeros(p_sc.shape, BF16)
    acc_sc[...] = jnp.zeros(acc_sc.shape, F32)
    if online:
        for hd in heads:
            maybe_s_sc[0][hd] = score(0, hd)
        init = tuple((jnp.full((1, tq), MASK_VALUE, F32), jnp.zeros((1, tq), F32))
                     for _ in heads)
        step = step_online
    else:
        init = tuple(jnp.zeros((1, tq), F32) for _ in heads)
        step = step_plain
    n_done = 0
    if not online:
        for n_tiles in ATT_UNROLLS:
            n_steps = (n_full - n_done) // n_tiles
            init = lax.fori_loop(
                0, n_steps, functools.partial(multi_plain, n_tiles=n_tiles, first=n_done), init)
            n_done = n_done + n_tiles * n_steps
    carry = lax.fori_loop(n_done, n_full, functools.partial(step, masked=False), init)
    carry = lax.fori_loop(n_full, n_kv, functools.partial(step, masked=True), carry)
    last = jnp.maximum(n_kv - 1, 0)
    ls = [c[1] for c in carry] if online else carry
    o_t = jnp.concatenate([(acc_sc[hd] + pv(last, hd)) / ls[hd] for hd in heads], axis=0)
    o_ref[0] = o_t.T.astype(BF16)


def _attention(q, k, vt, n_full, n_kv, posq, posk, online):
    B, H, _, S = q.shape
    tq = ATT_TQ
    nh = ATT_HEADS
    nk = S // ATT_TK
    scratch = [pltpu.VMEM((nh, ATT_TK, tq), BF16), pltpu.VMEM((nh, MLA_V, tq), F32)]
    if online:
        scratch.append(pltpu.VMEM((nh, ATT_TK, tq), F32))
    grid_spec = pltpu.PrefetchScalarGridSpec(
        num_scalar_prefetch=2,
        grid=(B, H // nh, S // tq),
        in_specs=[
            pl.BlockSpec((1, nh, HEAD_SLOT, tq), lambda b, hp, qi, *_: (b, hp, 0, qi)),
            pl.BlockSpec((1, nh, S, HEAD_SLOT), lambda b, hp, qi, *_: (b, hp, 0, 0)),
            pl.BlockSpec((1, nk, nh * MLA_V, ATT_TK), lambda b, hp, qi, *_: (b, 0, hp, 0)),
            pl.BlockSpec((1, 1, tq), lambda b, hp, qi, *_: (qi, 0, 0)),
            pl.BlockSpec((S, 1), lambda b, hp, qi, *_: (0, 0)),
        ],
        out_specs=pl.BlockSpec((1, tq, nh * MLA_V), lambda b, hp, qi, *_: (b, qi, hp)),
        scratch_shapes=scratch,
    )
    return pl.pallas_call(
        functools.partial(_attn_kernel, online=online),
        grid_spec=grid_spec,
        out_shape=jax.ShapeDtypeStruct((B, S, H * MLA_V), BF16),
        compiler_params=pltpu.CompilerParams(
            dimension_semantics=("parallel", "parallel", "arbitrary"),
            vmem_limit_bytes=VMEM_LIMIT),
        name="mla_attention_online" if online else "mla_attention",
    )(n_full, n_kv, q, k, vt, posq, posk)


def _hgrn_kernel(rec_ref, o_ref, st_sc):
    sb = pl.program_id(1)
    R = HG_ROWS
    C = HG_CHUNK
    W = HG_WIDTH

    @pl.when(sb == 0)
    def _():
        st_sc[...] = jnp.zeros(st_sc.shape, F32)

    ri = lax.broadcasted_iota(jnp.int32, (C, C), 0)
    ci = lax.broadcasted_iota(jnp.int32, (C, C), 1)
    sub_shift = HG_SUB.bit_length() - 1
    dk_shift = HG_DK.bit_length() - 1
    tri = jnp.where(ci <= ri, 1.0, 0.0)
    tri_blk = jnp.where((ci >> sub_shift) == (ri >> sub_shift), tri, 0.0)
    cum_mat = jnp.concatenate([tri, tri_blk], axis=0).astype(BF16)
    tt = ri & (HG_SUB - 1)
    lane_w = lax.broadcasted_iota(jnp.int32, (1, W), 1)
    head_masks = [jnp.where((lane_w >> dk_shift) == hh, 1.0, 0.0) for hh in range(HG_HEADS)]
    rw = lax.broadcasted_iota(jnp.int32, (W, W), 0)
    cw = lax.broadcasted_iota(jnp.int32, (W, W), 1)
    bd_mask = (rw >> dk_shift) == (cw >> dk_shift)
    ones_bd = jnp.where(bd_mask, 1.0, 0.0).astype(BF16)
    n_sub = C // HG_SUB

    def group(gi, carry):
        base = gi * (HG_GROUP * C)
        starts = [pl.multiple_of(base + j * C, C) for j in range(HG_GROUP)]
        chunks = range(HG_GROUP)
        gates, splits = [], []
        for r0 in starts:
            q = rec_ref[0, pl.ds(r0, C), 0:W]
            log_f = rec_ref[0, pl.ds(r0, C), W:2 * W]
            kk = rec_ref[0, pl.ds(r0, C), 2 * W:3 * W]
            vi = rec_ref[0, pl.ds(r0, C), 3 * W:4 * W]
            f_hi = log_f.astype(BF16)
            r1 = log_f - f_hi.astype(F32)
            f_mid = r1.astype(BF16)
            f_lo = (r1 - f_mid.astype(F32)).astype(BF16)
            gates.append((q, kk, vi.astype(BF16)))
            splits.append((f_hi, f_mid, f_lo))
        cums = [_dot(cum_mat, s[0]) + _dot(cum_mat, s[1]) + _dot(cum_mat, s[2]) for s in splits]

        def diag_direct(j, i):
            q, kk, vb = gates[j]
            rows = slice(i * HG_SUB, (i + 1) * HG_SUB)
            b_t, q_t, k_t = cums[j][:C][rows], q[rows], kk[rows]
            v_t = vb[rows].astype(F32)
            t_idx = lax.broadcasted_iota(jnp.int32, (HG_SUB, W), 0)
            slabs = []
            for s in range(HG_SUB):
                w = jnp.exp(jnp.minimum(b_t - b_t[s:s + 1, :], 0.0))
                slabs.append(jnp.where(t_idx >= s, q_t * k_t[s:s + 1, :] * w, 0.0))
            x = jnp.concatenate(slabs, axis=0).astype(BF16)
            a = _dot(x, ones_bd)
            o = a[0:HG_SUB] * v_t[0:1, :]
            for s in range(1, HG_SUB):
                o = o + a[s * HG_SUB:(s + 1) * HG_SUB] * v_t[s:s + 1, :]
            return o

        def finish(direct_diag):
            qbs, b_lasts, lhss, kds, kdls = [], [], [], [], []
            for j in chunks:
                q, kk, _ = gates[j]
                b = cums[j][:C]
                g = cums[j][C:]
                b_last = b[C - 1:C, :]
                rref = b - g
                qd = q * jnp.exp(g)
                qbs.append((q * jnp.exp(b)).astype(BF16))
                b_lasts.append(b_last)
                kdls.append((kk * jnp.exp(b_last - b)).astype(BF16))
                lhs_j, kd_j = [], []
                for i in range(n_sub):
                    rr = rref[i * HG_SUB:i * HG_SUB + 1, :]
                    kd_j.append((kk * jnp.exp(jnp.minimum(rr - b, EXP_CLAMP))).astype(BF16))
                    qi = qd[i * HG_SUB:(i + 1) * HG_SUB]
                    lhs_j.append(jnp.concatenate([qi * hm for hm in head_masks],
                                                 axis=0).astype(BF16))
                lhss.append(lhs_j)
                kds.append(kd_j)
            attn = [[_dot_nt(lhss[j][i], kds[j][i]) for i in range(n_sub)] for j in chunks]
            upds = [_dot_tn(gates[j][2], kdls[j]) for j in chunks]
            if direct_diag:
                attn = [[jnp.where(ci < i * HG_SUB, attn[j][i], 0.0).astype(BF16)
                         for i in range(n_sub)] for j in chunks]
            else:
                attn = [[jnp.where(ci <= i * HG_SUB + tt, attn[j][i], 0.0).astype(BF16)
                         for i in range(n_sub)] for j in chunks]
            pvs = [[_dot(attn[j][i], gates[j][2]) for i in range(n_sub)] for j in chunks]
            st = st_sc[...]
            for j in chunks:
                o_parts = []
                for i in range(n_sub):
                    pv = pvs[j][i]
                    oi = pv[0:HG_SUB] * head_masks[0]
                    for hh in range(1, HG_HEADS):
                        oi = oi + pv[hh * HG_SUB:(hh + 1) * HG_SUB] * head_masks[hh]
                    if direct_diag:
                        oi = oi + diag_direct(j, i)
                    o_parts.append(oi)
                o_inter = _dot_nt(qbs[j], st.astype(BF16))
                o_ref[0, pl.ds(starts[j], C), :] = o_inter + jnp.concatenate(o_parts, axis=0)
                st = st * jnp.exp(b_lasts[j]) + jnp.where(bd_mask, upds[j], 0.0)
            st_sc[...] = st
            return jnp.int32(0)

        g_min = cums[0][C:]
        for j in range(1, HG_GROUP):
            g_min = jnp.minimum(g_min, cums[j][C:])
        lax.cond(jnp.min(g_min) < -EXP_CLAMP,
                 functools.partial(finish, True), functools.partial(finish, False))
        return carry

    lax.fori_loop(0, R // (C * HG_GROUP), group, 0)


def _hgrn(rec):
    B, S, _ = rec.shape
    R = HG_ROWS
    return pl.pallas_call(
        _hgrn_kernel,
        grid=(B, S // R),
        in_specs=[pl.BlockSpec((1, R, 4 * HG_WIDTH), lambda b, s: (b, s, 0))],
        out_specs=pl.BlockSpec((1, R, HG_WIDTH), lambda b, s: (b, s, 0)),
        out_shape=jax.ShapeDtypeStruct((B, S, HG_WIDTH), F32),
        scratch_shapes=[pltpu.VMEM((HG_WIDTH, HG_WIDTH), F32)],
        compiler_params=pltpu.CompilerParams(
            dimension_semantics=("parallel", "arbitrary"), vmem_limit_bytes=VMEM_LIMIT),
        name="hgrn",
    )(rec)


def _mem_kv_kernel(mem_ref, g_ref, w_ref, gk_ref, k_out, v_out):
    m = mem_ref[0]
    mn = m * _rms(m) * g_ref[0]
    kv = _dot(mn.astype(BF16), w_ref[0])
    hw = X_HEADS * X_HEAD_DIM
    gk = gk_ref[0]
    for hd in range(X_HEADS):
        sl = slice(hd * X_HEAD_DIM, (hd + 1) * X_HEAD_DIM)
        kh = kv[:, sl]
        k_out[0, 0, :, sl] = (kh * _rms(kh) * gk).astype(BF16)
    v_out[0, 0] = kv[:, hw:].astype(BF16)


def _mem_kv(mem, mem_norm_g, w_xkv, xk_norm_g):
    B, M, _ = mem.shape
    L = w_xkv.shape[0]
    hw = X_HEADS * X_HEAD_DIM
    out_spec = pl.BlockSpec((1, 1, M, hw), lambda l, b: (l, b, 0, 0))
    return pl.pallas_call(
        _mem_kv_kernel,
        grid=(L, B),
        in_specs=[
            pl.BlockSpec((1, M, D_MODEL), lambda l, b: (b, 0, 0)),
            pl.BlockSpec((1, 1, D_MODEL), lambda l, b: (l, 0, 0)),
            pl.BlockSpec((1, D_MODEL, 2 * hw), lambda l, b: (l, 0, 0)),
            pl.BlockSpec((1, 1, X_HEAD_DIM), lambda l, b: (l, 0, 0)),
        ],
        out_specs=[out_spec, out_spec],
        out_shape=[jax.ShapeDtypeStruct((L, B, M, hw), BF16)] * 2,
        compiler_params=pltpu.CompilerParams(
            dimension_semantics=("parallel", "parallel"), vmem_limit_bytes=VMEM_LIMIT),
        name="mem_kv",
    )(mem, mem_norm_g, w_xkv, xk_norm_g)


def _out_xattn_kernel(x_ref, ya_ref, rec_ref, post_ref, onorm_ref, convw_ref, wo_ref, g_ref,
                      wq_ref, gq_ref, k_ref, v_ref, wxo_ref, o_ref, ubuf_sc):
    half = wo_ref.shape[0] // 2
    tm = x_ref.shape[1]
    W = HG_WIDTH
    subs = [slice(j * XATTN_SUB, (j + 1) * XATTN_SUB) for j in range(tm // XATTN_SUB)]

    rw = lax.broadcasted_iota(jnp.int32, (W, W), 0)
    cw = lax.broadcasted_iota(jnp.int32, (W, W), 1)
    dk_shift = HG_DK.bit_length() - 1
    ones_bd = jnp.where((rw >> dk_shift) == (cw >> dk_shift), 1.0, 0.0).astype(BF16)
    @pl.when(pl.program_id(1) == 0)
    def _():
        ubuf_sc[0:8, :] = jnp.zeros((8, CONV_WIDTH), F32)

    ubuf_sc[8:8 + tm, :] = post_ref[0, :, W + CONV_WIDTH:W + 2 * CONV_WIDTH]
    wc = convw_ref[...]
    ybs = []
    for r in subs:
        o = rec_ref[0, r, :]
        ms = _dot((o * o).astype(BF16), ones_bd) * (1.0 / HG_DK)
        y_hg = o * lax.rsqrt(ms + EPS) * onorm_ref[...] * post_ref[0, r, 0:W]
        u0, u1, u2 = (ubuf_sc[8 - d + r.start:8 - d + r.stop, :] for d in range(CONV_K))
        y_cv = post_ref[0, r, W:W + CONV_WIDTH] * (
            u2 * wc[0:1, :] + u1 * wc[1:2, :] + u0 * wc[2:3, :])
        ybs.append(jnp.concatenate([y_hg, y_cv], axis=1).astype(BF16))
    ubuf_sc[0:8, :] = ubuf_sc[tm:tm + 8, :]

    slots = [slice(hd * X_HEAD_DIM, (hd + 1) * X_HEAD_DIM) for hd in range(X_HEADS)]
    gq = gq_ref[...] * (1.0 / math.sqrt(X_HEAD_DIM))
    x1s = [x_ref[0, r, :] + _dot(ya_ref[0, r, :], wo_ref[:half, :])
           + _dot(yb, wo_ref[half:, :]) for r, yb in zip(subs, ybs)]
    hs = [(x1 * _rms(x1) * g_ref[...]).astype(BF16) for x1 in x1s]
    qs = [_dot(h, wq_ref[...]) for h in hs]
    qhs = [[(q[:, sl] * _rms(q[:, sl]) * gq).astype(BF16) for sl in slots] for q in qs]
    ss = [[_dot_nt(qh, k_ref[0, 0, :, sl]) for qh, sl in zip(row, slots)] for row in qhs]
    ps, ls = [], []
    for row in ss:
        p_row, l_row = [], []
        for s in row:
            p = jnp.exp(s - jnp.max(s, axis=1, keepdims=True))
            l_row.append(jnp.sum(p, axis=1, keepdims=True))
            p_row.append(p.astype(BF16))
        ps.append(p_row)
        ls.append(l_row)
    os_ = [jnp.concatenate([_dot(p, v_ref[0, 0, :, sl]) / l
                            for p, l, sl in zip(p_row, l_row, slots)], axis=1).astype(BF16)
           for p_row, l_row in zip(ps, ls)]
    for r, x1, o in zip(subs, x1s, os_):
        o_ref[0, r, :] = x1 + _dot(o, wxo_ref[...])


def _out_xattn(x, y_mla, o_rec, post, onorm_g, conv_w, w_out, g, w_xq, gq, k_mem, v_mem, w_xo,
               layer):
    B, S, _ = x.shape
    tm = ROW_TILE
    hw = X_HEADS * X_HEAD_DIM
    const = lambda b, s: (0, 0)
    row = lambda w: pl.BlockSpec((1, tm, w), lambda b, s: (b, s, 0))
    mem_spec = pl.BlockSpec((1, 1, MEM_LEN, hw), lambda b, s: (layer, b, 0, 0))
    return pl.pallas_call(
        _out_xattn_kernel,
        grid=(B, S // tm),
        in_specs=[
            row(D_MODEL), row(y_mla.shape[-1]), row(o_rec.shape[-1]), row(post.shape[-1]),
            pl.BlockSpec((1, HG_WIDTH), const),
            pl.BlockSpec((CONV_K, CONV_WIDTH), const),
            pl.BlockSpec((D_MODEL, D_MODEL), const),
            pl.BlockSpec((1, D_MODEL), const),
            pl.BlockSpec((D_MODEL, hw), const),
            pl.BlockSpec((1, X_HEAD_DIM), const),
            mem_spec, mem_spec,
            pl.BlockSpec((hw, D_MODEL), const),
        ],
        out_specs=row(D_MODEL),
        out_shape=jax.ShapeDtypeStruct((B, S, D_MODEL), F32),
        scratch_shapes=[pltpu.VMEM((tm + 8, CONV_WIDTH), F32)],
        compiler_params=pltpu.CompilerParams(
            dimension_semantics=("parallel", "arbitrary"), vmem_limit_bytes=VMEM_LIMIT),
        name="out_xattn",
    )(x, y_mla, o_rec, post, onorm_g, conv_w, w_out, g, w_xq, gq, k_mem, v_mem, w_xo)


def _mlp_kernel(x_ref, g_ref, wu_ref, wd_ref, o_ref):
    x = x_ref[0]
    h = (x * _rms(x) * g_ref[...]).astype(BF16)
    acc = x
    step = D_MODEL
    for c in range(D_FF // step):
        u = _dot(h, wu_ref[:, c * step:(c + 1) * step])
        a = jnp.square(jnp.maximum(u, 0.0)).astype(BF16)
        acc = acc + _dot(a, wd_ref[c * step:(c + 1) * step, :])
    o_ref[0] = acc


def _mlp(x, g, w_up, w_down):
    B, S, _ = x.shape
    tm = ROW_TILE
    const = lambda b, s: (0, 0)
    row = pl.BlockSpec((1, tm, D_MODEL), lambda b, s: (b, s, 0))
    return pl.pallas_call(
        _mlp_kernel,
        grid=(B, S // tm),
        in_specs=[row, pl.BlockSpec((1, D_MODEL), const),
                  pl.BlockSpec((D_MODEL, D_FF), const),
                  pl.BlockSpec((D_FF, D_MODEL), const)],
        out_specs=row,
        out_shape=jax.ShapeDtypeStruct((B, S, D_MODEL), F32),
        compiler_params=pltpu.CompilerParams(
            dimension_semantics=("parallel", "parallel"), vmem_limit_bytes=VMEM_LIMIT),
        name="mlp",
    )(x, g, w_up, w_down)


def _scatter_lanes(w_cols):
    idx = jnp.asarray(np.where(_LANE_MAP >= 0, _LANE_MAP, 0), jnp.int32)
    valid = jnp.asarray(_LANE_MAP >= 0)
    return jnp.where(valid, jnp.take(w_cols, idx, axis=-1), 0.0)


def _prep_layer(l, positions, w_in, w_uq, w_ukv, mla_qn_g, mla_kn_g):
    o_kr = MLA_Q_RANK + MLA_KV_RANK
    w = w_in[l]
    kr_src = jnp.concatenate(
        [jnp.zeros((D_MODEL, MLA_NOPE), F32), w[:, o_kr:o_kr + MLA_ROPE]], axis=1)
    w_in_p = jnp.concatenate(
        [w[:, :o_kr], _scatter_lanes(kr_src), w[:, o_kr + MLA_ROPE:]], axis=1).astype(BF16)

    def with_swapped_halves(w_slots):
        rank = w_slots.shape[0]
        both = jnp.concatenate([w_slots, jnp.roll(w_slots, HEAD_SLOT // 2, axis=-1)], axis=1)
        return both.reshape(rank, 2 * MLA_HEADS * HEAD_SLOT).astype(BF16)

    wq = w_uq[l].reshape(MLA_Q_RANK, MLA_HEADS, MLA_QK)
    w_uq_p = with_swapped_halves(_scatter_lanes(wq))

    wkv = w_ukv[l].reshape(MLA_KV_RANK, MLA_HEADS, MLA_NOPE + MLA_V)
    k_src = jnp.concatenate(
        [wkv[..., :MLA_NOPE], jnp.zeros((MLA_KV_RANK, MLA_HEADS, MLA_ROPE), F32)], axis=-1)
    w_uk_p = with_swapped_halves(_scatter_lanes(k_src))
    w_vt = wkv[..., MLA_NOPE:].reshape(MLA_KV_RANK, MLA_HEADS * MLA_V).T.astype(BF16)

    inv_freq = ROPE_BASE ** (-jnp.arange(0, MLA_ROPE, 2, dtype=F32) / MLA_ROPE)
    ang = positions.astype(F32)[:, None] * inv_freq[None, :]
    cos, sin = jnp.cos(ang), jnp.sin(ang)
    S = positions.shape[0]
    pad = HEAD_SLOT // 2 - ROPE_HALF
    c_tab = jnp.concatenate([cos, jnp.ones((S, pad), F32), cos, jnp.ones((S, pad), F32)], axis=1)
    s_tab = jnp.concatenate([-sin, jnp.zeros((S, pad), F32), sin, jnp.zeros((S, pad), F32)], axis=1)

    def tables(gain, scale):
        g = _scatter_lanes(gain)[None, :] * scale
        return c_tab * g, s_tab * jnp.roll(g, HEAD_SLOT // 2, axis=1)

    cq_t, sq_t = tables(mla_qn_g[l], math.log2(math.e) / math.sqrt(MLA_QK))
    ck_t, sk_t = tables(mla_kn_g[l], 1.0)
    return w_in_p, w_uq_p, w_uk_p, w_vt, (cq_t, sq_t, ck_t, sk_t)


def kernel(x, mem, positions, mix_norm_g, w_in, mla_q_norm_g, mla_kv_norm_g, w_uq, w_ukv,
           mla_qn_g, mla_kn_g, hgrn_lb_logits, hgrn_o_norm_g, conv_w, w_out,
           xattn_norm_g, mem_norm_g, w_xq, w_xkv, xq_norm_g, xk_norm_g, w_xo,
           mlp_norm_g, w_up, w_down):
    B, S, _ = x.shape
    L = w_in.shape[0]
    nq, nk = S // ATT_TQ, S // ATT_TK
    pq = positions.reshape(nq, ATT_TQ)
    pk = positions.reshape(nk, ATT_TK)
    vis = jnp.min(pk, axis=1)[None, :] <= jnp.max(pq, axis=1)[:, None]
    n_kv = jnp.max(jnp.where(vis, jnp.arange(1, nk + 1, dtype=jnp.int32)[None, :], 0),
                   axis=1).astype(jnp.int32)
    full = jnp.max(pk, axis=1)[None, :] <= jnp.min(pq, axis=1)[:, None]
    n_full = jnp.sum(jnp.cumprod(full.astype(jnp.int32), axis=1), axis=1).astype(jnp.int32)
    posq = positions.reshape(nq, 1, ATT_TQ)
    posk = positions.reshape(S, 1)

    k_mem, v_mem = _mem_kv(mem, mem_norm_g.reshape(L, 1, D_MODEL), w_xkv.astype(BF16),
                           xk_norm_g.reshape(L, 1, X_HEAD_DIM))
    for l in range(L):
        w_in_p, w_uq_p, w_uk_p, w_vt, tabs = _prep_layer(l, positions, w_in, w_uq, w_ukv,
                                                         mla_qn_g, mla_kn_g)
        lat, rec, post = _in_proj(x, mix_norm_g[l][None], w_in_p, hgrn_lb_logits, l)
        q, k, vt = _mla_qkv(lat, mla_q_norm_g[l][None], w_uq_p, mla_kv_norm_g[l][None],
                            w_uk_p, w_vt, tabs)
        score_bound = (math.log2(math.e) * math.sqrt(MLA_QK)
                       * jnp.max(jnp.abs(mla_qn_g[l])) * jnp.max(jnp.abs(mla_kn_g[l])))
        y_mla = lax.cond(score_bound <= ATT_PLAIN_MAX_SCORE,
                         functools.partial(_attention, online=False),
                         functools.partial(_attention, online=True),
                         q, k, vt, n_full, n_kv, posq, posk)
        o_rec = _hgrn(rec)
        x = _out_xattn(x, y_mla, o_rec, post, hgrn_o_norm_g[l][None], conv_w[l],
                       w_out[l].astype(BF16), xattn_norm_g[l][None], w_xq[l].astype(BF16),
                       xq_norm_g[l][None], k_mem, v_mem, w_xo[l].astype(BF16), l)
        x = _mlp(x, mlp_norm_g[l][None], w_up[l].astype(BF16), w_down[l].astype(BF16))
    return x
```

```python
import functools
import math

import numpy as np
import jax
import jax.numpy as jnp
from jax import lax
from jax.experimental import pallas as pl
from jax.experimental.pallas import tpu as pltpu

F32 = jnp.float32
BF16 = jnp.bfloat16

D_MODEL = 1024
DEPTH = 4
MEM_LEN = 256
EPS = 1e-6

MLA_HEADS = 8
MLA_NOPE = 64
MLA_ROPE = 32
MLA_V = 64
MLA_QK = MLA_NOPE + MLA_ROPE
MLA_Q_RANK = 384
MLA_KV_RANK = 256
ROPE_BASE = 10000.0

HG_HEADS = 4
HG_DK = 64
HG_CHUNK = 64
HG_SUB = 16
HG_GROUP = 8
HG_WIDTH = 256
CONV_WIDTH = 256
CONV_K = 3

X_HEADS = 4
X_HEAD_DIM = 128
D_FF = 4 * D_MODEL

LANES = 128
HEAD_SLOT = LANES
ROPE_HALF = MLA_ROPE // 2
N_IN_PAD = MLA_Q_RANK + MLA_KV_RANK + HEAD_SLOT + 4 * HG_WIDTH + 3 * CONV_WIDTH
HC_OFF = MLA_Q_RANK + MLA_KV_RANK + HEAD_SLOT
HC_WIDTH = N_IN_PAD - HC_OFF

ROW_TILE = 1024
MLP_TILE = 1024
IN_PROJ_SUB = 256
XATTN_SUB = 256
ATT_TQ = 256
ATT_TK = 256
ATT_HEADS = 8
ATT_CHAINS = 4
ATT_UNROLLS = (4, 2)
HG_ROWS = 512
MASK_VALUE = -1e30
ATT_PLAIN_MAX_SCORE = 64.0
EXP_CLAMP = 60.0
VMEM_LIMIT = 56 * 1024 * 1024


def _head_lane_map():
    m = -np.ones((HEAD_SLOT,), np.int64)
    m[0:ROPE_HALF] = MLA_NOPE + np.arange(ROPE_HALF)
    m[ROPE_HALF:64] = np.arange(64 - ROPE_HALF)
    m[64:64 + ROPE_HALF] = MLA_NOPE + ROPE_HALF + np.arange(ROPE_HALF)
    m[64 + ROPE_HALF:96] = (64 - ROPE_HALF) + np.arange(ROPE_HALF)
    return m


_LANE_MAP = _head_lane_map()


def _rms(x, n=None):
    n = x.shape[-1] if n is None else n
    return lax.rsqrt(jnp.sum(x * x, axis=-1, keepdims=True) * (1.0 / n) + EPS)


def _sigmoid(x):
    return 1.0 / (1.0 + jnp.exp(-x))


def _dot(a, b):
    return jnp.dot(a, b, preferred_element_type=F32)


def _dot_nt(a, b):
    return lax.dot_general(a, b, (((1,), (1,)), ((), ())), preferred_element_type=F32)


def _dot_tn(a, b):
    return lax.dot_general(a, b, (((0,), (0,)), ((), ())), preferred_element_type=F32)


def _in_proj_kernel(x_ref, g_ref, w_in_ref, lbl_ref, lat_out, rec_out, post_out, *, layer):
    W = HG_WIDTH
    subs = [slice(j * IN_PROJ_SUB, (j + 1) * IN_PROJ_SUB)
            for j in range(x_ref.shape[1] // IN_PROJ_SUB)]
    hs = []
    for r in subs:
        x = x_ref[0, r, :]
        hs.append((x * _rms(x) * g_ref[...]).astype(BF16))
    hcs = []
    for r, h in zip(subs, hs):
        lat_out[0, r, :] = _dot(h, w_in_ref[:, :HC_OFF])
        hcs.append(_dot(h, w_in_ref[:, HC_OFF:]))

    lg = lbl_ref[...]
    e = jnp.exp(lg - jnp.max(lg, axis=0, keepdims=True))
    soft = e / jnp.sum(e, axis=0, keepdims=True)
    lb = jnp.zeros((1, W), F32)
    for i in range(1, layer + 1):
        lb = lb + soft[i:i + 1, :]
    lb = jnp.maximum(lb, 0.0)
    log_lb = jnp.log(lb)
    log_1m = jnp.log1p(-lb)

    for rows, hc in zip(subs, hcs):
        zq, zf, vi, zg = (hc[:, i * W:(i + 1) * W] for i in range(4))
        cb, cc, cx = (hc[:, 4 * W + i * CONV_WIDTH:4 * W + (i + 1) * CONV_WIDTH]
                      for i in range(3))
        t = jnp.exp(-jnp.abs(zf))
        d = 1.0 + t
        r = 1.0 / d
        log_sig = jnp.minimum(zf, 0.0) - jnp.log(d)
        c2 = log_1m + log_sig
        log_f = jnp.maximum(log_lb, c2) + jnp.log(1.0 + jnp.exp(-jnp.abs(log_lb - c2)))
        rec_out[0, rows, 0:W] = zq * _sigmoid(zq)
        rec_out[0, rows, W:2 * W] = log_f
        rec_out[0, rows, 2 * W:3 * W] = (1.0 - lb) * jnp.where(zf >= 0.0, t * r, r)
        rec_out[0, rows, 3 * W:4 * W] = vi
        post_out[0, rows, 0:W] = zg * _sigmoid(zg)
        post_out[0, rows, W:W + CONV_WIDTH] = cb
        post_out[0, rows, W + CONV_WIDTH:W + 2 * CONV_WIDTH] = cc * cx


def _in_proj(x, g, w_in, lb_logits, layer):
    B, S, _ = x.shape
    tm = ROW_TILE
    const = lambda b, s: (0, 0)
    widths = (HC_OFF, 4 * HG_WIDTH, HG_WIDTH + 2 * CONV_WIDTH)
    return pl.pallas_call(
        functools.partial(_in_proj_kernel, layer=layer),
        grid=(B, S // tm),
        in_specs=[
            pl.BlockSpec((1, tm, D_MODEL), lambda b, s: (b, s, 0)),
            pl.BlockSpec((1, D_MODEL), const),
            pl.BlockSpec((D_MODEL, N_IN_PAD), const),
            pl.BlockSpec((DEPTH, HG_WIDTH), const),
        ],
        out_specs=[pl.BlockSpec((1, tm, w), lambda b, s: (b, s, 0)) for w in widths],
        out_shape=[jax.ShapeDtypeStruct((B, S, w), F32) for w in widths],
        compiler_params=pltpu.CompilerParams(
            dimension_semantics=("parallel", "parallel"), vmem_limit_bytes=VMEM_LIMIT),
        name="in_proj",
    )(x, g, w_in, lb_logits)


def _mla_qkv_kernel(lat_ref, gq_ref, w_uq_ref, gkv_ref, w_uk_ref, w_vt_ref,
                    cq_tab, sq_tab, ck_tab, sk_tab, q_out, k_out, vt_out):
    hw = MLA_HEADS * HEAD_SLOT
    subs = [slice(j * ATT_TK, (j + 1) * ATT_TK) for j in range(vt_out.shape[1])]
    lats = []
    for r in subs:
        cq = lat_ref[0, r, :MLA_Q_RANK]
        ckv = lat_ref[0, r, MLA_Q_RANK:MLA_Q_RANK + MLA_KV_RANK]
        lats.append(((cq * _rms(cq) * gq_ref[...]).astype(BF16),
                     (ckv * _rms(ckv) * gkv_ref[...]).astype(BF16)))
    qfs = [_dot(cqn, w_uq_ref[...]) for cqn, _ in lats]
    kfs = [_dot(ckvn, w_uk_ref[...]) for _, ckvn in lats]
    for j, (_, ckvn) in enumerate(lats):
        vt_out[0, j] = _dot_nt(w_vt_ref[...], ckvn).astype(BF16)
    for r, qf, kf in zip(subs, qfs, kfs):
        kr = lat_ref[0, r, MLA_Q_RANK + MLA_KV_RANK:]
        kr_sw = pltpu.roll(kr, HEAD_SLOT // 2, 1)
        cq_t, sq_t, ck_t, sk_t = cq_tab[r, :], sq_tab[r, :], ck_tab[r, :], sk_tab[r, :]
        for hd in range(MLA_HEADS):
            sl = slice(hd * HEAD_SLOT, (hd + 1) * HEAD_SLOT)
            sw = slice(hw + hd * HEAD_SLOT, hw + (hd + 1) * HEAD_SLOT)
            qh = qf[:, sl]
            qh = (qh * cq_t + qf[:, sw] * sq_t) * _rms(qh, MLA_QK)
            q_out[0, hd, :, r] = qh.T.astype(BF16)
            kh = kf[:, sl] + kr
            kh = (kh * ck_t + (kf[:, sw] + kr_sw) * sk_t) * _rms(kh, MLA_QK)
            k_out[0, hd, r, :] = kh.astype(BF16)


def _mla_qkv(lat, gq, w_uq, gkv, w_uk, w_vt, tabs):
    B, S, _ = lat.shape
    tm = ROW_TILE
    vw = MLA_HEADS * MLA_V
    const = lambda b, s: (0, 0)
    tab_spec = pl.BlockSpec((tm, HEAD_SLOT), lambda b, s: (s, 0))
    return pl.pallas_call(
        _mla_qkv_kernel,
        grid=(B, S // tm),
        in_specs=[
            pl.BlockSpec((1, tm, HC_OFF), lambda b, s: (b, s, 0)),
            pl.BlockSpec((1, MLA_Q_RANK), const),
            pl.BlockSpec((MLA_Q_RANK, 2 * MLA_HEADS * HEAD_SLOT), const),
            pl.BlockSpec((1, MLA_KV_RANK), const),
            pl.BlockSpec((MLA_KV_RANK, 2 * MLA_HEADS * HEAD_SLOT), const),
            pl.BlockSpec((vw, MLA_KV_RANK), const),
            tab_spec, tab_spec, tab_spec, tab_spec,
        ],
        out_specs=[
            pl.BlockSpec((1, MLA_HEADS, HEAD_SLOT, tm), lambda b, s: (b, 0, 0, s)),
            pl.BlockSpec((1, MLA_HEADS, tm, HEAD_SLOT), lambda b, s: (b, 0, s, 0)),
            pl.BlockSpec((1, tm // ATT_TK, vw, ATT_TK), lambda b, s: (b, s, 0, 0)),
        ],
        out_shape=[
            jax.ShapeDtypeStruct((B, MLA_HEADS, HEAD_SLOT, S), BF16),
            jax.ShapeDtypeStruct((B, MLA_HEADS, S, HEAD_SLOT), BF16),
            jax.ShapeDtypeStruct((B, S // ATT_TK, vw, ATT_TK), BF16),
        ],
        compiler_params=pltpu.CompilerParams(
            dimension_semantics=("parallel", "parallel"), vmem_limit_bytes=VMEM_LIMIT),
        name="mla_qkv",
    )(lat, gq, w_uq, gkv, w_uk, w_vt, *tabs)


def _attn_kernel(nfull_ref, nkv_ref, q_ref, k_ref, vt_ref, posq_ref, posk_ref, o_ref,
                 p_sc, acc_sc, *maybe_s_sc, online):
    qi = pl.program_id(2)
    n_full = nfull_ref[qi]
    n_kv = nkv_ref[qi]
    tq = q_ref.shape[3]
    nk = vt_ref.shape[1]
    heads = range(ATT_HEADS)
    qts = [q_ref[0, hd] for hd in heads]
    posq = posq_ref[0]

    groups = [list(heads)[i:i + ATT_CHAINS] for i in range(0, ATT_HEADS, ATT_CHAINS)]

    def score(ki, hd):
        off = pl.multiple_of(ki * ATT_TK, ATT_TK)
        return _dot(k_ref[0, hd, pl.ds(off, ATT_TK), :], qts[hd])

    def pv(ki, hd, p=None):
        p = p_sc[hd] if p is None else p
        return _dot(vt_ref[0, ki, hd * MLA_V:(hd + 1) * MLA_V, :], p)

    def issue(ki_scores, ki_pv):
        ss, pvs = {}, {}
        for g in groups:
            for hd in g:
                ss[hd] = score(ki_scores, hd)
            for hd in g:
                pvs[hd] = pv(ki_pv, hd)
        return ss, pvs

    def mask_of(ki):
        off = pl.multiple_of(ki * ATT_TK, ATT_TK)
        return posk_ref[pl.ds(off, ATT_TK), :] <= posq

    def step_plain(ki, carry, masked):
        ss, pvs = issue(ki, jnp.maximum(ki - 1, 0))
        if masked:
            mask = mask_of(ki)
        new = []
        for hd in heads:
            s = jnp.where(mask, ss[hd], MASK_VALUE) if masked else ss[hd]
            p = jnp.exp2(s)
            new.append(carry[hd] + jnp.sum(p, axis=0, keepdims=True))
            acc_sc[hd] = acc_sc[hd] + pvs[hd]
            p_sc[hd] = p.astype(BF16)
        return tuple(new)

    def multi_plain(j, carry, n_tiles, first):
        k0 = first + n_tiles * j
        ss, pvs = issue(k0, jnp.maximum(k0 - 1, 0))
        l_add = {hd: 0.0 for hd in heads}
        for hd in heads:
            acc_sc[hd] = acc_sc[hd] + pvs[hd]
        for u in range(1, n_tiles):
            ss_next = {}
            for g in groups:
                for hd in g:
                    ss_next[hd] = score(k0 + u, hd)
                for hd in g:
                    p = jnp.exp2(ss[hd])
                    l_add[hd] = l_add[hd] + jnp.sum(p, axis=0, keepdims=True)
                    acc_sc[hd] = acc_sc[hd] + pv(k0 + u - 1, hd, p.astype(BF16))
            ss = ss_next
        new = []
        for hd in heads:
            p = jnp.exp2(ss[hd])
            new.append(carry[hd] + l_add[hd] + jnp.sum(p, axis=0, keepdims=True))
            p_sc[hd] = p.astype(BF16)
        return tuple(new)

    def step_online(ki, carry, masked):
        (s_sc,) = maybe_s_sc
        s_next, pvs = issue(jnp.minimum(ki + 1, nk - 1), jnp.maximum(ki - 1, 0))
        if masked:
            mask = mask_of(ki)
        new = []
        for hd in heads:
            m_old, l_old = carry[hd]
            s = s_sc[hd]
            if masked:
                s = jnp.where(mask, s, MASK_VALUE)
            m_new = jnp.maximum(m_old, jnp.max(s, axis=0, keepdims=True))
            alpha = jnp.exp2(m_old - m_new)
            p = jnp.exp2(s - m_new)
            l_new = alpha * l_old + jnp.sum(p, axis=0, keepdims=True)
            acc_sc[hd] = alpha * (acc_sc[hd] + pvs[hd])
            p_sc[hd] = p.astype(BF16)
            new.append((m_new, l_new))
        for hd in heads:
            s_sc[hd] = s_next[hd]
        return tuple(new)

    p_sc[...] = jnp.zeros(p_sc.shape, BF16)
    acc_sc[...] = jnp.zeros(acc_sc.shape, F32)
    if online:
        for hd in heads:
            maybe_s_sc[0][hd] = score(0, hd)
        init = tuple((jnp.full((1, tq), MASK_VALUE, F32), jnp.zeros((1, tq), F32))
                     for _ in heads)
        step = step_online
    else:
        init = tuple(jnp.zeros((1, tq), F32) for _ in heads)
        step = step_plain
    n_done = 0
    if not online:
        for n_tiles in ATT_UNROLLS:
            n_steps = (n_full - n_done) // n_tiles
            init = lax.fori_loop(
                0, n_steps, functools.partial(multi_plain, n_tiles=n_tiles, first=n_done), init)
            n_done = n_done + n_tiles * n_steps
    carry = lax.fori_loop(n_done, n_full, functools.partial(step, masked=False), init)
    carry = lax.fori_loop(n_full, n_kv, functools.partial(step, masked=True), carry)
    last = jnp.maximum(n_kv - 1, 0)
    ls = [c[1] for c in carry] if online else carry
    o_t = jnp.concatenate([(acc_sc[hd] + pv(last, hd)) / ls[hd] for hd in heads], axis=0)
    o_ref[0] = o_t.T.astype(BF16)


def _attention(q, k, vt, n_full, n_kv, posq, posk, online):
    B, H, _, S = q.shape
    tq = ATT_TQ
    nh = ATT_HEADS
    nk = S // ATT_TK
    scratch = [pltpu.VMEM((nh, ATT_TK, tq), BF16), pltpu.VMEM((nh, MLA_V, tq), F32)]
    if online:
        scratch.append(pltpu.VMEM((nh, ATT_TK, tq), F32))
    grid_spec = pltpu.PrefetchScalarGridSpec(
        num_scalar_prefetch=2,
        grid=(B, H // nh, S // tq),
        in_specs=[
            pl.BlockSpec((1, nh, HEAD_SLOT, tq), lambda b, hp, qi, *_: (b, hp, 0, qi)),
            pl.BlockSpec((1, nh, S, HEAD_SLOT), lambda b, hp, qi, *_: (b, hp, 0, 0)),
            pl.BlockSpec((1, nk, nh * MLA_V, ATT_TK), lambda b, hp, qi, *_: (b, 0, hp, 0)),
            pl.BlockSpec((1, 1, tq), lambda b, hp, qi, *_: (qi, 0, 0)),
            pl.BlockSpec((S, 1), lambda b, hp, qi, *_: (0, 0)),
        ],
        out_specs=pl.BlockSpec((1, tq, nh * MLA_V), lambda b, hp, qi, *_: (b, qi, hp)),
        scratch_shapes=scratch,
    )
    return pl.pallas_call(
        functools.partial(_attn_kernel, online=online),
        grid_spec=grid_spec,
        out_shape=jax.ShapeDtypeStruct((B, S, H * MLA_V), BF16),
        compiler_params=pltpu.CompilerParams(
            dimension_semantics=("parallel", "parallel", "arbitrary"),
            vmem_limit_bytes=VMEM_LIMIT),
        name="mla_attention_online" if online else "mla_attention",
    )(n_full, n_kv, q, k, vt, posq, posk)


def _hgrn_kernel(rec_ref, o_ref, st_sc):
    sb = pl.program_id(1)
    R = HG_ROWS
    C = HG_CHUNK
    W = HG_WIDTH

    @pl.when(sb == 0)
    def _():
        st_sc[...] = jnp.zeros(st_sc.shape, F32)

    ri = lax.broadcasted_iota(jnp.int32, (C, C), 0)
    ci = lax.broadcasted_iota(jnp.int32, (C, C), 1)
    sub_shift = HG_SUB.bit_length() - 1
    dk_shift = HG_DK.bit_length() - 1
    tri = jnp.where(ci <= ri, 1.0, 0.0)
    tri_blk = jnp.where((ci >> sub_shift) == (ri >> sub_shift), tri, 0.0)
    cum_mat = jnp.concatenate([tri, tri_blk], axis=0).astype(BF16)
    tt = ri & (HG_SUB - 1)
    lane_w = lax.broadcasted_iota(jnp.int32, (1, W), 1)
    head_masks = [jnp.where((lane_w >> dk_shift) == hh, 1.0, 0.0) for hh in range(HG_HEADS)]
    rw = lax.broadcasted_iota(jnp.int32, (W, W), 0)
    cw = lax.broadcasted_iota(jnp.int32, (W, W), 1)
    bd_mask = (rw >> dk_shift) == (cw >> dk_shift)
    ones_bd = jnp.where(bd_mask, 1.0, 0.0).astype(BF16)
    n_sub = C // HG_SUB

    def group(gi, carry):
        base = gi * (HG_GROUP * C)
        starts = [pl.multiple_of(base + j * C, C) for j in range(HG_GROUP)]
        chunks = range(HG_GROUP)
        gates, splits = [], []
        for r0 in starts:
            q = rec_ref[0, pl.ds(r0, C), 0:W]
            log_f = rec_ref[0, pl.ds(r0, C), W:2 * W]
            kk = rec_ref[0, pl.ds(r0, C), 2 * W:3 * W]
            vi = rec_ref[0, pl.ds(r0, C), 3 * W:4 * W]
            f_hi = log_f.astype(BF16)
            r1 = log_f - f_hi.astype(F32)
            f_mid = r1.astype(BF16)
            f_lo = (r1 - f_mid.astype(F32)).astype(BF16)
            gates.append((q, kk, vi.astype(BF16)))
            splits.append((f_hi, f_mid, f_lo))
        cums = [_dot(cum_mat, s[0]) + _dot(cum_mat, s[1]) + _dot(cum_mat, s[2]) for s in splits]

        def diag_direct(j, i):
            q, kk, vb = gates[j]
            rows = slice(i * HG_SUB, (i + 1) * HG_SUB)
            b_t, q_t, k_t = cums[j][:C][rows], q[rows], kk[rows]
            v_t = vb[rows].astype(F32)
            t_idx = lax.broadcasted_iota(jnp.int32, (HG_SUB, W), 0)
            slabs = []
            for s in range(HG_SUB):
                w = jnp.exp(jnp.minimum(b_t - b_t[s:s + 1, :], 0.0))
                slabs.append(jnp.where(t_idx >= s, q_t * k_t[s:s + 1, :] * w, 0.0))
            x = jnp.concatenate(slabs, axis=0).astype(BF16)
            a = _dot(x, ones_bd)
            o = a[0:HG_SUB] * v_t[0:1, :]
            for s in range(1, HG_SUB):
                o = o + a[s * HG_SUB:(s + 1) * HG_SUB] * v_t[s:s + 1, :]
            return o

        def finish(direct_diag):
            qbs, b_lasts, lhss, kds, kdls = [], [], [], [], []
            for j in chunks:
                q, kk, _ = gates[j]
                b = cums[j][:C]
                g = cums[j][C:]
                b_last = b[C - 1:C, :]
                rref = b - g
                qd = q * jnp.exp(g)
                qbs.append((q * jnp.exp(b)).astype(BF16))
                b_lasts.append(b_last)
                kdls.append((kk * jnp.exp(b_last - b)).astype(BF16))
                lhs_j, kd_j = [], []
                for i in range(n_sub):
                    rr = rref[i * HG_SUB:i * HG_SUB + 1, :]
                    kd_j.append((kk * jnp.exp(jnp.minimum(rr - b, EXP_CLAMP))).astype(BF16))
                    qi = qd[i * HG_SUB:(i + 1) * HG_SUB]
                    lhs_j.append(jnp.concatenate([qi * hm for hm in head_masks],
                                                 axis=0).astype(BF16))
                lhss.append(lhs_j)
                kds.append(kd_j)
            attn = [[_dot_nt(lhss[j][i], kds[j][i]) for i in range(n_sub)] for j in chunks]
            upds = [_dot_tn(gates[j][2], kdls[j]) for j in chunks]
            if direct_diag:
                attn = [[jnp.where(ci < i * HG_SUB, attn[j][i], 0.0).astype(BF16)
                         for i in range(n_sub)] for j in chunks]
            else:
                attn = [[jnp.where(ci <= i * HG_SUB + tt, attn[j][i], 0.0).astype(BF16)
                         for i in range(n_sub)] for j in chunks]
            pvs = [[_dot(attn[j][i], gates[j][2]) for i in range(n_sub)] for j in chunks]
            st = st_sc[...]
            for j in chunks:
                o_parts = []
                for i in range(n_sub):
                    pv = pvs[j][i]
                    oi = pv[0:HG_SUB] * head_masks[0]
                    for hh in range(1, HG_HEADS):
                        oi = oi + pv[hh * HG_SUB:(hh + 1) * HG_SUB] * head_masks[hh]
                    if direct_diag:
                        oi = oi + diag_direct(j, i)
                    o_parts.append(oi)
                o_inter = _dot_nt(qbs[j], st.astype(BF16))
                o_ref[0, pl.ds(starts[j], C), :] = o_inter + jnp.concatenate(o_parts, axis=0)
                st = st * jnp.exp(b_lasts[j]) + jnp.where(bd_mask, upds[j], 0.0)
            st_sc[...] = st
            return jnp.int32(0)

        g_min = cums[0][C:]
        for j in range(1, HG_GROUP):
            g_min = jnp.minimum(g_min, cums[j][C:])
        lax.cond(jnp.min(g_min) < -EXP_CLAMP,
                 functools.partial(finish, True), functools.partial(finish, False))
        return carry

    lax.fori_loop(0, R // (C * HG_GROUP), group, 0)


def _hgrn(rec):
    B, S, _ = rec.shape
    R = HG_ROWS
    return pl.pallas_call(
        _hgrn_kernel,
        grid=(B, S // R),
        in_specs=[pl.BlockSpec((1, R, 4 * HG_WIDTH), lambda b, s: (b, s, 0))],
        out_specs=pl.BlockSpec((1, R, HG_WIDTH), lambda b, s: (b, s, 0)),
        out_shape=jax.ShapeDtypeStruct((B, S, HG_WIDTH), F32),
        scratch_shapes=[pltpu.VMEM((HG_WIDTH, HG_WIDTH), F32)],
        compiler_params=pltpu.CompilerParams(
            dimension_semantics=("parallel", "arbitrary"), vmem_limit_bytes=VMEM_LIMIT),
        name="hgrn",
    )(rec)


def _mem_kv_kernel(mem_ref, g_ref, w_ref, gk_ref, k_out, v_out):
    m = mem_ref[0]
    mn = m * _rms(m) * g_ref[0]
    kv = _dot(mn.astype(BF16), w_ref[0])
    hw = X_HEADS * X_HEAD_DIM
    gk = gk_ref[0]
    for hd in range(X_HEADS):
        sl = slice(hd * X_HEAD_DIM, (hd + 1) * X_HEAD_DIM)
        kh = kv[:, sl]
        k_out[0, 0, :, sl] = (kh * _rms(kh) * gk).astype(BF16)
    v_out[0, 0] = kv[:, hw:].astype(BF16)


def _mem_kv(mem, mem_norm_g, w_xkv, xk_norm_g):
    B, M, _ = mem.shape
    L = w_xkv.shape[0]
    hw = X_HEADS * X_HEAD_DIM
    out_spec = pl.BlockSpec((1, 1, M, hw), lambda l, b: (l, b, 0, 0))
    return pl.pallas_call(
        _mem_kv_kernel,
        grid=(L, B),
        in_specs=[
            pl.BlockSpec((1, M, D_MODEL), lambda l, b: (b, 0, 0)),
            pl.BlockSpec((1, 1, D_MODEL), lambda l, b: (l, 0, 0)),
            pl.BlockSpec((1, D_MODEL, 2 * hw), lambda l, b: (l, 0, 0)),
            pl.BlockSpec((1, 1, X_HEAD_DIM), lambda l, b: (l, 0, 0)),
        ],
        out_specs=[out_spec, out_spec],
        out_shape=[jax.ShapeDtypeStruct((L, B, M, hw), BF16)] * 2,
        compiler_params=pltpu.CompilerParams(
            dimension_semantics=("parallel", "parallel"), vmem_limit_bytes=VMEM_LIMIT),
        name="mem_kv",
    )(mem, mem_norm_g, w_xkv, xk_norm_g)


def _out_xattn_kernel(x_ref, ya_ref, rec_ref, post_ref, onorm_ref, convw_ref, wo_ref, g_ref,
                      wq_ref, gq_ref, k_ref, v_ref, wxo_ref, o_ref, ubuf_sc):
    half = wo_ref.shape[0] // 2
    tm = x_ref.shape[1]
    W = HG_WIDTH
    subs = [slice(j * XATTN_SUB, (j + 1) * XATTN_SUB) for j in range(tm // XATTN_SUB)]

    rw = lax.broadcasted_iota(jnp.int32, (W, W), 0)
    cw = lax.broadcasted_iota(jnp.int32, (W, W), 1)
    dk_shift = HG_DK.bit_length() - 1
    ones_bd = jnp.where((rw >> dk_shift) == (cw >> dk_shift), 1.0, 0.0).astype(BF16)
    @pl.when(pl.program_id(1) == 0)
    def _():
        ubuf_sc[0:8, :] = jnp.zeros((8, CONV_WIDTH), F32)

    ubuf_sc[8:8 + tm, :] = post_ref[0, :, W + CONV_WIDTH:W + 2 * CONV_WIDTH]
    wc = convw_ref[...]
    ybs = []
    for r in subs:
        o = rec_ref[0, r, :]
        ms = _dot((o * o).astype(BF16), ones_bd) * (1.0 / HG_DK)
        y_hg = o * lax.rsqrt(ms + EPS) * onorm_ref[...] * post_ref[0, r, 0:W]
        u0, u1, u2 = (ubuf_sc[8 - d + r.start:8 - d + r.stop, :] for d in range(CONV_K))
        y_cv = post_ref[0, r, W:W + CONV_WIDTH] * (
            u2 * wc[0:1, :] + u1 * wc[1:2, :] + u0 * wc[2:3, :])
        ybs.append(jnp.concatenate([y_hg, y_cv], axis=1).astype(BF16))
    ubuf_sc[0:8, :] = ubuf_sc[tm:tm + 8, :]

    slots = [slice(hd * X_HEAD_DIM, (hd + 1) * X_HEAD_DIM) for hd in range(X_HEADS)]
    gq = gq_ref[...] * (1.0 / math.sqrt(X_HEAD_DIM))
    x1s = [x_ref[0, r, :] + _dot(ya_ref[0, r, :], wo_ref[:half, :])
           + _dot(yb, wo_ref[half:, :]) for r, yb in zip(subs, ybs)]
    hs = [(x1 * _rms(x1) * g_ref[...]).astype(BF16) for x1 in x1s]
    qs = [_dot(h, wq_ref[...]) for h in hs]
    qhs = [[(q[:, sl] * _rms(q[:, sl]) * gq).astype(BF16) for sl in slots] for q in qs]
    ss = [[_dot_nt(qh, k_ref[0, 0, :, sl]) for qh, sl in zip(row, slots)] for row in qhs]
    ps, ls = [], []
    for row in ss:
        p_row, l_row = [], []
        for s in row:
            p = jnp.exp(s - jnp.max(s, axis=1, keepdims=True))
            l_row.append(jnp.sum(p, axis=1, keepdims=True))
            p_row.append(p.astype(BF16))
        ps.append(p_row)
        ls.append(l_row)
    os_ = [jnp.concatenate([_dot(p, v_ref[0, 0, :, sl]) / l
                            for p, l, sl in zip(p_row, l_row, slots)], axis=1).astype(BF16)
           for p_row, l_row in zip(ps, ls)]
    for r, x1, o in zip(subs, x1s, os_):
        o_ref[0, r, :] = x1 + _dot(o, wxo_ref[...])


def _out_xattn(x, y_mla, o_rec, post, onorm_g, conv_w, w_out, g, w_xq, gq, k_mem, v_mem, w_xo,
               layer):
    B, S, _ = x.shape
    tm = ROW_TILE
    hw = X_HEADS * X_HEAD_DIM
    const = lambda b, s: (0, 0)
    row = lambda w: pl.BlockSpec((1, tm, w), lambda b, s: (b, s, 0))
    mem_spec = pl.BlockSpec((1, 1, MEM_LEN, hw), lambda b, s: (layer, b, 0, 0))
    return pl.pallas_call(
        _out_xattn_kernel,
        grid=(B, S // tm),
        in_specs=[
            row(D_MODEL), row(y_mla.shape[-1]), row(o_rec.shape[-1]), row(post.shape[-1]),
            pl.BlockSpec((1, HG_WIDTH), const),
            pl.BlockSpec((CONV_K, CONV_WIDTH), const),
            pl.BlockSpec((D_MODEL, D_MODEL), const),
            pl.BlockSpec((1, D_MODEL), const),
            pl.BlockSpec((D_MODEL, hw), const),
            pl.BlockSpec((1, X_HEAD_DIM), const),
            mem_spec, mem_spec,
            pl.BlockSpec((hw, D_MODEL), const),
        ],
        out_specs=row(D_MODEL),
        out_shape=jax.ShapeDtypeStruct((B, S, D_MODEL), F32),
        scratch_shapes=[pltpu.VMEM((tm + 8, CONV_WIDTH), F32)],
        compiler_params=pltpu.CompilerParams(
            dimension_semantics=("parallel", "arbitrary"), vmem_limit_bytes=VMEM_LIMIT),
        name="out_xattn",
    )(x, y_mla, o_rec, post, onorm_g, conv_w, w_out, g, w_xq, gq, k_mem, v_mem, w_xo)


def _mlp_kernel(x_ref, g_ref, wu_ref, wd_ref, o_ref):
    x = x_ref[0]
    h = (x * _rms(x) * g_ref[...]).astype(BF16)
    acc = x
    step = D_MODEL
    for c in range(D_FF // step):
        u = _dot(h, wu_ref[:, c * step:(c + 1) * step])
        a = jnp.square(jnp.maximum(u, 0.0)).astype(BF16)
        acc = acc + _dot(a, wd_ref[c * step:(c + 1) * step, :])
    o_ref[0] = acc


def _mlp(x, g, w_up, w_down):
    B, S, _ = x.shape
    tm = MLP_TILE
    const = lambda b, s: (0, 0)
    row = pl.BlockSpec((1, tm, D_MODEL), lambda b, s: (b, s, 0))
    once = pl.Buffered(1)
    return pl.pallas_call(
        _mlp_kernel,
        grid=(B, S // tm),
        in_specs=[row, pl.BlockSpec((1, D_MODEL), const),
                  pl.BlockSpec((D_MODEL, D_FF), const, pipeline_mode=once),
                  pl.BlockSpec((D_FF, D_MODEL), const, pipeline_mode=once)],
        out_specs=row,
        out_shape=jax.ShapeDtypeStruct((B, S, D_MODEL), F32),
        compiler_params=pltpu.CompilerParams(
            dimension_semantics=("parallel", "parallel"), vmem_limit_bytes=VMEM_LIMIT),
        name="mlp",
    )(x, g, w_up, w_down)


def _scatter_lanes(w_cols):
    idx = jnp.asarray(np.where(_LANE_MAP >= 0, _LANE_MAP, 0), jnp.int32)
    valid = jnp.asarray(_LANE_MAP >= 0)
    return jnp.where(valid, jnp.take(w_cols, idx, axis=-1), 0.0)


def _prep_layer(l, positions, w_in, w_uq, w_ukv, mla_qn_g, mla_kn_g):
    o_kr = MLA_Q_RANK + MLA_KV_RANK
    w = w_in[l]
    kr_src = jnp.concatenate(
        [jnp.zeros((D_MODEL, MLA_NOPE), F32), w[:, o_kr:o_kr + MLA_ROPE]], axis=1)
    w_in_p = jnp.concatenate(
        [w[:, :o_kr], _scatter_lanes(kr_src), w[:, o_kr + MLA_ROPE:]], axis=1).astype(BF16)

    def with_swapped_halves(w_slots):
        rank = w_slots.shape[0]
        both = jnp.concatenate([w_slots, jnp.roll(w_slots, HEAD_SLOT // 2, axis=-1)], axis=1)
        return both.reshape(rank, 2 * MLA_HEADS * HEAD_SLOT).astype(BF16)

    wq = w_uq[l].reshape(MLA_Q_RANK, MLA_HEADS, MLA_QK)
    w_uq_p = with_swapped_halves(_scatter_lanes(wq))

    wkv = w_ukv[l].reshape(MLA_KV_RANK, MLA_HEADS, MLA_NOPE + MLA_V)
    k_src = jnp.concatenate(
        [wkv[..., :MLA_NOPE], jnp.zeros((MLA_KV_RANK, MLA_HEADS, MLA_ROPE), F32)], axis=-1)
    w_uk_p = with_swapped_halves(_scatter_lanes(k_src))
    w_vt = wkv[..., MLA_NOPE:].reshape(MLA_KV_RANK, MLA_HEADS * MLA_V).T.astype(BF16)

    inv_freq = ROPE_BASE ** (-jnp.arange(0, MLA_ROPE, 2, dtype=F32) / MLA_ROPE)
    ang = positions.astype(F32)[:, None] * inv_freq[None, :]
    cos, sin = jnp.cos(ang), jnp.sin(ang)
    S = positions.shape[0]
    pad = HEAD_SLOT // 2 - ROPE_HALF
    c_tab = jnp.concatenate([cos, jnp.ones((S, pad), F32), cos, jnp.ones((S, pad), F32)], axis=1)
    s_tab = jnp.concatenate([-sin, jnp.zeros((S, pad), F32), sin, jnp.zeros((S, pad), F32)], axis=1)

    def tables(gain, scale):
        g = _scatter_lanes(gain)[None, :] * scale
        return c_tab * g, s_tab * jnp.roll(g, HEAD_SLOT // 2, axis=1)

    cq_t, sq_t = tables(mla_qn_g[l], math.log2(math.e) / math.sqrt(MLA_QK))
    ck_t, sk_t = tables(mla_kn_g[l], 1.0)
    return w_in_p, w_uq_p, w_uk_p, w_vt, (cq_t, sq_t, ck_t, sk_t)


def kernel(x, mem, positions, mix_norm_g, w_in, mla_q_norm_g, mla_kv_norm_g, w_uq, w_ukv,
           mla_qn_g, mla_kn_g, hgrn_lb_logits, hgrn_o_norm_g, conv_w, w_out,
           xattn_norm_g, mem_norm_g, w_xq, w_xkv, xq_norm_g, xk_norm_g, w_xo,
           mlp_norm_g, w_up, w_down):
    B, S, _ = x.shape
    L = w_in.shape[0]
    nq, nk = S // ATT_TQ, S // ATT_TK
    pq = positions.reshape(nq, ATT_TQ)
    pk = positions.reshape(nk, ATT_TK)
    vis = jnp.min(pk, axis=1)[None, :] <= jnp.max(pq, axis=1)[:, None]
    n_kv = jnp.max(jnp.where(vis, jnp.arange(1, nk + 1, dtype=jnp.int32)[None, :], 0),
                   axis=1).astype(jnp.int32)
    full = jnp.max(pk, axis=1)[None, :] <= jnp.min(pq, axis=1)[:, None]
    n_full = jnp.sum(jnp.cumprod(full.astype(jnp.int32), axis=1), axis=1).astype(jnp.int32)
    posq = positions.reshape(nq, 1, ATT_TQ)
    posk = positions.reshape(S, 1)

    k_mem, v_mem = _mem_kv(mem, mem_norm_g.reshape(L, 1, D_MODEL), w_xkv.astype(BF16),
                           xk_norm_g.reshape(L, 1, X_HEAD_DIM))
    for l in range(L):
        w_in_p, w_uq_p, w_uk_p, w_vt, tabs = _prep_layer(l, positions, w_in, w_uq, w_ukv,
                                                         mla_qn_g, mla_kn_g)
        lat, rec, post = _in_proj(x, mix_norm_g[l][None], w_in_p, hgrn_lb_logits, l)
        q, k, vt = _mla_qkv(lat, mla_q_norm_g[l][None], w_uq_p, mla_kv_norm_g[l][None],
                            w_uk_p, w_vt, tabs)
        score_bound = (math.log2(math.e) * math.sqrt(MLA_QK)
                       * jnp.max(jnp.abs(mla_qn_g[l])) * jnp.max(jnp.abs(mla_kn_g[l])))
        y_mla = lax.cond(score_bound <= ATT_PLAIN_MAX_SCORE,
                         functools.partial(_attention, online=False),
                         functools.partial(_attention, online=True),
                         q, k, vt, n_full, n_kv, posq, posk)
        o_rec = _hgrn(rec)
        x = _out_xattn(x, y_mla, o_rec, post, hgrn_o_norm_g[l][None], conv_w[l],
                       w_out[l].astype(BF16), xattn_norm_g[l][None], w_xq[l].astype(BF16),
                       xq_norm_g[l][None], k_mem, v_mem, w_xo[l].astype(BF16), l)
        x = _mlp(x, mlp_norm_g[l][None], w_up[l].astype(BF16), w_down[l].astype(BF16))
    return x
```

```python
import functools
import math

import numpy as np
import jax
import jax.numpy as jnp
from jax import lax
from jax.experimental import pallas as pl
from jax.experimental.pallas import tpu as pltpu

F32 = jnp.float32
BF16 = jnp.bfloat16

D_MODEL = 1024
DEPTH = 4
MEM_LEN = 256
EPS = 1e-6

MLA_HEADS = 8
MLA_NOPE = 64
MLA_ROPE = 32
MLA_V = 64
MLA_QK = MLA_NOPE + MLA_ROPE
MLA_Q_RANK = 384
MLA_KV_RANK = 256
ROPE_BASE = 10000.0

HG_HEADS = 4
HG_DK = 64
HG_CHUNK = 64
HG_SUB = 16
HG_GROUP = 8
HG_WIDTH = 256
CONV_WIDTH = 256
CONV_K = 3

X_HEADS = 4
X_HEAD_DIM = 128
D_FF = 4 * D_MODEL

LANES = 128
HEAD_SLOT = LANES
ROPE_HALF = MLA_ROPE // 2
N_IN_PAD = MLA_Q_RANK + MLA_KV_RANK + HEAD_SLOT + 4 * HG_WIDTH + 3 * CONV_WIDTH
HC_OFF = MLA_Q_RANK + MLA_KV_RANK + HEAD_SLOT
HC_WIDTH = N_IN_PAD - HC_OFF

ROW_TILE = 1024
MLP_TILE = 1024
IN_PROJ_SUB = 256
XATTN_SUB = 256
ATT_TQ = 256
ATT_TK = 256
ATT_HEADS = 8
ATT_CHAINS = 4
ATT_UNROLLS = (4, 2)
HG_ROWS = 1024
MASK_VALUE = -1e30
ATT_PLAIN_MAX_SCORE = 64.0
EXP_CLAMP = 60.0
VMEM_LIMIT = 56 * 1024 * 1024


def _head_lane_map():
    m = -np.ones((HEAD_SLOT,), np.int64)
    m[0:ROPE_HALF] = MLA_NOPE + np.arange(ROPE_HALF)
    m[ROPE_HALF:64] = np.arange(64 - ROPE_HALF)
    m[64:64 + ROPE_HALF] = MLA_NOPE + ROPE_HALF + np.arange(ROPE_HALF)
    m[64 + ROPE_HALF:96] = (64 - ROPE_HALF) + np.arange(ROPE_HALF)
    return m


_LANE_MAP = _head_lane_map()


def _rms(x, n=None):
    n = x.shape[-1] if n is None else n
    return lax.rsqrt(jnp.sum(x * x, axis=-1, keepdims=True) * (1.0 / n) + EPS)


def _sigmoid(x):
    return 1.0 / (1.0 + jnp.exp(-x))


def _dot(a, b):
    return jnp.dot(a, b, preferred_element_type=F32)


def _dot_nt(a, b):
    return lax.dot_general(a, b, (((1,), (1,)), ((), ())), preferred_element_type=F32)


def _dot_tn(a, b):
    return lax.dot_general(a, b, (((0,), (0,)), ((), ())), preferred_element_type=F32)


def _in_proj_kernel(x_ref, g_ref, w_in_ref, lbl_ref, lat_out, rec_out, post_out, *, layer):
    W = HG_WIDTH
    subs = [slice(j * IN_PROJ_SUB, (j + 1) * IN_PROJ_SUB)
            for j in range(x_ref.shape[1] // IN_PROJ_SUB)]
    hs = []
    for r in subs:
        x = x_ref[0, r, :]
        hs.append((x * _rms(x) * g_ref[...]).astype(BF16))
    hcs = []
    for r, h in zip(subs, hs):
        lat_out[0, r, :] = _dot(h, w_in_ref[:, :HC_OFF])
        hcs.append(_dot(h, w_in_ref[:, HC_OFF:]))

    lg = lbl_ref[...]
    e = jnp.exp(lg - jnp.max(lg, axis=0, keepdims=True))
    soft = e / jnp.sum(e, axis=0, keepdims=True)
    lb = jnp.zeros((1, W), F32)
    for i in range(1, layer + 1):
        lb = lb + soft[i:i + 1, :]
    lb = jnp.maximum(lb, 0.0)
    log_lb = jnp.log(lb)
    log_1m = jnp.log1p(-lb)

    for rows, hc in zip(subs, hcs):
        zq, zf, vi, zg = (hc[:, i * W:(i + 1) * W] for i in range(4))
        cb, cc, cx = (hc[:, 4 * W + i * CONV_WIDTH:4 * W + (i + 1) * CONV_WIDTH]
                      for i in range(3))
        t = jnp.exp(-jnp.abs(zf))
        d = 1.0 + t
        r = 1.0 / d
        log_sig = jnp.minimum(zf, 0.0) - jnp.log(d)
        c2 = log_1m + log_sig
        log_f = jnp.maximum(log_lb, c2) + jnp.log(1.0 + jnp.exp(-jnp.abs(log_lb - c2)))
        rec_out[0, rows, 0:W] = zq * _sigmoid(zq)
        rec_out[0, rows, W:2 * W] = log_f
        rec_out[0, rows, 2 * W:3 * W] = (1.0 - lb) * jnp.where(zf >= 0.0, t * r, r)
        rec_out[0, rows, 3 * W:4 * W] = vi
        post_out[0, rows, 0:W] = zg * _sigmoid(zg)
        post_out[0, rows, W:W + CONV_WIDTH] = cb
        post_out[0, rows, W + CONV_WIDTH:W + 2 * CONV_WIDTH] = cc * cx


def _in_proj(x, g, w_in, lb_logits, layer):
    B, S, _ = x.shape
    tm = ROW_TILE
    const = lambda b, s: (0, 0)
    widths = (HC_OFF, 4 * HG_WIDTH, HG_WIDTH + 2 * CONV_WIDTH)
    return pl.pallas_call(
        functools.partial(_in_proj_kernel, layer=layer),
        grid=(B, S // tm),
        in_specs=[
            pl.BlockSpec((1, tm, D_MODEL), lambda b, s: (b, s, 0)),
            pl.BlockSpec((1, D_MODEL), const),
            pl.BlockSpec((D_MODEL, N_IN_PAD), const),
            pl.BlockSpec((DEPTH, HG_WIDTH), const),
        ],
        out_specs=[pl.BlockSpec((1, tm, w), lambda b, s: (b, s, 0)) for w in widths],
        out_shape=[jax.ShapeDtypeStruct((B, S, w), F32) for w in widths],
        compiler_params=pltpu.CompilerParams(
            dimension_semantics=("parallel", "parallel"), vmem_limit_bytes=VMEM_LIMIT),
        name="in_proj",
    )(x, g, w_in, lb_logits)


def _mla_qkv_kernel(lat_ref, gq_ref, w_uq_ref, gkv_ref, w_uk_ref, w_vt_ref,
                    cq_tab, sq_tab, ck_tab, sk_tab, q_out, k_out, vt_out):
    hw = MLA_HEADS * HEAD_SLOT
    subs = [slice(j * ATT_TK, (j + 1) * ATT_TK) for j in range(vt_out.shape[1])]
    lats = []
    for r in subs:
        cq = lat_ref[0, r, :MLA_Q_RANK]
        ckv = lat_ref[0, r, MLA_Q_RANK:MLA_Q_RANK + MLA_KV_RANK]
        lats.append(((cq * _rms(cq) * gq_ref[...]).astype(BF16),
                     (ckv * _rms(ckv) * gkv_ref[...]).astype(BF16)))
    qfs = [_dot(cqn, w_uq_ref[...]) for cqn, _ in lats]
    kfs = [_dot(ckvn, w_uk_ref[...]) for _, ckvn in lats]
    for j, (_, ckvn) in enumerate(lats):
        vt_out[0, j] = _dot_nt(w_vt_ref[...], ckvn).astype(BF16)
    for r, qf, kf in zip(subs, qfs, kfs):
        kr = lat_ref[0, r, MLA_Q_RANK + MLA_KV_RANK:]
        kr_sw = pltpu.roll(kr, HEAD_SLOT // 2, 1)
        cq_t, sq_t, ck_t, sk_t = cq_tab[r, :], sq_tab[r, :], ck_tab[r, :], sk_tab[r, :]
        for hd in range(MLA_HEADS):
            sl = slice(hd * HEAD_SLOT, (hd + 1) * HEAD_SLOT)
            sw = slice(hw + hd * HEAD_SLOT, hw + (hd + 1) * HEAD_SLOT)
            qh = qf[:, sl]
            qh = (qh * cq_t + qf[:, sw] * sq_t) * _rms(qh, MLA_QK)
            q_out[0, hd, :, r] = qh.T.astype(BF16)
            kh = kf[:, sl] + kr
            kh = (kh * ck_t + (kf[:, sw] + kr_sw) * sk_t) * _rms(kh, MLA_QK)
            k_out[0, hd, r, :] = kh.astype(BF16)


def _mla_qkv(lat, gq, w_uq, gkv, w_uk, w_vt, tabs):
    B, S, _ = lat.shape
    tm = ROW_TILE
    vw = MLA_HEADS * MLA_V
    const = lambda b, s: (0, 0)
    tab_spec = pl.BlockSpec((tm, HEAD_SLOT), lambda b, s: (s, 0))
    return pl.pallas_call(
        _mla_qkv_kernel,
        grid=(B, S // tm),
        in_specs=[
            pl.BlockSpec((1, tm, HC_OFF), lambda b, s: (b, s, 0)),
            pl.BlockSpec((1, MLA_Q_RANK), const),
            pl.BlockSpec((MLA_Q_RANK, 2 * MLA_HEADS * HEAD_SLOT), const),
            pl.BlockSpec((1, MLA_KV_RANK), const),
            pl.BlockSpec((MLA_KV_RANK, 2 * MLA_HEADS * HEAD_SLOT), const),
            pl.BlockSpec((vw, MLA_KV_RANK), const),
            tab_spec, tab_spec, tab_spec, tab_spec,
        ],
        out_specs=[
            pl.BlockSpec((1, MLA_HEADS, HEAD_SLOT, tm), lambda b, s: (b, 0, 0, s)),
            pl.BlockSpec((1, MLA_HEADS, tm, HEAD_SLOT), lambda b, s: (b, 0, s, 0)),
            pl.BlockSpec((1, tm // ATT_TK, vw, ATT_TK), lambda b, s: (b, s, 0, 0)),
        ],
        out_shape=[
            jax.ShapeDtypeStruct((B, MLA_HEADS, HEAD_SLOT, S), BF16),
            jax.ShapeDtypeStruct((B, MLA_HEADS, S, HEAD_SLOT), BF16),
            jax.ShapeDtypeStruct((B, S // ATT_TK, vw, ATT_TK), BF16),
        ],
        compiler_params=pltpu.CompilerParams(
            dimension_semantics=("parallel", "parallel"), vmem_limit_bytes=VMEM_LIMIT),
        name="mla_qkv",
    )(lat, gq, w_uq, gkv, w_uk, w_vt, *tabs)


def _attn_kernel(nfull_ref, nkv_ref, q_ref, k_ref, vt_ref, posq_ref, posk_ref, o_ref,
                 p_sc, acc_sc, *maybe_s_sc, online):
    qi = pl.program_id(2)
    n_full = nfull_ref[qi]
    n_kv = nkv_ref[qi]
    tq = q_ref.shape[3]
    nk = vt_ref.shape[1]
    heads = range(ATT_HEADS)
    qts = [q_ref[0, hd] for hd in heads]
    posq = posq_ref[0]

    groups = [list(heads)[i:i + ATT_CHAINS] for i in range(0, ATT_HEADS, ATT_CHAINS)]

    def score(ki, hd):
        off = pl.multiple_of(ki * ATT_TK, ATT_TK)
        return _dot(k_ref[0, hd, pl.ds(off, ATT_TK), :], qts[hd])

    def pv(ki, hd, p=None):
        p = p_sc[hd] if p is None else p
        return _dot(vt_ref[0, ki, hd * MLA_V:(hd + 1) * MLA_V, :], p)

    def issue(ki_scores, ki_pv):
        ss, pvs = {}, {}
        for g in groups:
            for hd in g:
                ss[hd] = score(ki_scores, hd)
            for hd in g:
                pvs[hd] = pv(ki_pv, hd)
        return ss, pvs

    def mask_of(ki):
        off = pl.multiple_of(ki * ATT_TK, ATT_TK)
        return posk_ref[pl.ds(off, ATT_TK), :] <= posq

    def step_plain(ki, carry, masked):
        ss, pvs = issue(ki, jnp.maximum(ki - 1, 0))
        if masked:
            mask = mask_of(ki)
        new = []
        for hd in heads:
            s = jnp.where(mask, ss[hd], MASK_VALUE) if masked else ss[hd]
            p = jnp.exp2(s)
            new.append(carry[hd] + jnp.sum(p, axis=0, keepdims=True))
            acc_sc[hd] = acc_sc[hd] + pvs[hd]
            p_sc[hd] = p.astype(BF16)
        return tuple(new)

    def multi_plain(j, carry, n_tiles, first):
        k0 = first + n_tiles * j
        ss, pvs = issue(k0, jnp.maximum(k0 - 1, 0))
        l_add = {hd: 0.0 for hd in heads}
        for hd in heads:
            acc_sc[hd] = acc_sc[hd] + pvs[hd]
        for u in range(1, n_tiles):
            ss_next = {}
            for g in groups:
                for hd in g:
                    ss_next[hd] = score(k0 + u, hd)
                for hd in g:
                    p = jnp.exp2(ss[hd])
                    l_add[hd] = l_add[hd] + jnp.sum(p, axis=0, keepdims=True)
                    acc_sc[hd] = acc_sc[hd] + pv(k0 + u - 1, hd, p.astype(BF16))
            ss = ss_next
        new = []
        for hd in heads:
            p = jnp.exp2(ss[hd])
            new.append(carry[hd] + l_add[hd] + jnp.sum(p, axis=0, keepdims=True))
            p_sc[hd] = p.astype(BF16)
        return tuple(new)

    def step_online(ki, carry, masked):
        (s_sc,) = maybe_s_sc
        s_next, pvs = issue(jnp.minimum(ki + 1, nk - 1), jnp.maximum(ki - 1, 0))
        if masked:
            mask = mask_of(ki)
        new = []
        for hd in heads:
            m_old, l_old = carry[hd]
            s = s_sc[hd]
            if masked:
                s = jnp.where(mask, s, MASK_VALUE)
            m_new = jnp.maximum(m_old, jnp.max(s, axis=0, keepdims=True))
            alpha = jnp.exp2(m_old - m_new)
            p = jnp.exp2(s - m_new)
            l_new = alpha * l_old + jnp.sum(p, axis=0, keepdims=True)
            acc_sc[hd] = alpha * (acc_sc[hd] + pvs[hd])
            p_sc[hd] = p.astype(BF16)
            new.append((m_new, l_new))
        for hd in heads:
            s_sc[hd] = s_next[hd]
        return tuple(new)

    p_sc[...] = jnp.zeros(p_sc.shape, BF16)
    acc_sc[...] = jnp.zeros(acc_sc.shape, F32)
    if online:
        for hd in heads:
            maybe_s_sc[0][hd] = score(0, hd)
        init = tuple((jnp.full((1, tq), MASK_VALUE, F32), jnp.zeros((1, tq), F32))
                     for _ in heads)
        step = step_online
    else:
        init = tuple(jnp.zeros((1, tq), F32) for _ in heads)
        step = step_plain
    n_done = 0
    if not online:
        for n_tiles in ATT_UNROLLS:
            n_steps = (n_full - n_done) // n_tiles
            init = lax.fori_loop(
                0, n_steps, functools.partial(multi_plain, n_tiles=n_tiles, first=n_done), init)
            n_done = n_done + n_tiles * n_steps
    carry = lax.fori_loop(n_done, n_full, functools.partial(step, masked=False), init)
    carry = lax.fori_loop(n_full, n_kv, functools.partial(step, masked=True), carry)
    last = jnp.maximum(n_kv - 1, 0)
    ls = [c[1] for c in carry] if online else carry
    o_t = jnp.concatenate([(acc_sc[hd] + pv(last, hd)) / ls[hd] for hd in heads], axis=0)
    o_ref[0] = o_t.T.astype(BF16)


def _attention(q, k, vt, n_full, n_kv, posq, posk, online):
    B, H, _, S = q.shape
    tq = ATT_TQ
    nh = ATT_HEADS
    nk = S // ATT_TK
    scratch = [pltpu.VMEM((nh, ATT_TK, tq), BF16), pltpu.VMEM((nh, MLA_V, tq), F32)]
    if online:
        scratch.append(pltpu.VMEM((nh, ATT_TK, tq), F32))
    grid_spec = pltpu.PrefetchScalarGridSpec(
        num_scalar_prefetch=2,
        grid=(B, H // nh, S // tq),
        in_specs=[
            pl.BlockSpec((1, nh, HEAD_SLOT, tq), lambda b, hp, qi, *_: (b, hp, 0, qi)),
            pl.BlockSpec((1, nh, S, HEAD_SLOT), lambda b, hp, qi, *_: (b, hp, 0, 0)),
            pl.BlockSpec((1, nk, nh * MLA_V, ATT_TK), lambda b, hp, qi, *_: (b, 0, hp, 0)),
            pl.BlockSpec((1, 1, tq), lambda b, hp, qi, *_: (qi, 0, 0)),
            pl.BlockSpec((S, 1), lambda b, hp, qi, *_: (0, 0)),
        ],
        out_specs=pl.BlockSpec((1, tq, nh * MLA_V), lambda b, hp, qi, *_: (b, qi, hp)),
        scratch_shapes=scratch,
    )
    return pl.pallas_call(
        functools.partial(_attn_kernel, online=online),
        grid_spec=grid_spec,
        out_shape=jax.ShapeDtypeStruct((B, S, H * MLA_V), BF16),
        compiler_params=pltpu.CompilerParams(
            dimension_semantics=("parallel", "parallel", "arbitrary"),
            vmem_limit_bytes=VMEM_LIMIT),
        name="mla_attention_online" if online else "mla_attention",
    )(n_full, n_kv, q, k, vt, posq, posk)


def _hgrn_kernel(rec_ref, o_ref, st_sc):
    sb = pl.program_id(1)
    R = HG_ROWS
    C = HG_CHUNK
    W = HG_WIDTH

    @pl.when(sb == 0)
    def _():
        st_sc[...] = jnp.zeros(st_sc.shape, F32)

    ri = lax.broadcasted_iota(jnp.int32, (C, C), 0)
    ci = lax.broadcasted_iota(jnp.int32, (C, C), 1)
    sub_shift = HG_SUB.bit_length() - 1
    dk_shift = HG_DK.bit_length() - 1
    tri = jnp.where(ci <= ri, 1.0, 0.0)
    tri_blk = jnp.where((ci >> sub_shift) == (ri >> sub_shift), tri, 0.0)
    cum_mat = jnp.concatenate([tri, tri_blk], axis=0).astype(BF16)
    tt = ri & (HG_SUB - 1)
    lane_w = lax.broadcasted_iota(jnp.int32, (1, W), 1)
    head_masks = [jnp.where((lane_w >> dk_shift) == hh, 1.0, 0.0) for hh in range(HG_HEADS)]
    rw = lax.broadcasted_iota(jnp.int32, (W, W), 0)
    cw = lax.broadcasted_iota(jnp.int32, (W, W), 1)
    bd_mask = (rw >> dk_shift) == (cw >> dk_shift)
    ones_bd = jnp.where(bd_mask, 1.0, 0.0).astype(BF16)
    n_sub = C // HG_SUB

    def group(gi, carry):
        base = gi * (HG_GROUP * C)
        starts = [pl.multiple_of(base + j * C, C) for j in range(HG_GROUP)]
        chunks = range(HG_GROUP)
        gates, splits = [], []
        for r0 in starts:
            q = rec_ref[0, pl.ds(r0, C), 0:W]
            log_f = rec_ref[0, pl.ds(r0, C), W:2 * W]
            kk = rec_ref[0, pl.ds(r0, C), 2 * W:3 * W]
            vi = rec_ref[0, pl.ds(r0, C), 3 * W:4 * W]
            f_hi = log_f.astype(BF16)
            r1 = log_f - f_hi.astype(F32)
            f_mid = r1.astype(BF16)
            f_lo = (r1 - f_mid.astype(F32)).astype(BF16)
            gates.append((q, kk, vi.astype(BF16)))
            splits.append((f_hi, f_mid, f_lo))
        cums = [_dot(cum_mat, s[0]) + _dot(cum_mat, s[1]) + _dot(cum_mat, s[2]) for s in splits]

        def diag_direct(j, i):
            q, kk, vb = gates[j]
            rows = slice(i * HG_SUB, (i + 1) * HG_SUB)
            b_t, q_t, k_t = cums[j][:C][rows], q[rows], kk[rows]
            v_t = vb[rows].astype(F32)
            t_idx = lax.broadcasted_iota(jnp.int32, (HG_SUB, W), 0)
            slabs = []
            for s in range(HG_SUB):
                w = jnp.exp(jnp.minimum(b_t - b_t[s:s + 1, :], 0.0))
                slabs.append(jnp.where(t_idx >= s, q_t * k_t[s:s + 1, :] * w, 0.0))
            x = jnp.concatenate(slabs, axis=0).astype(BF16)
            a = _dot(x, ones_bd)
            o = a[0:HG_SUB] * v_t[0:1, :]
            for s in range(1, HG_SUB):
                o = o + a[s * HG_SUB:(s + 1) * HG_SUB] * v_t[s:s + 1, :]
            return o

        def finish(direct_diag):
            qbs, b_lasts, lhss, kds, kdls = [], [], [], [], []
            for j in chunks:
                q, kk, _ = gates[j]
                b = cums[j][:C]
                g = cums[j][C:]
                b_last = b[C - 1:C, :]
                rref = b - g
                qd = q * jnp.exp(g)
                qbs.append((q * jnp.exp(b)).astype(BF16))
                b_lasts.append(b_last)
                kdls.append((kk * jnp.exp(b_last - b)).astype(BF16))
                lhs_j, kd_j = [], []
                for i in range(n_sub):
                    rr = rref[i * HG_SUB:i * HG_SUB + 1, :]
                    kd_j.append((kk * jnp.exp(jnp.minimum(rr - b, EXP_CLAMP))).astype(BF16))
                    qi = qd[i * HG_SUB:(i + 1) * HG_SUB]
                    lhs_j.append(jnp.concatenate([qi * hm for hm in head_masks],
                                                 axis=0).astype(BF16))
                lhss.append(lhs_j)
                kds.append(kd_j)
            attn = [[_dot_nt(lhss[j][i], kds[j][i]) for i in range(n_sub)] for j in chunks]
            upds = [_dot_tn(gates[j][2], kdls[j]) for j in chunks]
            if direct_diag:
                attn = [[jnp.where(ci < i * HG_SUB, attn[j][i], 0.0).astype(BF16)
                         for i in range(n_sub)] for j in chunks]
            else:
                attn = [[jnp.where(ci <= i * HG_SUB + tt, attn[j][i], 0.0).astype(BF16)
                         for i in range(n_sub)] for j in chunks]
            pvs = [[_dot(attn[j][i], gates[j][2]) for i in range(n_sub)] for j in chunks]
            st = st_sc[...]
            for j in chunks:
                o_parts = []
                for i in range(n_sub):
                    pv = pvs[j][i]
                    oi = pv[0:HG_SUB] * head_masks[0]
                    for hh in range(1, HG_HEADS):
                        oi = oi + pv[hh * HG_SUB:(hh + 1) * HG_SUB] * head_masks[hh]
                    if direct_diag:
                        oi = oi + diag_direct(j, i)
                    o_parts.append(oi)
                o_inter = _dot_nt(qbs[j], st.astype(BF16))
                o_ref[0, pl.ds(starts[j], C), :] = o_inter + jnp.concatenate(o_parts, axis=0)
                st = st * jnp.exp(b_lasts[j]) + jnp.where(bd_mask, upds[j], 0.0)
            st_sc[...] = st
            return jnp.int32(0)

        g_min = cums[0][C:]
        for j in range(1, HG_GROUP):
            g_min = jnp.minimum(g_min, cums[j][C:])
        lax.cond(jnp.min(g_min) < -EXP_CLAMP,
                 functools.partial(finish, True), functools.partial(finish, False))
        return carry

    lax.fori_loop(0, R // (C * HG_GROUP), group, 0)


def _hgrn(rec):
    B, S, _ = rec.shape
    R = HG_ROWS
    return pl.pallas_call(
        _hgrn_kernel,
        grid=(B, S // R),
        in_specs=[pl.BlockSpec((1, R, 4 * HG_WIDTH), lambda b, s: (b, s, 0))],
        out_specs=pl.BlockSpec((1, R, HG_WIDTH), lambda b, s: (b, s, 0)),
        out_shape=jax.ShapeDtypeStruct((B, S, HG_WIDTH), F32),
        scratch_shapes=[pltpu.VMEM((HG_WIDTH, HG_WIDTH), F32)],
        compiler_params=pltpu.CompilerParams(
            dimension_semantics=("parallel", "arbitrary"), vmem_limit_bytes=VMEM_LIMIT),
        name="hgrn",
    )(rec)


def _mem_kv_kernel(mem_ref, g_ref, w_ref, gk_ref, k_out, v_out):
    m = mem_ref[0]
    mn = m * _rms(m) * g_ref[0]
    kv = _dot(mn.astype(BF16), w_ref[0])
    hw = X_HEADS * X_HEAD_DIM
    gk = gk_ref[0]
    for hd in range(X_HEADS):
        sl = slice(hd * X_HEAD_DIM, (hd + 1) * X_HEAD_DIM)
        kh = kv[:, sl]
        k_out[0, 0, :, sl] = (kh * _rms(kh) * gk).astype(BF16)
    v_out[0, 0] = kv[:, hw:].astype(BF16)


def _mem_kv(mem, mem_norm_g, w_xkv, xk_norm_g):
    B, M, _ = mem.shape
    L = w_xkv.shape[0]
    hw = X_HEADS * X_HEAD_DIM
    out_spec = pl.BlockSpec((1, 1, M, hw), lambda l, b: (l, b, 0, 0))
    return pl.pallas_call(
        _mem_kv_kernel,
        grid=(L, B),
        in_specs=[
            pl.BlockSpec((1, M, D_MODEL), lambda l, b: (b, 0, 0)),
            pl.BlockSpec((1, 1, D_MODEL), lambda l, b: (l, 0, 0)),
            pl.BlockSpec((1, D_MODEL, 2 * hw), lambda l, b: (l, 0, 0)),
            pl.BlockSpec((1, 1, X_HEAD_DIM), lambda l, b: (l, 0, 0)),
        ],
        out_specs=[out_spec, out_spec],
        out_shape=[jax.ShapeDtypeStruct((L, B, M, hw), BF16)] * 2,
        compiler_params=pltpu.CompilerParams(
            dimension_semantics=("parallel", "parallel"), vmem_limit_bytes=VMEM_LIMIT),
        name="mem_kv",
    )(mem, mem_norm_g, w_xkv, xk_norm_g)


def _out_xattn_kernel(x_ref, ya_ref, rec_ref, post_ref, onorm_ref, convw_ref, wo_ref, g_ref,
                      wq_ref, gq_ref, k_ref, v_ref, wxo_ref, o_ref, ubuf_sc):
    half = wo_ref.shape[0] // 2
    tm = x_ref.shape[1]
    W = HG_WIDTH
    subs = [slice(j * XATTN_SUB, (j + 1) * XATTN_SUB) for j in range(tm // XATTN_SUB)]

    rw = lax.broadcasted_iota(jnp.int32, (W, W), 0)
    cw = lax.broadcasted_iota(jnp.int32, (W, W), 1)
    dk_shift = HG_DK.bit_length() - 1
    ones_bd = jnp.where((rw >> dk_shift) == (cw >> dk_shift), 1.0, 0.0).astype(BF16)
    @pl.when(pl.program_id(1) == 0)
    def _():
        ubuf_sc[0:8, :] = jnp.zeros((8, CONV_WIDTH), F32)

    ubuf_sc[8:8 + tm, :] = post_ref[0, :, W + CONV_WIDTH:W + 2 * CONV_WIDTH]
    wc = convw_ref[...]
    ybs = []
    for r in subs:
        o = rec_ref[0, r, :]
        ms = _dot((o * o).astype(BF16), ones_bd) * (1.0 / HG_DK)
        y_hg = o * lax.rsqrt(ms + EPS) * onorm_ref[...] * post_ref[0, r, 0:W]
        u0, u1, u2 = (ubuf_sc[8 - d + r.start:8 - d + r.stop, :] for d in range(CONV_K))
        y_cv = post_ref[0, r, W:W + CONV_WIDTH] * (
            u2 * wc[0:1, :] + u1 * wc[1:2, :] + u0 * wc[2:3, :])
        ybs.append(jnp.concatenate([y_hg, y_cv], axis=1).astype(BF16))
    ubuf_sc[0:8, :] = ubuf_sc[tm:tm + 8, :]

    slots = [slice(hd * X_HEAD_DIM, (hd + 1) * X_HEAD_DIM) for hd in range(X_HEADS)]
    gq = gq_ref[...] * (1.0 / math.sqrt(X_HEAD_DIM))
    x1s = [x_ref[0, r, :] + _dot(ya_ref[0, r, :], wo_ref[:half, :])
           + _dot(yb, wo_ref[half:, :]) for r, yb in zip(subs, ybs)]
    hs = [(x1 * _rms(x1) * g_ref[...]).astype(BF16) for x1 in x1s]
    qs = [_dot(h, wq_ref[...]) for h in hs]
    qhs = [[(q[:, sl] * _rms(q[:, sl]) * gq).astype(BF16) for sl in slots] for q in qs]
    ss = [[_dot_nt(qh, k_ref[0, 0, :, sl]) for qh, sl in zip(row, slots)] for row in qhs]
    ps, ls = [], []
    for row in ss:
        p_row, l_row = [], []
        for s in row:
            p = jnp.exp(s - jnp.max(s, axis=1, keepdims=True))
            l_row.append(jnp.sum(p, axis=1, keepdims=True))
            p_row.append(p.astype(BF16))
        ps.append(p_row)
        ls.append(l_row)
    os_ = [jnp.concatenate([_dot(p, v_ref[0, 0, :, sl]) / l
                            for p, l, sl in zip(p_row, l_row, slots)], axis=1).astype(BF16)
           for p_row, l_row in zip(ps, ls)]
    for r, x1, o in zip(subs, x1s, os_):
        o_ref[0, r, :] = x1 + _dot(o, wxo_ref[...])


def _out_xattn(x, y_mla, o_rec, post, onorm_g, conv_w, w_out, g, w_xq, gq, k_mem, v_mem, w_xo,
               layer):
    B, S, _ = x.shape
    tm = ROW_TILE
    hw = X_HEADS * X_HEAD_DIM
    const = lambda b, s: (0, 0)
    row = lambda w: pl.BlockSpec((1, tm, w), lambda b, s: (b, s, 0))
    mem_spec = pl.BlockSpec((1, 1, MEM_LEN, hw), lambda b, s: (layer, b, 0, 0))
    return pl.pallas_call(
        _out_xattn_kernel,
        grid=(B, S // tm),
        in_specs=[
            row(D_MODEL), row(y_mla.shape[-1]), row(o_rec.shape[-1]), row(post.shape[-1]),
            pl.BlockSpec((1, HG_WIDTH), const),
            pl.BlockSpec((CONV_K, CONV_WIDTH), const),
            pl.BlockSpec((D_MODEL, D_MODEL), const),
            pl.BlockSpec((1, D_MODEL), const),
            pl.BlockSpec((D_MODEL, hw), const),
            pl.BlockSpec((1, X_HEAD_DIM), const),
            mem_spec, mem_spec,
            pl.BlockSpec((hw, D_MODEL), const),
        ],
        out_specs=row(D_MODEL),
        out_shape=jax.ShapeDtypeStruct((B, S, D_MODEL), F32),
        scratch_shapes=[pltpu.VMEM((tm + 8, CONV_WIDTH), F32)],
        compiler_params=pltpu.CompilerParams(
            dimension_semantics=("parallel", "arbitrary"), vmem_limit_bytes=VMEM_LIMIT),
        name="out_xattn",
    )(x, y_mla, o_rec, post, onorm_g, conv_w, w_out, g, w_xq, gq, k_mem, v_mem, w_xo)


def _mlp_kernel(x_ref, g_ref, wu_ref, wd_ref, o_ref):
    x = x_ref[0]
    h = (x * _rms(x) * g_ref[...]).astype(BF16)
    acc = x
    step = D_MODEL
    for c in range(D_FF // step):
        u = _dot(h, wu_ref[:, c * step:(c + 1) * step])
        a = jnp.square(jnp.maximum(u, 0.0)).astype(BF16)
        acc = acc + _dot(a, wd_ref[c * step:(c + 1) * step, :])
    o_ref[0] = acc


def _mlp(x, g, w_up, w_down):
    B, S, _ = x.shape
    tm = MLP_TILE
    const = lambda b, s: (0, 0)
    row = pl.BlockSpec((1, tm, D_MODEL), lambda b, s: (b, s, 0))
    once = pl.Buffered(1)
    return pl.pallas_call(
        _mlp_kernel,
        grid=(B, S // tm),
        in_specs=[row, pl.BlockSpec((1, D_MODEL), const),
                  pl.BlockSpec((D_MODEL, D_FF), const, pipeline_mode=once),
                  pl.BlockSpec((D_FF, D_MODEL), const, pipeline_mode=once)],
        out_specs=row,
        out_shape=jax.ShapeDtypeStruct((B, S, D_MODEL), F32),
        compiler_params=pltpu.CompilerParams(
            dimension_semantics=("parallel", "parallel"), vmem_limit_bytes=VMEM_LIMIT),
        name="mlp",
    )(x, g, w_up, w_down)


def _scatter_lanes(w_cols):
    idx = jnp.asarray(np.where(_LANE_MAP >= 0, _LANE_MAP, 0), jnp.int32)
    valid = jnp.asarray(_LANE_MAP >= 0)
    return jnp.where(valid, jnp.take(w_cols, idx, axis=-1), 0.0)


def _prep_layer(l, positions, w_in, w_uq, w_ukv, mla_qn_g, mla_kn_g):
    o_kr = MLA_Q_RANK + MLA_KV_RANK
    w = w_in[l]
    kr_src = jnp.concatenate(
        [jnp.zeros((D_MODEL, MLA_NOPE), F32), w[:, o_kr:o_kr + MLA_ROPE]], axis=1)
    w_in_p = jnp.concatenate(
        [w[:, :o_kr], _scatter_lanes(kr_src), w[:, o_kr + MLA_ROPE:]], axis=1).astype(BF16)

    def with_swapped_halves(w_slots):
        rank = w_slots.shape[0]
        both = jnp.concatenate([w_slots, jnp.roll(w_slots, HEAD_SLOT // 2, axis=-1)], axis=1)
        return both.reshape(rank, 2 * MLA_HEADS * HEAD_SLOT).astype(BF16)

    wq = w_uq[l].reshape(MLA_Q_RANK, MLA_HEADS, MLA_QK)
    w_uq_p = with_swapped_halves(_scatter_lanes(wq))

    wkv = w_ukv[l].reshape(MLA_KV_RANK, MLA_HEADS, MLA_NOPE + MLA_V)
    k_src = jnp.concatenate(
        [wkv[..., :MLA_NOPE], jnp.zeros((MLA_KV_RANK, MLA_HEADS, MLA_ROPE), F32)], axis=-1)
    w_uk_p = with_swapped_halves(_scatter_lanes(k_src))
    w_vt = wkv[..., MLA_NOPE:].reshape(MLA_KV_RANK, MLA_HEADS * MLA_V).T.astype(BF16)

    inv_freq = ROPE_BASE ** (-jnp.arange(0, MLA_ROPE, 2, dtype=F32) / MLA_ROPE)
    ang = positions.astype(F32)[:, None] * inv_freq[None, :]
    cos, sin = jnp.cos(ang), jnp.sin(ang)
    S = positions.shape[0]
    pad = HEAD_SLOT // 2 - ROPE_HALF
    c_tab = jnp.concatenate([cos, jnp.ones((S, pad), F32), cos, jnp.ones((S, pad), F32)], axis=1)
    s_tab = jnp.concatenate([-sin, jnp.zeros((S, pad), F32), sin, jnp.zeros((S, pad), F32)], axis=1)

    def tables(gain, scale):
        g = _scatter_lanes(gain)[None, :] * scale
        return c_tab * g, s_tab * jnp.roll(g, HEAD_SLOT // 2, axis=1)

    cq_t, sq_t = tables(mla_qn_g[l], math.log2(math.e) / math.sqrt(MLA_QK))
    ck_t, sk_t = tables(mla_kn_g[l], 1.0)
    return w_in_p, w_uq_p, w_uk_p, w_vt, (cq_t, sq_t, ck_t, sk_t)


def kernel(x, mem, positions, mix_norm_g, w_in, mla_q_norm_g, mla_kv_norm_g, w_uq, w_ukv,
           mla_qn_g, mla_kn_g, hgrn_lb_logits, hgrn_o_norm_g, conv_w, w_out,
           xattn_norm_g, mem_norm_g, w_xq, w_xkv, xq_norm_g, xk_norm_g, w_xo,
           mlp_norm_g, w_up, w_down):
    B, S, _ = x.shape
    L = w_in.shape[0]
    nq, nk = S // ATT_TQ, S // ATT_TK
    pq = positions.reshape(nq, ATT_TQ)
    pk = positions.reshape(nk, ATT_TK)
    vis = jnp.min(pk, axis=1)[None, :] <= jnp.max(pq, axis=1)[:, None]
    n_kv = jnp.max(jnp.where(vis, jnp.arange(1, nk + 1, dtype=jnp.int32)[None, :], 0),
                   axis=1).astype(jnp.int32)
    full = jnp.max(pk, axis=1)[None, :] <= jnp.min(pq, axis=1)[:, None]
    n_full = jnp.sum(jnp.cumprod(full.astype(jnp.int32), axis=1), axis=1).astype(jnp.int32)
    posq = positions.reshape(nq, 1, ATT_TQ)
    posk = positions.reshape(S, 1)

    k_mem, v_mem = _mem_kv(mem, mem_norm_g.reshape(L, 1, D_MODEL), w_xkv.astype(BF16),
                           xk_norm_g.reshape(L, 1, X_HEAD_DIM))
    for l in range(L):
        w_in_p, w_uq_p, w_uk_p, w_vt, tabs = _prep_layer(l, positions, w_in, w_uq, w_ukv,
                                                         mla_qn_g, mla_kn_g)
        lat, rec, post = _in_proj(x, mix_norm_g[l][None], w_in_p, hgrn_lb_logits, l)
        q, k, vt = _mla_qkv(lat, mla_q_norm_g[l][None], w_uq_p, mla_kv_norm_g[l][None],
                            w_uk_p, w_vt, tabs)
        score_bound = (math.log2(math.e) * math.sqrt(MLA_QK)
                       * jnp.max(jnp.abs(mla_qn_g[l])) * jnp.max(jnp.abs(mla_kn_g[l])))
        y_mla = lax.cond(score_bound <= ATT_PLAIN_MAX_SCORE,
                         functools.partial(_attention, online=False),
                         functools.partial(_attention, online=True),
                         q, k, vt, n_full, n_kv, posq, posk)
        o_rec = _hgrn(rec)
        x = _out_xattn(x, y_mla, o_rec, post, hgrn_o_norm_g[l][None], conv_w[l],
                       w_out[l].astype(BF16), xattn_norm_g[l][None], w_xq[l].astype(BF16),
                       xq_norm_g[l][None], k_mem, v_mem, w_xo[l].astype(BF16), l)
        x = _mlp(x, mlp_norm_g[l][None], w_up[l].astype(BF16), w_down[l].astype(BF16))
    return x
```

```python
import functools
import math

import numpy as np
import jax
import jax.numpy as jnp
from jax import lax
from jax.experimental import pallas as pl
from jax.experimental.pallas import tpu as pltpu

F32 = jnp.float32
BF16 = jnp.bfloat16

D_MODEL = 1024
DEPTH = 4
MEM_LEN = 256
EPS = 1e-6

MLA_HEADS = 8
MLA_NOPE = 64
MLA_ROPE = 32
MLA_V = 64
MLA_QK = MLA_NOPE + MLA_ROPE
MLA_Q_RANK = 384
MLA_KV_RANK = 256
ROPE_BASE = 10000.0

HG_HEADS = 4
HG_DK = 64
HG_CHUNK = 64
HG_SUB = 16
HG_GROUP = 8
HG_WIDTH = 256
CONV_WIDTH = 256
CONV_K = 3

X_HEADS = 4
X_HEAD_DIM = 128
D_FF = 4 * D_MODEL

LANES = 128
HEAD_SLOT = LANES
ROPE_HALF = MLA_ROPE // 2
N_IN_PAD = MLA_Q_RANK + MLA_KV_RANK + HEAD_SLOT + 4 * HG_WIDTH + 3 * CONV_WIDTH
HC_OFF = MLA_Q_RANK + MLA_KV_RANK + HEAD_SLOT

ROW_TILE = 1024
MLP_TILE = 1024
IN_PROJ_SUB = 256
XATTN_SUB = 256
ATT_TQ = 256
ATT_TK = 256
ATT_HEADS = 8
ATT_CHAINS = 4
ATT_UNROLLS = (4, 2)
HG_ROWS = 1024
MASK_VALUE = -1e30
ATT_PLAIN_MAX_SCORE = 64.0
EXP_CLAMP = 60.0
VMEM_LIMIT = 56 * 1024 * 1024


def _head_lane_map():
    m = -np.ones((HEAD_SLOT,), np.int64)
    m[0:ROPE_HALF] = MLA_NOPE + np.arange(ROPE_HALF)
    m[ROPE_HALF:64] = np.arange(64 - ROPE_HALF)
    m[64:64 + ROPE_HALF] = MLA_NOPE + ROPE_HALF + np.arange(ROPE_HALF)
    m[64 + ROPE_HALF:96] = (64 - ROPE_HALF) + np.arange(ROPE_HALF)
    return m


_LANE_MAP = _head_lane_map()


def _rms(x, n=None):
    n = x.shape[-1] if n is None else n
    return lax.rsqrt(jnp.sum(x * x, axis=-1, keepdims=True) * (1.0 / n) + EPS)


def _sigmoid(x):
    return 1.0 / (1.0 + jnp.exp(-x))


def _dot(a, b):
    return jnp.dot(a, b, preferred_element_type=F32)


def _dot_nt(a, b):
    return lax.dot_general(a, b, (((1,), (1,)), ((), ())), preferred_element_type=F32)


def _dot_tn(a, b):
    return lax.dot_general(a, b, (((0,), (0,)), ((), ())), preferred_element_type=F32)


def _in_proj_kernel(x_ref, g_ref, w_in_ref, lbl_ref, lat_out, rec_out, post_out, *, layer):
    W = HG_WIDTH
    subs = [slice(j * IN_PROJ_SUB, (j + 1) * IN_PROJ_SUB)
            for j in range(x_ref.shape[1] // IN_PROJ_SUB)]
    hs = []
    for r in subs:
        x = x_ref[0, r, :]
        hs.append((x * _rms(x) * g_ref[...]).astype(BF16))
    hcs = []
    for r, h in zip(subs, hs):
        lat_out[0, r, :] = _dot(h, w_in_ref[:, :HC_OFF])
        hcs.append(_dot(h, w_in_ref[:, HC_OFF:]))

    lg = lbl_ref[...]
    e = jnp.exp(lg - jnp.max(lg, axis=0, keepdims=True))
    soft = e / jnp.sum(e, axis=0, keepdims=True)
    lb = jnp.zeros((1, W), F32)
    for i in range(1, layer + 1):
        lb = lb + soft[i:i + 1, :]
    lb = jnp.maximum(lb, 0.0)
    log_lb = jnp.log(lb)
    log_1m = jnp.log1p(-lb)

    for rows, hc in zip(subs, hcs):
        zq, zf, vi, zg = (hc[:, i * W:(i + 1) * W] for i in range(4))
        cb, cc, cx = (hc[:, 4 * W + i * CONV_WIDTH:4 * W + (i + 1) * CONV_WIDTH]
                      for i in range(3))
        t = jnp.exp(-jnp.abs(zf))
        d = 1.0 + t
        r = 1.0 / d
        log_sig = jnp.minimum(zf, 0.0) - jnp.log(d)
        c2 = log_1m + log_sig
        log_f = jnp.maximum(log_lb, c2) + jnp.log(1.0 + jnp.exp(-jnp.abs(log_lb - c2)))
        rec_out[0, rows, 0:W] = zq * _sigmoid(zq)
        rec_out[0, rows, W:2 * W] = log_f
        rec_out[0, rows, 2 * W:3 * W] = (1.0 - lb) * jnp.where(zf >= 0.0, t * r, r)
        rec_out[0, rows, 3 * W:4 * W] = vi
        post_out[0, rows, 0:W] = zg * _sigmoid(zg)
        post_out[0, rows, W:W + CONV_WIDTH] = cb
        post_out[0, rows, W + CONV_WIDTH:W + 2 * CONV_WIDTH] = cc * cx


def _in_proj(x, g, w_in, lb_logits, layer):
    B, S, _ = x.shape
    tm = ROW_TILE
    const = lambda b, s: (0, 0)
    widths = (HC_OFF, 4 * HG_WIDTH, HG_WIDTH + 2 * CONV_WIDTH)
    return pl.pallas_call(
        functools.partial(_in_proj_kernel, layer=layer),
        grid=(B, S // tm),
        in_specs=[
            pl.BlockSpec((1, tm, D_MODEL), lambda b, s: (b, s, 0)),
            pl.BlockSpec((1, D_MODEL), const),
            pl.BlockSpec((D_MODEL, N_IN_PAD), const),
            pl.BlockSpec((DEPTH, HG_WIDTH), const),
        ],
        out_specs=[pl.BlockSpec((1, tm, w), lambda b, s: (b, s, 0)) for w in widths],
        out_shape=[jax.ShapeDtypeStruct((B, S, w), F32) for w in widths],
        compiler_params=pltpu.CompilerParams(
            dimension_semantics=("parallel", "parallel"), vmem_limit_bytes=VMEM_LIMIT),
        name="in_proj",
    )(x, g, w_in, lb_logits)


def _mla_qkv_kernel(lat_ref, gq_ref, w_uq_ref, gkv_ref, w_uk_ref, w_vt_ref,
                    cq_tab, sq_tab, ck_tab, sk_tab, q_out, k_out, vt_out):
    hw = MLA_HEADS * HEAD_SLOT
    subs = [slice(j * ATT_TK, (j + 1) * ATT_TK) for j in range(vt_out.shape[1])]
    lats = []
    for r in subs:
        cq = lat_ref[0, r, :MLA_Q_RANK]
        ckv = lat_ref[0, r, MLA_Q_RANK:MLA_Q_RANK + MLA_KV_RANK]
        lats.append(((cq * _rms(cq) * gq_ref[...]).astype(BF16),
                     (ckv * _rms(ckv) * gkv_ref[...]).astype(BF16)))
    qfs = [_dot(cqn, w_uq_ref[...]) for cqn, _ in lats]
    kfs = [_dot(ckvn, w_uk_ref[...]) for _, ckvn in lats]
    for j, (_, ckvn) in enumerate(lats):
        vt_out[0, j] = _dot_nt(w_vt_ref[...], ckvn).astype(BF16)
    for r, qf, kf in zip(subs, qfs, kfs):
        kr = lat_ref[0, r, MLA_Q_RANK + MLA_KV_RANK:]
        kr_sw = pltpu.roll(kr, HEAD_SLOT // 2, 1)
        cq_t, sq_t, ck_t, sk_t = cq_tab[r, :], sq_tab[r, :], ck_tab[r, :], sk_tab[r, :]
        for hd in range(MLA_HEADS):
            sl = slice(hd * HEAD_SLOT, (hd + 1) * HEAD_SLOT)
            sw = slice(hw + hd * HEAD_SLOT, hw + (hd + 1) * HEAD_SLOT)
            qh = qf[:, sl]
            qh = (qh * cq_t + qf[:, sw] * sq_t) * _rms(qh, MLA_QK)
            q_out[0, hd, :, r] = qh.T.astype(BF16)
            kh = kf[:, sl] + kr
            kh = (kh * ck_t + (kf[:, sw] + kr_sw) * sk_t) * _rms(kh, MLA_QK)
            k_out[0, hd, r, :] = kh.astype(BF16)


def _mla_qkv(lat, gq, w_uq, gkv, w_uk, w_vt, tabs):
    B, S, _ = lat.shape
    tm = ROW_TILE
    vw = MLA_HEADS * MLA_V
    const = lambda b, s: (0, 0)
    tab_spec = pl.BlockSpec((tm, HEAD_SLOT), lambda b, s: (s, 0))
    return pl.pallas_call(
        _mla_qkv_kernel,
        grid=(B, S // tm),
        in_specs=[
            pl.BlockSpec((1, tm, HC_OFF), lambda b, s: (b, s, 0)),
            pl.BlockSpec((1, MLA_Q_RANK), const),
            pl.BlockSpec((MLA_Q_RANK, 2 * MLA_HEADS * HEAD_SLOT), const),
            pl.BlockSpec((1, MLA_KV_RANK), const),
            pl.BlockSpec((MLA_KV_RANK, 2 * MLA_HEADS * HEAD_SLOT), const),
            pl.BlockSpec((vw, MLA_KV_RANK), const),
            tab_spec, tab_spec, tab_spec, tab_spec,
        ],
        out_specs=[
            pl.BlockSpec((1, MLA_HEADS, HEAD_SLOT, tm), lambda b, s: (b, 0, 0, s)),
            pl.BlockSpec((1, MLA_HEADS, tm, HEAD_SLOT), lambda b, s: (b, 0, s, 0)),
            pl.BlockSpec((1, tm // ATT_TK, vw, ATT_TK), lambda b, s: (b, s, 0, 0)),
        ],
        out_shape=[
            jax.ShapeDtypeStruct((B, MLA_HEADS, HEAD_SLOT, S), BF16),
            jax.ShapeDtypeStruct((B, MLA_HEADS, S, HEAD_SLOT), BF16),
            jax.ShapeDtypeStruct((B, S // ATT_TK, vw, ATT_TK), BF16),
        ],
        compiler_params=pltpu.CompilerParams(
            dimension_semantics=("parallel", "parallel"), vmem_limit_bytes=VMEM_LIMIT),
        name="mla_qkv",
    )(lat, gq, w_uq, gkv, w_uk, w_vt, *tabs)


def _attn_kernel(nfull_ref, nkv_ref, q_ref, k_ref, vt_ref, posq_ref, posk_ref, o_ref,
                 p_sc, acc_sc, *maybe_s_sc, online):
    qi = pl.program_id(2)
    n_full = nfull_ref[qi]
    n_kv = nkv_ref[qi]
    tq = q_ref.shape[3]
    nk = vt_ref.shape[1]
    heads = range(ATT_HEADS)
    qts = [q_ref[0, hd] for hd in heads]
    posq = posq_ref[0]

    groups = [list(heads)[i:i + ATT_CHAINS] for i in range(0, ATT_HEADS, ATT_CHAINS)]

    def score(ki, hd):
        off = pl.multiple_of(ki * ATT_TK, ATT_TK)
        return _dot(k_ref[0, hd, pl.ds(off, ATT_TK), :], qts[hd])

    def pv(ki, hd, p=None):
        p = p_sc[hd] if p is None else p
        return _dot(vt_ref[0, ki, hd * MLA_V:(hd + 1) * MLA_V, :], p)

    def issue(ki_scores, ki_pv):
        ss, pvs = {}, {}
        for g in groups:
            for hd in g:
                ss[hd] = score(ki_scores, hd)
            for hd in g:
                pvs[hd] = pv(ki_pv, hd)
        return ss, pvs

    def mask_of(ki):
        off = pl.multiple_of(ki * ATT_TK, ATT_TK)
        return posk_ref[pl.ds(off, ATT_TK), :] <= posq

    def step_plain(ki, carry, masked):
        ss, pvs = issue(ki, jnp.maximum(ki - 1, 0))
        if masked:
            mask = mask_of(ki)
        new = []
        for hd in heads:
            s = jnp.where(mask, ss[hd], MASK_VALUE) if masked else ss[hd]
            p = jnp.exp2(s)
            new.append(carry[hd] + jnp.sum(p, axis=0, keepdims=True))
            acc_sc[hd] = acc_sc[hd] + pvs[hd]
            p_sc[hd] = p.astype(BF16)
        return tuple(new)

    def multi_plain(j, carry, n_tiles, first):
        k0 = first + n_tiles * j
        ss, pvs = issue(k0, jnp.maximum(k0 - 1, 0))
        l_add = {hd: 0.0 for hd in heads}
        for hd in heads:
            acc_sc[hd] = acc_sc[hd] + pvs[hd]
        for u in range(1, n_tiles):
            ss_next = {}
            for g in groups:
                for hd in g:
                    ss_next[hd] = score(k0 + u, hd)
                for hd in g:
                    p = jnp.exp2(ss[hd])
                    l_add[hd] = l_add[hd] + jnp.sum(p, axis=0, keepdims=True)
                    acc_sc[hd] = acc_sc[hd] + pv(k0 + u - 1, hd, p.astype(BF16))
            ss = ss_next
        new = []
        for hd in heads:
            p = jnp.exp2(ss[hd])
            new.append(carry[hd] + l_add[hd] + jnp.sum(p, axis=0, keepdims=True))
            p_sc[hd] = p.astype(BF16)
        return tuple(new)

    def step_online(ki, carry, masked):
        (s_sc,) = maybe_s_sc
        s_next, pvs = issue(jnp.minimum(ki + 1, nk - 1), jnp.maximum(ki - 1, 0))
        if masked:
            mask = mask_of(ki)
        new = []
        for hd in heads:
            m_old, l_old = carry[hd]
            s = s_sc[hd]
            if masked:
                s = jnp.where(mask, s, MASK_VALUE)
            m_new = jnp.maximum(m_old, jnp.max(s, axis=0, keepdims=True))
            alpha = jnp.exp2(m_old - m_new)
            p = jnp.exp2(s - m_new)
            l_new = alpha * l_old + jnp.sum(p, axis=0, keepdims=True)
            acc_sc[hd] = alpha * (acc_sc[hd] + pvs[hd])
            p_sc[hd] = p.astype(BF16)
            new.append((m_new, l_new))
        for hd in heads:
            s_sc[hd] = s_next[hd]
        return tuple(new)

    p_sc[...] = jnp.zeros(p_sc.shape, BF16)
    acc_sc[...] = jnp.zeros(acc_sc.shape, F32)
    if online:
        for hd in heads:
            maybe_s_sc[0][hd] = score(0, hd)
        init = tuple((jnp.full((1, tq), MASK_VALUE, F32), jnp.zeros((1, tq), F32))
                     for _ in heads)
        step = step_online
    else:
        init = tuple(jnp.zeros((1, tq), F32) for _ in heads)
        step = step_plain
    n_done = 0
    if not online:
        for n_tiles in ATT_UNROLLS:
            n_steps = (n_full - n_done) // n_tiles
            init = lax.fori_loop(
                0, n_steps, functools.partial(multi_plain, n_tiles=n_tiles, first=n_done), init)
            n_done = n_done + n_tiles * n_steps
    carry = lax.fori_loop(n_done, n_full, functools.partial(step, masked=False), init)
    carry = lax.fori_loop(n_full, n_kv, functools.partial(step, masked=True), carry)
    last = jnp.maximum(n_kv - 1, 0)
    ls = [c[1] for c in carry] if online else carry
    o_t = jnp.concatenate([(acc_sc[hd] + pv(last, hd)) / ls[hd] for hd in heads], axis=0)
    o_ref[0] = o_t.T.astype(BF16)


def _attention(q, k, vt, n_full, n_kv, posq, posk, online):
    B, H, _, S = q.shape
    tq = ATT_TQ
    nh = ATT_HEADS
    nk = S // ATT_TK
    scratch = [pltpu.VMEM((nh, ATT_TK, tq), BF16), pltpu.VMEM((nh, MLA_V, tq), F32)]
    if online:
        scratch.append(pltpu.VMEM((nh, ATT_TK, tq), F32))
    grid_spec = pltpu.PrefetchScalarGridSpec(
        num_scalar_prefetch=2,
        grid=(B, H // nh, S // tq),
        in_specs=[
            pl.BlockSpec((1, nh, HEAD_SLOT, tq), lambda b, hp, qi, *_: (b, hp, 0, qi)),
            pl.BlockSpec((1, nh, S, HEAD_SLOT), lambda b, hp, qi, *_: (b, hp, 0, 0)),
            pl.BlockSpec((1, nk, nh * MLA_V, ATT_TK), lambda b, hp, qi, *_: (b, 0, hp, 0)),
            pl.BlockSpec((1, 1, tq), lambda b, hp, qi, *_: (qi, 0, 0)),
            pl.BlockSpec((S, 1), lambda b, hp, qi, *_: (0, 0)),
        ],
        out_specs=pl.BlockSpec((1, tq, nh * MLA_V), lambda b, hp, qi, *_: (b, qi, hp)),
        scratch_shapes=scratch,
    )
    return pl.pallas_call(
        functools.partial(_attn_kernel, online=online),
        grid_spec=grid_spec,
        out_shape=jax.ShapeDtypeStruct((B, S, H * MLA_V), BF16),
        compiler_params=pltpu.CompilerParams(
            dimension_semantics=("parallel", "parallel", "arbitrary"),
            vmem_limit_bytes=VMEM_LIMIT),
        name="mla_attention_online" if online else "mla_attention",
    )(n_full, n_kv, q, k, vt, posq, posk)


def _hgrn_kernel(rec_ref, o_ref, st_sc):
    sb = pl.program_id(1)
    R = HG_ROWS
    C = HG_CHUNK
    W = HG_WIDTH

    @pl.when(sb == 0)
    def _():
        st_sc[...] = jnp.zeros(st_sc.shape, F32)

    ri = lax.broadcasted_iota(jnp.int32, (C, C), 0)
    ci = lax.broadcasted_iota(jnp.int32, (C, C), 1)
    sub_shift = HG_SUB.bit_length() - 1
    dk_shift = HG_DK.bit_length() - 1
    tri = jnp.where(ci <= ri, 1.0, 0.0)
    tri_blk = jnp.where((ci >> sub_shift) == (ri >> sub_shift), tri, 0.0)
    cum_mat = jnp.concatenate([tri, tri_blk], axis=0).astype(BF16)
    tt = ri & (HG_SUB - 1)
    lane_w = lax.broadcasted_iota(jnp.int32, (1, W), 1)
    head_masks = [jnp.where((lane_w >> dk_shift) == hh, 1.0, 0.0) for hh in range(HG_HEADS)]
    rw = lax.broadcasted_iota(jnp.int32, (W, W), 0)
    cw = lax.broadcasted_iota(jnp.int32, (W, W), 1)
    bd_mask = (rw >> dk_shift) == (cw >> dk_shift)
    ones_bd = jnp.where(bd_mask, 1.0, 0.0).astype(BF16)
    n_sub = C // HG_SUB

    def group(gi, carry):
        base = gi * (HG_GROUP * C)
        starts = [pl.multiple_of(base + j * C, C) for j in range(HG_GROUP)]
        chunks = range(HG_GROUP)
        gates, splits = [], []
        for r0 in starts:
            q = rec_ref[0, pl.ds(r0, C), 0:W]
            log_f = rec_ref[0, pl.ds(r0, C), W:2 * W]
            kk = rec_ref[0, pl.ds(r0, C), 2 * W:3 * W]
            vi = rec_ref[0, pl.ds(r0, C), 3 * W:4 * W]
            f_hi = log_f.astype(BF16)
            r1 = log_f - f_hi.astype(F32)
            f_mid = r1.astype(BF16)
            f_lo = (r1 - f_mid.astype(F32)).astype(BF16)
            gates.append((q, kk, vi.astype(BF16)))
            splits.append((f_hi, f_mid, f_lo))
        cums = [_dot(cum_mat, s[0]) + _dot(cum_mat, s[1]) + _dot(cum_mat, s[2]) for s in splits]

        def diag_direct(j, i):
            q, kk, vb = gates[j]
            rows = slice(i * HG_SUB, (i + 1) * HG_SUB)
            b_t, q_t, k_t = cums[j][:C][rows], q[rows], kk[rows]
            v_t = vb[rows].astype(F32)
            t_idx = lax.broadcasted_iota(jnp.int32, (HG_SUB, W), 0)
            slabs = []
            for s in range(HG_SUB):
                w = jnp.exp(jnp.minimum(b_t - b_t[s:s + 1, :], 0.0))
                slabs.append(jnp.where(t_idx >= s, q_t * k_t[s:s + 1, :] * w, 0.0))
            x = jnp.concatenate(slabs, axis=0).astype(BF16)
            a = _dot(x, ones_bd)
            o = a[0:HG_SUB] * v_t[0:1, :]
            for s in range(1, HG_SUB):
                o = o + a[s * HG_SUB:(s + 1) * HG_SUB] * v_t[s:s + 1, :]
            return o

        def finish(direct_diag):
            qbs, b_lasts, lhss, kds, kdls = [], [], [], [], []
            for j in chunks:
                q, kk, _ = gates[j]
                b = cums[j][:C]
                g = cums[j][C:]
                b_last = b[C - 1:C, :]
                rref = b - g
                qd = q * jnp.exp(g)
                qbs.append((q * jnp.exp(b)).astype(BF16))
                b_lasts.append(b_last)
                kdls.append((kk * jnp.exp(b_last - b)).astype(BF16))
                lhs_j, kd_j = [], []
                for i in range(n_sub):
                    rr = rref[i * HG_SUB:i * HG_SUB + 1, :]
                    kd_j.append((kk * jnp.exp(jnp.minimum(rr - b, EXP_CLAMP))).astype(BF16))
                    qi = qd[i * HG_SUB:(i + 1) * HG_SUB]
                    lhs_j.append(jnp.concatenate([qi * hm for hm in head_masks],
                                                 axis=0).astype(BF16))
                lhss.append(lhs_j)
                kds.append(kd_j)
            attn = [[_dot_nt(lhss[j][i], kds[j][i]) for i in range(n_sub)] for j in chunks]
            upds = [_dot_tn(gates[j][2], kdls[j]) for j in chunks]
            if direct_diag:
                attn = [[jnp.where(ci < i * HG_SUB, attn[j][i], 0.0).astype(BF16)
                         for i in range(n_sub)] for j in chunks]
            else:
                attn = [[jnp.where(ci <= i * HG_SUB + tt, attn[j][i], 0.0).astype(BF16)
                         for i in range(n_sub)] for j in chunks]
            pvs = [[_dot(attn[j][i], gates[j][2]) for i in range(n_sub)] for j in chunks]
            st = st_sc[...]
            for j in chunks:
                o_parts = []
                for i in range(n_sub):
                    pv = pvs[j][i]
                    oi = pv[0:HG_SUB] * head_masks[0]
                    for hh in range(1, HG_HEADS):
                        oi = oi + pv[hh * HG_SUB:(hh + 1) * HG_SUB] * head_masks[hh]
                    if direct_diag:
                        oi = oi + diag_direct(j, i)
                    o_parts.append(oi)
                o_inter = _dot_nt(qbs[j], st.astype(BF16))
                o_ref[0, pl.ds(starts[j], C), :] = o_inter + jnp.concatenate(o_parts, axis=0)
                st = st * jnp.exp(b_lasts[j]) + jnp.where(bd_mask, upds[j], 0.0)
            st_sc[...] = st
            return jnp.int32(0)

        g_min = cums[0][C:]
        for j in range(1, HG_GROUP):
            g_min = jnp.minimum(g_min, cums[j][C:])
        lax.cond(jnp.min(g_min) < -EXP_CLAMP,
                 functools.partial(finish, True), functools.partial(finish, False))
        return carry

    lax.fori_loop(0, R // (C * HG_GROUP), group, 0)


def _hgrn(rec):
    B, S, _ = rec.shape
    R = HG_ROWS
    return pl.pallas_call(
        _hgrn_kernel,
        grid=(B, S // R),
        in_specs=[pl.BlockSpec((1, R, 4 * HG_WIDTH), lambda b, s: (b, s, 0))],
        out_specs=pl.BlockSpec((1, R, HG_WIDTH), lambda b, s: (b, s, 0)),
        out_shape=jax.ShapeDtypeStruct((B, S, HG_WIDTH), F32),
        scratch_shapes=[pltpu.VMEM((HG_WIDTH, HG_WIDTH), F32)],
        compiler_params=pltpu.CompilerParams(
            dimension_semantics=("parallel", "arbitrary"), vmem_limit_bytes=VMEM_LIMIT),
        name="hgrn",
    )(rec)


def _mem_kv_kernel(mem_ref, g_ref, w_ref, gk_ref, k_out, v_out):
    m = mem_ref[0]
    mn = m * _rms(m) * g_ref[0]
    kv = _dot(mn.astype(BF16), w_ref[0])
    hw = X_HEADS * X_HEAD_DIM
    gk = gk_ref[0]
    for hd in range(X_HEADS):
        sl = slice(hd * X_HEAD_DIM, (hd + 1) * X_HEAD_DIM)
        kh = kv[:, sl]
        k_out[0, 0, :, sl] = (kh * _rms(kh) * gk).astype(BF16)
    v_out[0, 0] = kv[:, hw:].astype(BF16)


def _mem_kv(mem, mem_norm_g, w_xkv, xk_norm_g):
    B, M, _ = mem.shape
    L = w_xkv.shape[0]
    hw = X_HEADS * X_HEAD_DIM
    out_spec = pl.BlockSpec((1, 1, M, hw), lambda l, b: (l, b, 0, 0))
    return pl.pallas_call(
        _mem_kv_kernel,
        grid=(L, B),
        in_specs=[
            pl.BlockSpec((1, M, D_MODEL), lambda l, b: (b, 0, 0)),
            pl.BlockSpec((1, 1, D_MODEL), lambda l, b: (l, 0, 0)),
            pl.BlockSpec((1, D_MODEL, 2 * hw), lambda l, b: (l, 0, 0)),
            pl.BlockSpec((1, 1, X_HEAD_DIM), lambda l, b: (l, 0, 0)),
        ],
        out_specs=[out_spec, out_spec],
        out_shape=[jax.ShapeDtypeStruct((L, B, M, hw), BF16)] * 2,
        compiler_params=pltpu.CompilerParams(
            dimension_semantics=("parallel", "parallel"), vmem_limit_bytes=VMEM_LIMIT),
        name="mem_kv",
    )(mem, mem_norm_g, w_xkv, xk_norm_g)


def _out_xattn_kernel(x_ref, ya_ref, rec_ref, post_ref, onorm_ref, convw_ref, wo_ref, g_ref,
                      wq_ref, gq_ref, k_ref, v_ref, wxo_ref, o_ref, ubuf_sc):
    half = wo_ref.shape[0] // 2
    tm = x_ref.shape[1]
    W = HG_WIDTH
    subs = [slice(j * XATTN_SUB, (j + 1) * XATTN_SUB) for j in range(tm // XATTN_SUB)]

    rw = lax.broadcasted_iota(jnp.int32, (W, W), 0)
    cw = lax.broadcasted_iota(jnp.int32, (W, W), 1)
    dk_shift = HG_DK.bit_length() - 1
    ones_bd = jnp.where((rw >> dk_shift) == (cw >> dk_shift), 1.0, 0.0).astype(BF16)
    @pl.when(pl.program_id(1) == 0)
    def _():
        ubuf_sc[0:8, :] = jnp.zeros((8, CONV_WIDTH), F32)

    ubuf_sc[8:8 + tm, :] = post_ref[0, :, W + CONV_WIDTH:W + 2 * CONV_WIDTH]
    wc = convw_ref[...]
    ybs = []
    for r in subs:
        o = rec_ref[0, r, :]
        ms = _dot((o * o).astype(BF16), ones_bd) * (1.0 / HG_DK)
        y_hg = o * lax.rsqrt(ms + EPS) * onorm_ref[...] * post_ref[0, r, 0:W]
        u0, u1, u2 = (ubuf_sc[8 - d + r.start:8 - d + r.stop, :] for d in range(CONV_K))
        y_cv = post_ref[0, r, W:W + CONV_WIDTH] * (
            u2 * wc[0:1, :] + u1 * wc[1:2, :] + u0 * wc[2:3, :])
        ybs.append(jnp.concatenate([y_hg, y_cv], axis=1).astype(BF16))
    ubuf_sc[0:8, :] = ubuf_sc[tm:tm + 8, :]

    slots = [slice(hd * X_HEAD_DIM, (hd + 1) * X_HEAD_DIM) for hd in range(X_HEADS)]
    gq = gq_ref[...] * (1.0 / math.sqrt(X_HEAD_DIM))
    x1s = [x_ref[0, r, :] + _dot(ya_ref[0, r, :], wo_ref[:half, :])
           + _dot(yb, wo_ref[half:, :]) for r, yb in zip(subs, ybs)]
    hs = [(x1 * _rms(x1) * g_ref[...]).astype(BF16) for x1 in x1s]
    qs = [_dot(h, wq_ref[...]) for h in hs]
    qhs = [[(q[:, sl] * _rms(q[:, sl]) * gq).astype(BF16) for sl in slots] for q in qs]
    ss = [[_dot_nt(qh, k_ref[0, 0, :, sl]) for qh, sl in zip(row, slots)] for row in qhs]
    ps, ls = [], []
    for row in ss:
        p_row, l_row = [], []
        for s in row:
            p = jnp.exp(s - jnp.max(s, axis=1, keepdims=True))
            l_row.append(jnp.sum(p, axis=1, keepdims=True))
            p_row.append(p.astype(BF16))
        ps.append(p_row)
        ls.append(l_row)
    os_ = [jnp.concatenate([_dot(p, v_ref[0, 0, :, sl]) / l
                            for p, l, sl in zip(p_row, l_row, slots)], axis=1).astype(BF16)
           for p_row, l_row in zip(ps, ls)]
    for r, x1, o in zip(subs, x1s, os_):
        o_ref[0, r, :] = x1 + _dot(o, wxo_ref[...])


def _out_xattn(x, y_mla, o_rec, post, onorm_g, conv_w, w_out, g, w_xq, gq, k_mem, v_mem, w_xo,
               layer):
    B, S, _ = x.shape
    tm = ROW_TILE
    hw = X_HEADS * X_HEAD_DIM
    const = lambda b, s: (0, 0)
    row = lambda w: pl.BlockSpec((1, tm, w), lambda b, s: (b, s, 0))
    mem_spec = pl.BlockSpec((1, 1, MEM_LEN, hw), lambda b, s: (layer, b, 0, 0))
    return pl.pallas_call(
        _out_xattn_kernel,
        grid=(B, S // tm),
        in_specs=[
            row(D_MODEL), row(y_mla.shape[-1]), row(o_rec.shape[-1]), row(post.shape[-1]),
            pl.BlockSpec((1, HG_WIDTH), const),
            pl.BlockSpec((CONV_K, CONV_WIDTH), const),
            pl.BlockSpec((D_MODEL, D_MODEL), const),
            pl.BlockSpec((1, D_MODEL), const),
            pl.BlockSpec((D_MODEL, hw), const),
            pl.BlockSpec((1, X_HEAD_DIM), const),
            mem_spec, mem_spec,
            pl.BlockSpec((hw, D_MODEL), const),
        ],
        out_specs=row(D_MODEL),
        out_shape=jax.ShapeDtypeStruct((B, S, D_MODEL), F32),
        scratch_shapes=[pltpu.VMEM((tm + 8, CONV_WIDTH), F32)],
        compiler_params=pltpu.CompilerParams(
            dimension_semantics=("parallel", "arbitrary"), vmem_limit_bytes=VMEM_LIMIT),
        name="out_xattn",
    )(x, y_mla, o_rec, post, onorm_g, conv_w, w_out, g, w_xq, gq, k_mem, v_mem, w_xo)


def _mlp_kernel(x_ref, g_ref, wu_ref, wd_ref, o_ref):
    x = x_ref[0]
    h = (x * _rms(x) * g_ref[...]).astype(BF16)
    acc = x
    step = D_MODEL
    for c in range(D_FF // step):
        u = _dot(h, wu_ref[:, c * step:(c + 1) * step])
        a = jnp.square(jnp.maximum(u, 0.0)).astype(BF16)
        acc = acc + _dot(a, wd_ref[c * step:(c + 1) * step, :])
    o_ref[0] = acc


def _mlp(x, g, w_up, w_down):
    B, S, _ = x.shape
    tm = MLP_TILE
    const = lambda b, s: (0, 0)
    row = pl.BlockSpec((1, tm, D_MODEL), lambda b, s: (b, s, 0))
    once = pl.Buffered(1)
    return pl.pallas_call(
        _mlp_kernel,
        grid=(B, S // tm),
        in_specs=[row, pl.BlockSpec((1, D_MODEL), const),
                  pl.BlockSpec((D_MODEL, D_FF), const, pipeline_mode=once),
                  pl.BlockSpec((D_FF, D_MODEL), const, pipeline_mode=once)],
        out_specs=row,
        out_shape=jax.ShapeDtypeStruct((B, S, D_MODEL), F32),
        compiler_params=pltpu.CompilerParams(
            dimension_semantics=("parallel", "parallel"), vmem_limit_bytes=VMEM_LIMIT),
        name="mlp",
    )(x, g, w_up, w_down)


def _scatter_lanes(w_cols):
    idx = jnp.asarray(np.where(_LANE_MAP >= 0, _LANE_MAP, 0), jnp.int32)
    valid = jnp.asarray(_LANE_MAP >= 0)
    return jnp.where(valid, jnp.take(w_cols, idx, axis=-1), 0.0)


def _prep_layer(l, positions, w_in, w_uq, w_ukv, mla_qn_g, mla_kn_g):
    o_kr = MLA_Q_RANK + MLA_KV_RANK
    w = w_in[l]
    kr_src = jnp.concatenate(
        [jnp.zeros((D_MODEL, MLA_NOPE), F32), w[:, o_kr:o_kr + MLA_ROPE]], axis=1)
    w_in_p = jnp.concatenate(
        [w[:, :o_kr], _scatter_lanes(kr_src), w[:, o_kr + MLA_ROPE:]], axis=1).astype(BF16)

    def with_swapped_halves(w_slots):
        rank = w_slots.shape[0]
        both = jnp.concatenate([w_slots, jnp.roll(w_slots, HEAD_SLOT // 2, axis=-1)], axis=1)
        return both.reshape(rank, 2 * MLA_HEADS * HEAD_SLOT).astype(BF16)

    wq = w_uq[l].reshape(MLA_Q_RANK, MLA_HEADS, MLA_QK)
    w_uq_p = with_swapped_halves(_scatter_lanes(wq))

    wkv = w_ukv[l].reshape(MLA_KV_RANK, MLA_HEADS, MLA_NOPE + MLA_V)
    k_src = jnp.concatenate(
        [wkv[..., :MLA_NOPE], jnp.zeros((MLA_KV_RANK, MLA_HEADS, MLA_ROPE), F32)], axis=-1)
    w_uk_p = with_swapped_halves(_scatter_lanes(k_src))
    w_vt = wkv[..., MLA_NOPE:].reshape(MLA_KV_RANK, MLA_HEADS * MLA_V).T.astype(BF16)

    inv_freq = ROPE_BASE ** (-jnp.arange(0, MLA_ROPE, 2, dtype=F32) / MLA_ROPE)
    ang = positions.astype(F32)[:, None] * inv_freq[None, :]
    cos, sin = jnp.cos(ang), jnp.sin(ang)
    S = positions.shape[0]
    pad = HEAD_SLOT // 2 - ROPE_HALF
    c_tab = jnp.concatenate([cos, jnp.ones((S, pad), F32), cos, jnp.ones((S, pad), F32)], axis=1)
    s_tab = jnp.concatenate([-sin, jnp.zeros((S, pad), F32), sin, jnp.zeros((S, pad), F32)], axis=1)

    def tables(gain, scale):
        g = _scatter_lanes(gain)[None, :] * scale
        return c_tab * g, s_tab * jnp.roll(g, HEAD_SLOT // 2, axis=1)

    cq_t, sq_t = tables(mla_qn_g[l], math.log2(math.e) / math.sqrt(MLA_QK))
    ck_t, sk_t = tables(mla_kn_g[l], 1.0)
    return w_in_p, w_uq_p, w_uk_p, w_vt, (cq_t, sq_t, ck_t, sk_t)


def kernel(x, mem, positions, mix_norm_g, w_in, mla_q_norm_g, mla_kv_norm_g, w_uq, w_ukv,
           mla_qn_g, mla_kn_g, hgrn_lb_logits, hgrn_o_norm_g, conv_w, w_out,
           xattn_norm_g, mem_norm_g, w_xq, w_xkv, xq_norm_g, xk_norm_g, w_xo,
           mlp_norm_g, w_up, w_down):
    B, S, _ = x.shape
    L = w_in.shape[0]
    assert x.shape[2] == D_MODEL and L == DEPTH and mem.shape[1:] == (MEM_LEN, D_MODEL)
    assert all(S % t == 0 for t in (ROW_TILE, MLP_TILE, HG_ROWS, ATT_TQ, ATT_TK))
    assert ROW_TILE % ATT_TK == 0 and HG_ROWS % (HG_CHUNK * HG_GROUP) == 0
    assert MLA_HEADS % ATT_HEADS == 0 and ATT_HEADS % ATT_CHAINS == 0
    nq, nk = S // ATT_TQ, S // ATT_TK
    pq = positions.reshape(nq, ATT_TQ)
    pk = positions.reshape(nk, ATT_TK)
    vis = jnp.min(pk, axis=1)[None, :] <= jnp.max(pq, axis=1)[:, None]
    n_kv = jnp.max(jnp.where(vis, jnp.arange(1, nk + 1, dtype=jnp.int32)[None, :], 0),
                   axis=1).astype(jnp.int32)
    full = jnp.max(pk, axis=1)[None, :] <= jnp.min(pq, axis=1)[:, None]
    n_full = jnp.sum(jnp.cumprod(full.astype(jnp.int32), axis=1), axis=1).astype(jnp.int32)
    posq = positions.reshape(nq, 1, ATT_TQ)
    posk = positions.reshape(S, 1)

    k_mem, v_mem = _mem_kv(mem, mem_norm_g.reshape(L, 1, D_MODEL), w_xkv.astype(BF16),
                           xk_norm_g.reshape(L, 1, X_HEAD_DIM))
    for l in range(L):
        w_in_p, w_uq_p, w_uk_p, w_vt, tabs = _prep_layer(l, positions, w_in, w_uq, w_ukv,
                                                         mla_qn_g, mla_kn_g)
        lat, rec, post = _in_proj(x, mix_norm_g[l][None], w_in_p, hgrn_lb_logits, l)
        q, k, vt = _mla_qkv(lat, mla_q_norm_g[l][None], w_uq_p, mla_kv_norm_g[l][None],
                            w_uk_p, w_vt, tabs)
        score_bound = (math.log2(math.e) * math.sqrt(MLA_QK)
                       * jnp.max(jnp.abs(mla_qn_g[l])) * jnp.max(jnp.abs(mla_kn_g[l])))
        y_mla = lax.cond(score_bound <= ATT_PLAIN_MAX_SCORE,
                         functools.partial(_attention, online=False),
                         functools.partial(_attention, online=True),
                         q, k, vt, n_full, n_kv, posq, posk)
        o_rec = _hgrn(rec)
        x = _out_xattn(x, y_mla, o_rec, post, hgrn_o_norm_g[l][None], conv_w[l],
                       w_out[l].astype(BF16), xattn_norm_g[l][None], w_xq[l].astype(BF16),
                       xq_norm_g[l][None], k_mem, v_mem, w_xo[l].astype(BF16), l)
        x = _mlp(x, mlp_norm_g[l][None], w_up[l].astype(BF16), w_down[l].astype(BF16))
    return x
```

```python
import functools
import math

import numpy as np
import jax
import jax.numpy as jnp
from jax import lax
from jax.experimental import pallas as pl
from jax.experimental.pallas import tpu as pltpu

F32 = jnp.float32
BF16 = jnp.bfloat16

D_MODEL = 1024
DEPTH = 4
MEM_LEN = 256
EPS = 1e-6

MLA_HEADS = 8
MLA_NOPE = 64
MLA_ROPE = 32
MLA_V = 64
MLA_QK = MLA_NOPE + MLA_ROPE
MLA_Q_RANK = 384
MLA_KV_RANK = 256
ROPE_BASE = 10000.0

HG_HEADS = 4
HG_DK = 64
HG_CHUNK = 64
HG_SUB = 16
HG_GROUP = 8
HG_WIDTH = 256
CONV_WIDTH = 256
CONV_K = 3

X_HEADS = 4
X_HEAD_DIM = 128
D_FF = 4 * D_MODEL

LANES = 128
HEAD_SLOT = LANES
ROPE_HALF = MLA_ROPE // 2
N_IN_PAD = MLA_Q_RANK + MLA_KV_RANK + HEAD_SLOT + 4 * HG_WIDTH + 3 * CONV_WIDTH
HC_OFF = MLA_Q_RANK + MLA_KV_RANK + HEAD_SLOT

ROW_TILE = 1024
MLP_TILE = 1024
IN_PROJ_SUB = 256
XATTN_SUB = 256
ATT_TQ = 256
ATT_TK = 256
ATT_HEADS = 8
ATT_CHAINS = 4
ATT_UNROLLS = (4, 2)
HG_ROWS = 1024
MASK_VALUE = -1e30
ATT_PLAIN_MAX_SCORE = 64.0
EXP_CLAMP = 60.0
VMEM_LIMIT = 56 * 1024 * 1024


def _head_lane_map():
    m = -np.ones((HEAD_SLOT,), np.int64)
    m[0:ROPE_HALF] = MLA_NOPE + np.arange(ROPE_HALF)
    m[ROPE_HALF:64] = np.arange(64 - ROPE_HALF)
    m[64:64 + ROPE_HALF] = MLA_NOPE + ROPE_HALF + np.arange(ROPE_HALF)
    m[64 + ROPE_HALF:96] = (64 - ROPE_HALF) + np.arange(ROPE_HALF)
    return m


_LANE_MAP = _head_lane_map()


def _rms(x, n=None):
    n = x.shape[-1] if n is None else n
    return lax.rsqrt(jnp.sum(x * x, axis=-1, keepdims=True) * (1.0 / n) + EPS)


def _sigmoid(x):
    return 1.0 / (1.0 + jnp.exp(-x))


def _dot(a, b):
    return jnp.dot(a, b, preferred_element_type=F32)


def _dot_nt(a, b):
    return lax.dot_general(a, b, (((1,), (1,)), ((), ())), preferred_element_type=F32)


def _dot_tn(a, b):
    return lax.dot_general(a, b, (((0,), (0,)), ((), ())), preferred_element_type=F32)


def _in_proj_kernel(x_ref, g_ref, w_in_ref, lbl_ref, lat_out, rec_out, post_out, *, layer):
    W = HG_WIDTH
    subs = [slice(j * IN_PROJ_SUB, (j + 1) * IN_PROJ_SUB)
            for j in range(x_ref.shape[1] // IN_PROJ_SUB)]
    hs = []
    for r in subs:
        x = x_ref[0, r, :]
        hs.append((x * _rms(x) * g_ref[...]).astype(BF16))
    hcs = []
    for r, h in zip(subs, hs):
        lat_out[0, r, :] = _dot(h, w_in_ref[:, :HC_OFF])
        hcs.append(_dot(h, w_in_ref[:, HC_OFF:]))

    lg = lbl_ref[...]
    e = jnp.exp(lg - jnp.max(lg, axis=0, keepdims=True))
    soft = e / jnp.sum(e, axis=0, keepdims=True)
    lb = jnp.zeros((1, W), F32)
    for i in range(1, layer + 1):
        lb = lb + soft[i:i + 1, :]
    lb = jnp.maximum(lb, 0.0)
    log_lb = jnp.log(lb)
    log_1m = jnp.log1p(-lb)

    for rows, hc in zip(subs, hcs):
        zq, zf, vi, zg = (hc[:, i * W:(i + 1) * W] for i in range(4))
        cb, cc, cx = (hc[:, 4 * W + i * CONV_WIDTH:4 * W + (i + 1) * CONV_WIDTH]
                      for i in range(3))
        t = jnp.exp(-jnp.abs(zf))
        d = 1.0 + t
        r = 1.0 / d
        log_sig = jnp.minimum(zf, 0.0) - jnp.log(d)
        c2 = log_1m + log_sig
        log_f = jnp.maximum(log_lb, c2) + jnp.log(1.0 + jnp.exp(-jnp.abs(log_lb - c2)))
        rec_out[0, rows, 0:W] = zq * _sigmoid(zq)
        rec_out[0, rows, W:2 * W] = log_f
        rec_out[0, rows, 2 * W:3 * W] = (1.0 - lb) * jnp.where(zf >= 0.0, t * r, r)
        rec_out[0, rows, 3 * W:4 * W] = vi
        post_out[0, rows, 0:W] = zg * _sigmoid(zg)
        post_out[0, rows, W:W + CONV_WIDTH] = cb
        post_out[0, rows, W + CONV_WIDTH:W + 2 * CONV_WIDTH] = cc * cx


def _in_proj(x, g, w_in, lb_logits, layer):
    B, S, _ = x.shape
    tm = ROW_TILE
    const = lambda b, s: (0, 0)
    widths = (HC_OFF, 4 * HG_WIDTH, HG_WIDTH + 2 * CONV_WIDTH)
    return pl.pallas_call(
        functools.partial(_in_proj_kernel, layer=layer),
        grid=(B, S // tm),
        in_specs=[
            pl.BlockSpec((1, tm, D_MODEL), lambda b, s: (b, s, 0)),
            pl.BlockSpec((1, D_MODEL), const),
            pl.BlockSpec((D_MODEL, N_IN_PAD), const),
            pl.BlockSpec((DEPTH, HG_WIDTH), const),
        ],
        out_specs=[pl.BlockSpec((1, tm, w), lambda b, s: (b, s, 0)) for w in widths],
        out_shape=[jax.ShapeDtypeStruct((B, S, w), F32) for w in widths],
        compiler_params=pltpu.CompilerParams(
            dimension_semantics=("parallel", "parallel"), vmem_limit_bytes=VMEM_LIMIT),
        name="in_proj",
    )(x, g, w_in, lb_logits)


def _mla_qkv_kernel(lat_ref, gq_ref, w_uq_ref, gkv_ref, w_uk_ref, w_vt_ref,
                    cq_tab, sq_tab, ck_tab, sk_tab, q_out, k_out, vt_out):
    hw = MLA_HEADS * HEAD_SLOT
    subs = [slice(j * ATT_TK, (j + 1) * ATT_TK) for j in range(vt_out.shape[1])]
    lats = []
    for r in subs:
        cq = lat_ref[0, r, :MLA_Q_RANK]
        ckv = lat_ref[0, r, MLA_Q_RANK:MLA_Q_RANK + MLA_KV_RANK]
        lats.append(((cq * _rms(cq) * gq_ref[...]).astype(BF16),
                     (ckv * _rms(ckv) * gkv_ref[...]).astype(BF16)))
    qfs = [_dot(cqn, w_uq_ref[...]) for cqn, _ in lats]
    kfs = [_dot(ckvn, w_uk_ref[...]) for _, ckvn in lats]
    for j, (_, ckvn) in enumerate(lats):
        vt_out[0, j] = _dot_nt(w_vt_ref[...], ckvn).astype(BF16)
    for r, qf, kf in zip(subs, qfs, kfs):
        kr = lat_ref[0, r, MLA_Q_RANK + MLA_KV_RANK:]
        kr_sw = pltpu.roll(kr, HEAD_SLOT // 2, 1)
        cq_t, sq_t, ck_t, sk_t = cq_tab[r, :], sq_tab[r, :], ck_tab[r, :], sk_tab[r, :]
        for hd in range(MLA_HEADS):
            sl = slice(hd * HEAD_SLOT, (hd + 1) * HEAD_SLOT)
            sw = slice(hw + hd * HEAD_SLOT, hw + (hd + 1) * HEAD_SLOT)
            qh = qf[:, sl]
            qh = (qh * cq_t + qf[:, sw] * sq_t) * _rms(qh, MLA_QK)
            q_out[0, hd, :, r] = qh.T.astype(BF16)
            kh = kf[:, sl] + kr
            kh = (kh * ck_t + (kf[:, sw] + kr_sw) * sk_t) * _rms(kh, MLA_QK)
            k_out[0, hd, r, :] = kh.astype(BF16)


def _mla_qkv(lat, gq, w_uq, gkv, w_uk, w_vt, tabs):
    B, S, _ = lat.shape
    tm = ROW_TILE
    vw = MLA_HEADS * MLA_V
    const = lambda b, s: (0, 0)
    tab_spec = pl.BlockSpec((tm, HEAD_SLOT), lambda b, s: (s, 0))
    return pl.pallas_call(
        _mla_qkv_kernel,
        grid=(B, S // tm),
        in_specs=[
            pl.BlockSpec((1, tm, HC_OFF), lambda b, s: (b, s, 0)),
            pl.BlockSpec((1, MLA_Q_RANK), const),
            pl.BlockSpec((MLA_Q_RANK, 2 * MLA_HEADS * HEAD_SLOT), const),
            pl.BlockSpec((1, MLA_KV_RANK), const),
            pl.BlockSpec((MLA_KV_RANK, 2 * MLA_HEADS * HEAD_SLOT), const),
            pl.BlockSpec((vw, MLA_KV_RANK), const),
            tab_spec, tab_spec, tab_spec, tab_spec,
        ],
        out_specs=[
            pl.BlockSpec((1, MLA_HEADS, HEAD_SLOT, tm), lambda b, s: (b, 0, 0, s)),
            pl.BlockSpec((1, MLA_HEADS, tm, HEAD_SLOT), lambda b, s: (b, 0, s, 0)),
            pl.BlockSpec((1, tm // ATT_TK, vw, ATT_TK), lambda b, s: (b, s, 0, 0)),
        ],
        out_shape=[
            jax.ShapeDtypeStruct((B, MLA_HEADS, HEAD_SLOT, S), BF16),
            jax.ShapeDtypeStruct((B, MLA_HEADS, S, HEAD_SLOT), BF16),
            jax.ShapeDtypeStruct((B, S // ATT_TK, vw, ATT_TK), BF16),
        ],
        compiler_params=pltpu.CompilerParams(
            dimension_semantics=("parallel", "parallel"), vmem_limit_bytes=VMEM_LIMIT),
        name="mla_qkv",
    )(lat, gq, w_uq, gkv, w_uk, w_vt, *tabs)


def _attn_kernel(nfull_ref, nkv_ref, q_ref, k_ref, vt_ref, posq_ref, posk_ref, o_ref,
                 p_sc, acc_sc, *maybe_s_sc, online):
    qi = pl.program_id(2)
    n_full = nfull_ref[qi]
    n_kv = nkv_ref[qi]
    tq = q_ref.shape[3]
    nk = vt_ref.shape[1]
    heads = range(ATT_HEADS)
    qts = [q_ref[0, hd] for hd in heads]
    posq = posq_ref[0]

    groups = [list(heads)[i:i + ATT_CHAINS] for i in range(0, ATT_HEADS, ATT_CHAINS)]

    def score(ki, hd):
        off = pl.multiple_of(ki * ATT_TK, ATT_TK)
        return _dot(k_ref[0, hd, pl.ds(off, ATT_TK), :], qts[hd])

    def pv(ki, hd, p=None):
        p = p_sc[hd] if p is None else p
        return _dot(vt_ref[0, ki, hd * MLA_V:(hd + 1) * MLA_V, :], p)

    def issue(ki_scores, ki_pv):
        ss, pvs = {}, {}
        for g in groups:
            for hd in g:
                ss[hd] = score(ki_scores, hd)
            for hd in g:
                pvs[hd] = pv(ki_pv, hd)
        return ss, pvs

    def mask_of(ki):
        off = pl.multiple_of(ki * ATT_TK, ATT_TK)
        return posk_ref[pl.ds(off, ATT_TK), :] <= posq

    def step_plain(ki, carry, masked):
        ss, pvs = issue(ki, jnp.maximum(ki - 1, 0))
        if masked:
            mask = mask_of(ki)
        new = []
        for hd in heads:
            s = jnp.where(mask, ss[hd], MASK_VALUE) if masked else ss[hd]
            p = jnp.exp2(s)
            new.append(carry[hd] + jnp.sum(p, axis=0, keepdims=True))
            acc_sc[hd] = acc_sc[hd] + pvs[hd]
            p_sc[hd] = p.astype(BF16)
        return tuple(new)

    def multi_plain(j, carry, n_tiles, first):
        k0 = first + n_tiles * j
        ss, pvs = issue(k0, jnp.maximum(k0 - 1, 0))
        l_add = {hd: 0.0 for hd in heads}
        for hd in heads:
            acc_sc[hd] = acc_sc[hd] + pvs[hd]
        for u in range(1, n_tiles):
            ss_next = {}
            for g in groups:
                for hd in g:
                    ss_next[hd] = score(k0 + u, hd)
                for hd in g:
                    p = jnp.exp2(ss[hd])
                    l_add[hd] = l_add[hd] + jnp.sum(p, axis=0, keepdims=True)
                    acc_sc[hd] = acc_sc[hd] + pv(k0 + u - 1, hd, p.astype(BF16))
            ss = ss_next
        new = []
        for hd in heads:
            p = jnp.exp2(ss[hd])
            new.append(carry[hd] + l_add[hd] + jnp.sum(p, axis=0, keepdims=True))
            p_sc[hd] = p.astype(BF16)
        return tuple(new)

    def step_online(ki, carry, masked):
        (s_sc,) = maybe_s_sc
        s_next, pvs = issue(jnp.minimum(ki + 1, nk - 1), jnp.maximum(ki - 1, 0))
        if masked:
            mask = mask_of(ki)
        new = []
        for hd in heads:
            m_old, l_old = carry[hd]
            s = s_sc[hd]
            if masked:
                s = jnp.where(mask, s, MASK_VALUE)
            m_new = jnp.maximum(m_old, jnp.max(s, axis=0, keepdims=True))
            alpha = jnp.exp2(m_old - m_new)
            p = jnp.exp2(s - m_new)
            l_new = alpha * l_old + jnp.sum(p, axis=0, keepdims=True)
            acc_sc[hd] = alpha * (acc_sc[hd] + pvs[hd])
            p_sc[hd] = p.astype(BF16)
            new.append((m_new, l_new))
        for hd in heads:
            s_sc[hd] = s_next[hd]
        return tuple(new)

    p_sc[...] = jnp.zeros(p_sc.shape, BF16)
    acc_sc[...] = jnp.zeros(acc_sc.shape, F32)
    if online:
        for hd in heads:
            maybe_s_sc[0][hd] = score(0, hd)
        init = tuple((jnp.full((1, tq), MASK_VALUE, F32), jnp.zeros((1, tq), F32))
                     for _ in heads)
        step = step_online
    else:
        init = tuple(jnp.zeros((1, tq), F32) for _ in heads)
        step = step_plain
    last = jnp.maximum(n_kv - 1, 0)
    if online:
        carry = lax.fori_loop(0, n_full, functools.partial(step, masked=False), init)
        carry = lax.fori_loop(n_full, n_kv, functools.partial(step, masked=True), carry)
        outs = [(acc_sc[hd] + pv(last, hd)) / carry[hd][1] for hd in heads]
    else:
        n_unmasked = jnp.minimum(n_full, last)
        n_done = 0
        for n_tiles in ATT_UNROLLS:
            n_steps = (n_unmasked - n_done) // n_tiles
            init = lax.fori_loop(
                0, n_steps, functools.partial(multi_plain, n_tiles=n_tiles, first=n_done), init)
            n_done = n_done + n_tiles * n_steps
        carry = lax.fori_loop(n_done, n_unmasked, functools.partial(step, masked=False), init)
        carry = lax.fori_loop(n_unmasked, last, functools.partial(step, masked=True), carry)
        ss, pvs = issue(last, jnp.maximum(last - 1, 0))
        mask = mask_of(last)
        outs = []
        for hd in heads:
            p = jnp.exp2(jnp.where(mask, ss[hd], MASK_VALUE))
            l = carry[hd] + jnp.sum(p, axis=0, keepdims=True)
            outs.append((acc_sc[hd] + pvs[hd] + pv(last, hd, p.astype(BF16))) / l)
    o_ref[0] = jnp.concatenate(outs, axis=0).T.astype(BF16)


def _attention(q, k, vt, n_full, n_kv, posq, posk, online):
    B, H, _, S = q.shape
    tq = ATT_TQ
    nh = ATT_HEADS
    nk = S // ATT_TK
    scratch = [pltpu.VMEM((nh, ATT_TK, tq), BF16), pltpu.VMEM((nh, MLA_V, tq), F32)]
    if online:
        scratch.append(pltpu.VMEM((nh, ATT_TK, tq), F32))
    grid_spec = pltpu.PrefetchScalarGridSpec(
        num_scalar_prefetch=2,
        grid=(B, H // nh, S // tq),
        in_specs=[
            pl.BlockSpec((1, nh, HEAD_SLOT, tq), lambda b, hp, qi, *_: (b, hp, 0, qi)),
            pl.BlockSpec((1, nh, S, HEAD_SLOT), lambda b, hp, qi, *_: (b, hp, 0, 0)),
            pl.BlockSpec((1, nk, nh * MLA_V, ATT_TK), lambda b, hp, qi, *_: (b, 0, hp, 0)),
            pl.BlockSpec((1, 1, tq), lambda b, hp, qi, *_: (qi, 0, 0)),
            pl.BlockSpec((S, 1), lambda b, hp, qi, *_: (0, 0)),
        ],
        out_specs=pl.BlockSpec((1, tq, nh * MLA_V), lambda b, hp, qi, *_: (b, qi, hp)),
        scratch_shapes=scratch,
    )
    return pl.pallas_call(
        functools.partial(_attn_kernel, online=online),
        grid_spec=grid_spec,
        out_shape=jax.ShapeDtypeStruct((B, S, H * MLA_V), BF16),
        compiler_params=pltpu.CompilerParams(
            dimension_semantics=("parallel", "parallel", "arbitrary"),
            vmem_limit_bytes=VMEM_LIMIT),
        name="mla_attention_online" if online else "mla_attention",
    )(n_full, n_kv, q, k, vt, posq, posk)


def _hgrn_kernel(rec_ref, o_ref, st_sc):
    sb = pl.program_id(1)
    R = HG_ROWS
    C = HG_CHUNK
    W = HG_WIDTH

    @pl.when(sb == 0)
    def _():
        st_sc[...] = jnp.zeros(st_sc.shape, F32)

    ri = lax.broadcasted_iota(jnp.int32, (C, C), 0)
    ci = lax.broadcasted_iota(jnp.int32, (C, C), 1)
    sub_shift = HG_SUB.bit_length() - 1
    dk_shift = HG_DK.bit_length() - 1
    tri = jnp.where(ci <= ri, 1.0, 0.0)
    tri_blk = jnp.where((ci >> sub_shift) == (ri >> sub_shift), tri, 0.0)
    cum_mat = jnp.concatenate([tri, tri_blk], axis=0).astype(BF16)
    tt = ri & (HG_SUB - 1)
    lane_w = lax.broadcasted_iota(jnp.int32, (1, W), 1)
    head_masks = [jnp.where((lane_w >> dk_shift) == hh, 1.0, 0.0) for hh in range(HG_HEADS)]
    rw = lax.broadcasted_iota(jnp.int32, (W, W), 0)
    cw = lax.broadcasted_iota(jnp.int32, (W, W), 1)
    bd_mask = (rw >> dk_shift) == (cw >> dk_shift)
    ones_bd = jnp.where(bd_mask, 1.0, 0.0).astype(BF16)
    n_sub = C // HG_SUB

    def group(gi, carry):
        base = gi * (HG_GROUP * C)
        starts = [pl.multiple_of(base + j * C, C) for j in range(HG_GROUP)]
        chunks = range(HG_GROUP)
        gates, splits = [], []
        for r0 in starts:
            q = rec_ref[0, pl.ds(r0, C), 0:W]
            log_f = rec_ref[0, pl.ds(r0, C), W:2 * W]
            kk = rec_ref[0, pl.ds(r0, C), 2 * W:3 * W]
            vi = rec_ref[0, pl.ds(r0, C), 3 * W:4 * W]
            f_hi = log_f.astype(BF16)
            r1 = log_f - f_hi.astype(F32)
            f_mid = r1.astype(BF16)
            f_lo = (r1 - f_mid.astype(F32)).astype(BF16)
            gates.append((q, kk, vi.astype(BF16)))
            splits.append((f_hi, f_mid, f_lo))
        cums = [_dot(cum_mat, s[0]) + _dot(cum_mat, s[1]) + _dot(cum_mat, s[2]) for s in splits]

        def diag_direct(j, i):
            q, kk, vb = gates[j]
            rows = slice(i * HG_SUB, (i + 1) * HG_SUB)
            b_t, q_t, k_t = cums[j][:C][rows], q[rows], kk[rows]
            v_t = vb[rows].astype(F32)
            t_idx = lax.broadcasted_iota(jnp.int32, (HG_SUB, W), 0)
            slabs = []
            for s in range(HG_SUB):
                w = jnp.exp(jnp.minimum(b_t - b_t[s:s + 1, :], 0.0))
                slabs.append(jnp.where(t_idx >= s, q_t * k_t[s:s + 1, :] * w, 0.0))
            x = jnp.concatenate(slabs, axis=0).astype(BF16)
            a = _dot(x, ones_bd)
            o = a[0:HG_SUB] * v_t[0:1, :]
            for s in range(1, HG_SUB):
                o = o + a[s * HG_SUB:(s + 1) * HG_SUB] * v_t[s:s + 1, :]
            return o

        def finish(direct_diag):
            qbs, b_lasts, lhss, kds, kdls = [], [], [], [], []
            for j in chunks:
                q, kk, _ = gates[j]
                b = cums[j][:C]
                g = cums[j][C:]
                b_last = b[C - 1:C, :]
                rref = b - g
                qd = q * jnp.exp(g)
                qbs.append((q * jnp.exp(b)).astype(BF16))
                b_lasts.append(b_last)
                kdls.append((kk * jnp.exp(b_last - b)).astype(BF16))
                lhs_j, kd_j = [], []
                for i in range(n_sub):
                    rr = rref[i * HG_SUB:i * HG_SUB + 1, :]
                    kd_j.append((kk * jnp.exp(jnp.minimum(rr - b, EXP_CLAMP))).astype(BF16))
                    qi = qd[i * HG_SUB:(i + 1) * HG_SUB]
                    lhs_j.append(jnp.concatenate([qi * hm for hm in head_masks],
                                                 axis=0).astype(BF16))
                lhss.append(lhs_j)
                kds.append(kd_j)
            attn = [[_dot_nt(lhss[j][i], kds[j][i]) for i in range(n_sub)] for j in chunks]
            upds = [_dot_tn(gates[j][2], kdls[j]) for j in chunks]
            if direct_diag:
                attn = [[jnp.where(ci < i * HG_SUB, attn[j][i], 0.0).astype(BF16)
                         for i in range(n_sub)] for j in chunks]
            else:
                attn = [[jnp.where(ci <= i * HG_SUB + tt, attn[j][i], 0.0).astype(BF16)
                         for i in range(n_sub)] for j in chunks]
            pvs = [[_dot(attn[j][i], gates[j][2]) for i in range(n_sub)] for j in chunks]
            st = st_sc[...]
            for j in chunks:
                o_parts = []
                for i in range(n_sub):
                    pv = pvs[j][i]
                    oi = pv[0:HG_SUB] * head_masks[0]
                    for hh in range(1, HG_HEADS):
                        oi = oi + pv[hh * HG_SUB:(hh + 1) * HG_SUB] * head_masks[hh]
                    if direct_diag:
                        oi = oi + diag_direct(j, i)
                    o_parts.append(oi)
                o_inter = _dot_nt(qbs[j], st.astype(BF16))
                o_ref[0, pl.ds(starts[j], C), :] = o_inter + jnp.concatenate(o_parts, axis=0)
                st = st * jnp.exp(b_lasts[j]) + jnp.where(bd_mask, upds[j], 0.0)
            st_sc[...] = st
            return jnp.int32(0)

        g_min = cums[0][C:]
        for j in range(1, HG_GROUP):
            g_min = jnp.minimum(g_min, cums[j][C:])
        lax.cond(jnp.min(g_min) < -EXP_CLAMP,
                 functools.partial(finish, True), functools.partial(finish, False))
        return carry

    lax.fori_loop(0, R // (C * HG_GROUP), group, 0)


def _hgrn(rec):
    B, S, _ = rec.shape
    R = HG_ROWS
    return pl.pallas_call(
        _hgrn_kernel,
        grid=(B, S // R),
        in_specs=[pl.BlockSpec((1, R, 4 * HG_WIDTH), lambda b, s: (b, s, 0))],
        out_specs=pl.BlockSpec((1, R, HG_WIDTH), lambda b, s: (b, s, 0)),
        out_shape=jax.ShapeDtypeStruct((B, S, HG_WIDTH), F32),
        scratch_shapes=[pltpu.VMEM((HG_WIDTH, HG_WIDTH), F32)],
        compiler_params=pltpu.CompilerParams(
            dimension_semantics=("parallel", "arbitrary"), vmem_limit_bytes=VMEM_LIMIT),
        name="hgrn",
    )(rec)


def _mem_kv_kernel(mem_ref, g_ref, w_ref, gk_ref, k_out, v_out):
    m = mem_ref[0]
    mn = m * _rms(m) * g_ref[0]
    kv = _dot(mn.astype(BF16), w_ref[0])
    hw = X_HEADS * X_HEAD_DIM
    gk = gk_ref[0]
    for hd in range(X_HEADS):
        sl = slice(hd * X_HEAD_DIM, (hd + 1) * X_HEAD_DIM)
        kh = kv[:, sl]
        k_out[0, 0, :, sl] = (kh * _rms(kh) * gk).astype(BF16)
    v_out[0, 0] = kv[:, hw:].astype(BF16)


def _mem_kv(mem, mem_norm_g, w_xkv, xk_norm_g):
    B, M, _ = mem.shape
    L = w_xkv.shape[0]
    hw = X_HEADS * X_HEAD_DIM
    out_spec = pl.BlockSpec((1, 1, M, hw), lambda l, b: (l, b, 0, 0))
    return pl.pallas_call(
        _mem_kv_kernel,
        grid=(L, B),
        in_specs=[
            pl.BlockSpec((1, M, D_MODEL), lambda l, b: (b, 0, 0)),
            pl.BlockSpec((1, 1, D_MODEL), lambda l, b: (l, 0, 0)),
            pl.BlockSpec((1, D_MODEL, 2 * hw), lambda l, b: (l, 0, 0)),
            pl.BlockSpec((1, 1, X_HEAD_DIM), lambda l, b: (l, 0, 0)),
        ],
        out_specs=[out_spec, out_spec],
        out_shape=[jax.ShapeDtypeStruct((L, B, M, hw), BF16)] * 2,
        compiler_params=pltpu.CompilerParams(
            dimension_semantics=("parallel", "parallel"), vmem_limit_bytes=VMEM_LIMIT),
        name="mem_kv",
    )(mem, mem_norm_g, w_xkv, xk_norm_g)


def _out_xattn_kernel(x_ref, ya_ref, rec_ref, post_ref, onorm_ref, convw_ref, wo_ref, g_ref,
                      wq_ref, gq_ref, k_ref, v_ref, wxo_ref, o_ref, ubuf_sc):
    half = wo_ref.shape[0] // 2
    tm = x_ref.shape[1]
    W = HG_WIDTH
    subs = [slice(j * XATTN_SUB, (j + 1) * XATTN_SUB) for j in range(tm // XATTN_SUB)]

    rw = lax.broadcasted_iota(jnp.int32, (W, W), 0)
    cw = lax.broadcasted_iota(jnp.int32, (W, W), 1)
    dk_shift = HG_DK.bit_length() - 1
    ones_bd = jnp.where((rw >> dk_shift) == (cw >> dk_shift), 1.0, 0.0).astype(BF16)
    @pl.when(pl.program_id(1) == 0)
    def _():
        ubuf_sc[0:8, :] = jnp.zeros((8, CONV_WIDTH), F32)

    ubuf_sc[8:8 + tm, :] = post_ref[0, :, W + CONV_WIDTH:W + 2 * CONV_WIDTH]
    wc = convw_ref[...]
    ybs = []
    for r in subs:
        o = rec_ref[0, r, :]
        ms = _dot((o * o).astype(BF16), ones_bd) * (1.0 / HG_DK)
        y_hg = o * lax.rsqrt(ms + EPS) * onorm_ref[...] * post_ref[0, r, 0:W]
        u0, u1, u2 = (ubuf_sc[8 - d + r.start:8 - d + r.stop, :] for d in range(CONV_K))
        y_cv = post_ref[0, r, W:W + CONV_WIDTH] * (
            u2 * wc[0:1, :] + u1 * wc[1:2, :] + u0 * wc[2:3, :])
        ybs.append(jnp.concatenate([y_hg, y_cv], axis=1).astype(BF16))
    ubuf_sc[0:8, :] = ubuf_sc[tm:tm + 8, :]

    slots = [slice(hd * X_HEAD_DIM, (hd + 1) * X_HEAD_DIM) for hd in range(X_HEADS)]
    gq = gq_ref[...] * (1.0 / math.sqrt(X_HEAD_DIM))
    x1s = [x_ref[0, r, :] + _dot(ya_ref[0, r, :], wo_ref[:half, :])
           + _dot(yb, wo_ref[half:, :]) for r, yb in zip(subs, ybs)]
    hs = [(x1 * _rms(x1) * g_ref[...]).astype(BF16) for x1 in x1s]
    qs = [_dot(h, wq_ref[...]) for h in hs]
    qhs = [[(q[:, sl] * _rms(q[:, sl]) * gq).astype(BF16) for sl in slots] for q in qs]
    ss = [[_dot_nt(qh, k_ref[0, 0, :, sl]) for qh, sl in zip(row, slots)] for row in qhs]
    ps, ls = [], []
    for row in ss:
        p_row, l_row = [], []
        for s in row:
            p = jnp.exp(s - jnp.max(s, axis=1, keepdims=True))
            l_row.append(jnp.sum(p, axis=1, keepdims=True))
            p_row.append(p.astype(BF16))
        ps.append(p_row)
        ls.append(l_row)
    os_ = [jnp.concatenate([_dot(p, v_ref[0, 0, :, sl]) / l
                            for p, l, sl in zip(p_row, l_row, slots)], axis=1).astype(BF16)
           for p_row, l_row in zip(ps, ls)]
    for r, x1, o in zip(subs, x1s, os_):
        o_ref[0, r, :] = x1 + _dot(o, wxo_ref[...])


def _out_xattn(x, y_mla, o_rec, post, onorm_g, conv_w, w_out, g, w_xq, gq, k_mem, v_mem, w_xo,
               layer):
    B, S, _ = x.shape
    tm = ROW_TILE
    hw = X_HEADS * X_HEAD_DIM
    const = lambda b, s: (0, 0)
    row = lambda w: pl.BlockSpec((1, tm, w), lambda b, s: (b, s, 0))
    mem_spec = pl.BlockSpec((1, 1, MEM_LEN, hw), lambda b, s: (layer, b, 0, 0))
    return pl.pallas_call(
        _out_xattn_kernel,
        grid=(B, S // tm),
        in_specs=[
            row(D_MODEL), row(y_mla.shape[-1]), row(o_rec.shape[-1]), row(post.shape[-1]),
            pl.BlockSpec((1, HG_WIDTH), const),
            pl.BlockSpec((CONV_K, CONV_WIDTH), const),
            pl.BlockSpec((D_MODEL, D_MODEL), const),
            pl.BlockSpec((1, D_MODEL), const),
            pl.BlockSpec((D_MODEL, hw), const),
            pl.BlockSpec((1, X_HEAD_DIM), const),
            mem_spec, mem_spec,
            pl.BlockSpec((hw, D_MODEL), const),
        ],
        out_specs=row(D_MODEL),
        out_shape=jax.ShapeDtypeStruct((B, S, D_MODEL), F32),
        scratch_shapes=[pltpu.VMEM((tm + 8, CONV_WIDTH), F32)],
        compiler_params=pltpu.CompilerParams(
            dimension_semantics=("parallel", "arbitrary"), vmem_limit_bytes=VMEM_LIMIT),
        name="out_xattn",
    )(x, y_mla, o_rec, post, onorm_g, conv_w, w_out, g, w_xq, gq, k_mem, v_mem, w_xo)


def _mlp_kernel(x_ref, g_ref, wu_ref, wd_ref, o_ref):
    x = x_ref[0]
    h = (x * _rms(x) * g_ref[...]).astype(BF16)
    acc = x
    step = D_MODEL
    for c in range(D_FF // step):
        u = _dot(h, wu_ref[:, c * step:(c + 1) * step])
        a = jnp.square(jnp.maximum(u, 0.0)).astype(BF16)
        acc = acc + _dot(a, wd_ref[c * step:(c + 1) * step, :])
    o_ref[0] = acc


def _mlp(x, g, w_up, w_down):
    B, S, _ = x.shape
    tm = MLP_TILE
    const = lambda b, s: (0, 0)
    row = pl.BlockSpec((1, tm, D_MODEL), lambda b, s: (b, s, 0))
    once = pl.Buffered(1)
    return pl.pallas_call(
        _mlp_kernel,
        grid=(B, S // tm),
        in_specs=[row, pl.BlockSpec((1, D_MODEL), const),
                  pl.BlockSpec((D_MODEL, D_FF), const, pipeline_mode=once),
                  pl.BlockSpec((D_FF, D_MODEL), const, pipeline_mode=once)],
        out_specs=row,
        out_shape=jax.ShapeDtypeStruct((B, S, D_MODEL), F32),
        compiler_params=pltpu.CompilerParams(
            dimension_semantics=("parallel", "parallel"), vmem_limit_bytes=VMEM_LIMIT),
        name="mlp",
    )(x, g, w_up, w_down)


def _scatter_lanes(w_cols):
    idx = jnp.asarray(np.where(_LANE_MAP >= 0, _LANE_MAP, 0), jnp.int32)
    valid = jnp.asarray(_LANE_MAP >= 0)
    return jnp.where(valid, jnp.take(w_cols, idx, axis=-1), 0.0)


def _prep_layer(l, positions, w_in, w_uq, w_ukv, mla_qn_g, mla_kn_g):
    o_kr = MLA_Q_RANK + MLA_KV_RANK
    w = w_in[l]
    kr_src = jnp.concatenate(
        [jnp.zeros((D_MODEL, MLA_NOPE), F32), w[:, o_kr:o_kr + MLA_ROPE]], axis=1)
    w_in_p = jnp.concatenate(
        [w[:, :o_kr], _scatter_lanes(kr_src), w[:, o_kr + MLA_ROPE:]], axis=1).astype(BF16)

    def with_swapped_halves(w_slots):
        rank = w_slots.shape[0]
        both = jnp.concatenate([w_slots, jnp.roll(w_slots, HEAD_SLOT // 2, axis=-1)], axis=1)
        return both.reshape(rank, 2 * MLA_HEADS * HEAD_SLOT).astype(BF16)

    wq = w_uq[l].reshape(MLA_Q_RANK, MLA_HEADS, MLA_QK)
    w_uq_p = with_swapped_halves(_scatter_lanes(wq))

    wkv = w_ukv[l].reshape(MLA_KV_RANK, MLA_HEADS, MLA_NOPE + MLA_V)
    k_src = jnp.concatenate(
        [wkv[..., :MLA_NOPE], jnp.zeros((MLA_KV_RANK, MLA_HEADS, MLA_ROPE), F32)], axis=-1)
    w_uk_p = with_swapped_halves(_scatter_lanes(k_src))
    w_vt = wkv[..., MLA_NOPE:].reshape(MLA_KV_RANK, MLA_HEADS * MLA_V).T.astype(BF16)

    inv_freq = ROPE_BASE ** (-jnp.arange(0, MLA_ROPE, 2, dtype=F32) / MLA_ROPE)
    ang = positions.astype(F32)[:, None] * inv_freq[None, :]
    cos, sin = jnp.cos(ang), jnp.sin(ang)
    S = positions.shape[0]
    pad = HEAD_SLOT // 2 - ROPE_HALF
    c_tab = jnp.concatenate([cos, jnp.ones((S, pad), F32), cos, jnp.ones((S, pad), F32)], axis=1)
    s_tab = jnp.concatenate([-sin, jnp.zeros((S, pad), F32), sin, jnp.zeros((S, pad), F32)], axis=1)

    def tables(gain, scale):
        g = _scatter_lanes(gain)[None, :] * scale
        return c_tab * g, s_tab * jnp.roll(g, HEAD_SLOT // 2, axis=1)

    cq_t, sq_t = tables(mla_qn_g[l], math.log2(math.e) / math.sqrt(MLA_QK))
    ck_t, sk_t = tables(mla_kn_g[l], 1.0)
    return w_in_p, w_uq_p, w_uk_p, w_vt, (cq_t, sq_t, ck_t, sk_t)


def kernel(x, mem, positions, mix_norm_g, w_in, mla_q_norm_g, mla_kv_norm_g, w_uq, w_ukv,
           mla_qn_g, mla_kn_g, hgrn_lb_logits, hgrn_o_norm_g, conv_w, w_out,
           xattn_norm_g, mem_norm_g, w_xq, w_xkv, xq_norm_g, xk_norm_g, w_xo,
           mlp_norm_g, w_up, w_down):
    B, S, _ = x.shape
    L = w_in.shape[0]
    assert x.shape[2] == D_MODEL and L == DEPTH and mem.shape[1:] == (MEM_LEN, D_MODEL)
    assert all(S % t == 0 for t in (ROW_TILE, MLP_TILE, HG_ROWS, ATT_TQ, ATT_TK))
    assert ROW_TILE % ATT_TK == 0 and HG_ROWS % (HG_CHUNK * HG_GROUP) == 0
    assert MLA_HEADS % ATT_HEADS == 0 and ATT_HEADS % ATT_CHAINS == 0
    nq, nk = S // ATT_TQ, S // ATT_TK
    pq = positions.reshape(nq, ATT_TQ)
    pk = positions.reshape(nk, ATT_TK)
    vis = jnp.min(pk, axis=1)[None, :] <= jnp.max(pq, axis=1)[:, None]
    n_kv = jnp.max(jnp.where(vis, jnp.arange(1, nk + 1, dtype=jnp.int32)[None, :], 0),
                   axis=1).astype(jnp.int32)
    full = jnp.max(pk, axis=1)[None, :] <= jnp.min(pq, axis=1)[:, None]
    n_full = jnp.sum(jnp.cumprod(full.astype(jnp.int32), axis=1), axis=1).astype(jnp.int32)
    posq = positions.reshape(nq, 1, ATT_TQ)
    posk = positions.reshape(S, 1)

    k_mem, v_mem = _mem_kv(mem, mem_norm_g.reshape(L, 1, D_MODEL), w_xkv.astype(BF16),
                           xk_norm_g.reshape(L, 1, X_HEAD_DIM))
    for l in range(L):
        w_in_p, w_uq_p, w_uk_p, w_vt, tabs = _prep_layer(l, positions, w_in, w_uq, w_ukv,
                                                         mla_qn_g, mla_kn_g)
        lat, rec, post = _in_proj(x, mix_norm_g[l][None], w_in_p, hgrn_lb_logits, l)
        q, k, vt = _mla_qkv(lat, mla_q_norm_g[l][None], w_uq_p, mla_kv_norm_g[l][None],
                            w_uk_p, w_vt, tabs)
        score_bound = (math.log2(math.e) * math.sqrt(MLA_QK)
                       * jnp.max(jnp.abs(mla_qn_g[l])) * jnp.max(jnp.abs(mla_kn_g[l])))
        y_mla = lax.cond(score_bound <= ATT_PLAIN_MAX_SCORE,
                         functools.partial(_attention, online=False),
                         functools.partial(_attention, online=True),
                         q, k, vt, n_full, n_kv, posq, posk)
        o_rec = _hgrn(rec)
        x = _out_xattn(x, y_mla, o_rec, post, hgrn_o_norm_g[l][None], conv_w[l],
                       w_out[l].astype(BF16), xattn_norm_g[l][None], w_xq[l].astype(BF16),
                       xq_norm_g[l][None], k_mem, v_mem, w_xo[l].astype(BF16), l)
        x = _mlp(x, mlp_norm_g[l][None], w_up[l].astype(BF16), w_down[l].astype(BF16))
    return x
```
